```python
import jax, jax.numpy as jnp
from jax import lax
import numpy as np

D_MODEL = 1024
BATCH = 1
SEQ = 16384
DEPTH = 1

GRID_W = 64
CTX_LEN = 256
FOURIER_WIDTH = D_MODEL // 2
FOURIER_GROUPS = 4
FOURIER_GROUP_DIM = FOURIER_WIDTH // FOURIER_GROUPS
HGRN_WIDTH = D_MODEL // 2
HGRN_HEAD_DIM = 128
HGRN_HEADS = HGRN_WIDTH // HGRN_HEAD_DIM
CHUNK = 64
N_BRANCHES = 2
D_FF = 4 * D_MODEL
N_MOD = 6
POS_BASE = 10000.0
LN_EPS = 1e-5
RMS_EPS = 1e-6
DEEPNORM_ALPHA = (2.0 * DEPTH) ** 0.25
DEEPNORM_BETA = (8.0 * DEPTH) ** -0.25
PROJ_SPLITS = (FOURIER_WIDTH, FOURIER_WIDTH + HGRN_WIDTH, FOURIER_WIDTH + 2 * HGRN_WIDTH,
               FOURIER_WIDTH + 3 * HGRN_WIDTH, FOURIER_WIDTH + 4 * HGRN_WIDTH, FOURIER_WIDTH + 5 * HGRN_WIDTH)
PROJ_WIDTH = FOURIER_WIDTH + 5 * HGRN_WIDTH + N_BRANCHES * D_MODEL

kernel_name = 'hybrid_fnet_hgrn2_adaln_deepnorm_block'


def _layernorm(x, g, b):
    xf = x.astype(jnp.float32)
    mu = jnp.mean(xf, axis=-1, keepdims=True)
    var = jnp.mean(jnp.square(xf - mu), axis=-1, keepdims=True)
    return ((xf - mu) * lax.rsqrt(var + LN_EPS) * g + b).astype(x.dtype)


def _modulate(x, shift, scale):
    return x * (1 + scale) + shift


def _sincos_2d(rows, cols, dim):
    quarter = dim // 4
    omega = 1.0 / (POS_BASE ** (jnp.arange(quarter, dtype=jnp.float32) / quarter))
    r = jnp.arange(rows, dtype=jnp.float32)[:, None] * omega
    cc = jnp.arange(cols, dtype=jnp.float32)[:, None] * omega
    er = jnp.concatenate([jnp.sin(r), jnp.cos(r)], axis=-1)
    ec = jnp.concatenate([jnp.sin(cc), jnp.cos(cc)], axis=-1)
    emb = jnp.concatenate([jnp.broadcast_to(er[:, None, :], (rows, cols, dim // 2)),
                           jnp.broadcast_to(ec[None, :, :], (rows, cols, dim // 2))], axis=-1)
    return emb.reshape(rows * cols, dim)


def _lower_bounds(logits):
    p = jax.nn.softmax(logits.astype(jnp.float32), axis=1)
    return jnp.cumsum(p, axis=1)[:, :DEPTH]


def _split_heads(t):
    B, T, _ = t.shape
    return t.reshape(B, T, HGRN_HEADS, HGRN_HEAD_DIM).transpose(0, 2, 1, 3)


def _chunk_scan(q, k, v, log_f, s0):
    B, H, T, DK = q.shape
    DV = v.shape[-1]
    n = T // CHUNK

    def to_chunks(t):
        return jnp.moveaxis(t.astype(jnp.float32).reshape(B, H, n, CHUNK, t.shape[-1]), 2, 0)

    mask = jnp.tril(jnp.ones((CHUNK, CHUNK), dtype=bool))[:, :, None]

    def step(S, inp):
        qc, kc, vc, gc = inp
        b = jnp.cumsum(gc, axis=2)
        o_inter = jnp.einsum('bhtk,bhkv->bhtv', qc * jnp.exp(b), S)
        rel = b[:, :, :, None, :] - b[:, :, None, :, :]
        decay = jnp.where(mask, jnp.exp(jnp.minimum(rel, 0.0)), 0.0)
        scores = jnp.einsum('bhtk,bhsk,bhtsk->bhts', qc, kc, decay)
        o_intra = jnp.einsum('bhts,bhsv->bhtv', scores, vc)
        b_last = b[:, :, -1:, :]
        S_new = (jnp.exp(b_last[:, :, 0, :])[..., None] * S
                 + jnp.einsum('bhsk,bhsv->bhkv', kc * jnp.exp(b_last - b), vc))
        return S_new, o_inter + o_intra

    S_T, o = lax.scan(step, s0, (to_chunks(q), to_chunks(k), to_chunks(v), to_chunks(log_f)))
    o = jnp.moveaxis(o, 0, 2).reshape(B, H, T, DV)
    return o.astype(v.dtype), S_T


def _hgrn_direction(q, v, f_pre, lb, s0):
    lb = lb.reshape(HGRN_HEADS, 1, HGRN_HEAD_DIM)
    f = lb + (1.0 - lb) * jax.nn.sigmoid(f_pre.astype(jnp.float32))
    return _chunk_scan(q, 1.0 - f, v, jnp.log(f), s0)


def _hgrn_branch(q, v, f_fwd, f_bwd, og, norm_g, lb_fwd, lb_bwd, s0_fwd, s0_bwd):
    qh = _split_heads(jax.nn.silu(q))
    vh = _split_heads(v)
    o_f, s_f = _hgrn_direction(qh, vh, _split_heads(f_fwd), lb_fwd, s0_fwd)
    o_b, s_b = _hgrn_direction(qh[:, :, ::-1], vh[:, :, ::-1], _split_heads(f_bwd)[:, :, ::-1], lb_bwd, s0_bwd)
    o = (o_f + o_b[:, :, ::-1]).astype(jnp.float32)
    o = o * lax.rsqrt(jnp.mean(jnp.square(o), axis=-1, keepdims=True) + RMS_EPS)
    B, H, T, dh = o.shape
    o = (o.transpose(0, 2, 1, 3).reshape(B, T, H * dh) * norm_g).astype(q.dtype)
    return o * jax.nn.silu(og), s_f, s_b


def _fourier_mix(u):
    B, T, _ = u.shape
    ug = u.astype(jnp.float32).reshape(B, T, FOURIER_GROUPS, FOURIER_GROUP_DIM)
    y = jnp.fft.fft2(ug, axes=(1, 3), norm='ortho').real
    return y.reshape(B, T, FOURIER_WIDTH).astype(u.dtype)


def _merge(u_four, o_hgrn, gate_pre, w_fp, w_hp, w_o, b_o):
    g_four, g_hgrn = jnp.split(jax.nn.sigmoid(gate_pre.astype(jnp.float32)).astype(u_four.dtype), N_BRANCHES, axis=-1)
    y = g_four * (_fourier_mix(u_four) @ w_fp) + g_hgrn * (o_hgrn @ w_hp)
    return y @ w_o + b_o


def _mlp(h, w1, b1, w2, b2):
    return jnp.square(jax.nn.relu(h @ w1 + b1)) @ w2 + b2


def setup_inputs(seed: int = 0) -> dict:
    key = jax.random.key(seed)
    ks = jax.random.split(key, 24)
    f32 = jnp.float32
    D = D_MODEL
    nrm = lambda k, shape, s: jax.random.normal(k, shape, f32) * s
    w_in = nrm(ks[4], (DEPTH, D, PROJ_WIDTH), D ** -0.5)
    v_lo, v_hi = PROJ_SPLITS[1], PROJ_SPLITS[2]
    w_in = w_in.at[:, :, v_lo:v_hi].multiply(DEEPNORM_BETA)
    return {
        'x': nrm(ks[0], (BATCH, SEQ, D), 1.0),
        'c': nrm(ks[1], (BATCH, D), 1.0),
        'ctx': nrm(ks[2], (BATCH, CTX_LEN, D), 1.0),
        'c_ctx': nrm(ks[3], (D,), 1.0),
        'ln_in_g': 1.0 + nrm(ks[5], (D,), 0.02),
        'ln_in_b': nrm(ks[6], (D,), 0.02),
        'w_ada': nrm(ks[7], (DEPTH, D, N_MOD * D), D ** -0.5),
        'b_ada': nrm(ks[8], (DEPTH, N_MOD * D), 0.02),
        'w_in': w_in,
        'b_in': nrm(ks[9], (DEPTH, PROJ_WIDTH), 0.02),
        'hgrn_lb_logits': nrm(ks[10], (2, DEPTH + 1, HGRN_WIDTH), 0.5),
        'hgrn_norm_g': 1.0 + nrm(ks[11], (DEPTH, HGRN_WIDTH), 0.02),
        'w_four_proj': nrm(ks[12], (DEPTH, FOURIER_WIDTH, D), FOURIER_WIDTH ** -0.5 * DEEPNORM_BETA),
        'w_hgrn_proj': nrm(ks[13], (DEPTH, HGRN_WIDTH, D), HGRN_WIDTH ** -0.5 * DEEPNORM_BETA),
        'w_out': nrm(ks[14], (DEPTH, D, D), D ** -0.5 * DEEPNORM_BETA),
        'b_out': nrm(ks[15], (DEPTH, D), 0.02),
        'w_mlp1': nrm(ks[16], (DEPTH, D, D_FF), D ** -0.5 * DEEPNORM_BETA),
        'b_mlp1': nrm(ks[17], (DEPTH, D_FF), 0.02),
        'w_mlp2': nrm(ks[18], (DEPTH, D_FF, D), D_FF ** -0.5 * DEEPNORM_BETA),
        'b_mlp2': nrm(ks[19], (DEPTH, D), 0.02),
        'ln_post_g': 1.0 + nrm(ks[20], (DEPTH, 2, D), 0.02),
        'ln_post_b': nrm(ks[21], (DEPTH, 2, D), 0.02),
    }


def reference(x, c, ctx, c_ctx, ln_in_g, ln_in_b, w_ada, b_ada, w_in, b_in, hgrn_lb_logits, hgrn_norm_g,
              w_four_proj, w_hgrn_proj, w_out, b_out, w_mlp1, b_mlp1, w_mlp2, b_mlp2, ln_post_g, ln_post_b):
    B, T, D = x.shape
    rows = T // GRID_W
    pos = _sincos_2d(rows, GRID_W, D).astype(x.dtype)
    xl = _layernorm(x + pos[None], ln_in_g, ln_in_b)
    xc = _layernorm(ctx, ln_in_g, ln_in_b)
    lbs = _lower_bounds(hgrn_lb_logits)
    s_zero = jnp.zeros((B, HGRN_HEADS, HGRN_HEAD_DIM, HGRN_HEAD_DIM), jnp.float32)

    for l in range(DEPTH):
        mod_l = jnp.split((jax.nn.silu(c) @ w_ada[l] + b_ada[l])[:, None, :], N_MOD, axis=-1)
        mod_c = jnp.split(jax.nn.silu(c_ctx) @ w_ada[l] + b_ada[l], N_MOD, axis=-1)
        lb_f, lb_b = lbs[0, l], lbs[1, l]

        hc = _modulate(xc, mod_c[0], mod_c[1])
        pc = jnp.split(hc @ w_in[l] + b_in[l], PROJ_SPLITS, axis=-1)
        oc_h, s_f, s_b = _hgrn_branch(pc[1], pc[2], pc[3], pc[4], pc[5], hgrn_norm_g[l], lb_f, lb_b, s_zero, s_zero)
        if l < DEPTH - 1:
            mix_c = _merge(pc[0], oc_h, pc[6], w_four_proj[l], w_hgrn_proj[l], w_out[l], b_out[l])
            xc = _layernorm(DEEPNORM_ALPHA * xc + mod_c[2] * mix_c, ln_post_g[l, 0], ln_post_b[l, 0])
            hc2 = _modulate(xc, mod_c[3], mod_c[4])
            xc = _layernorm(DEEPNORM_ALPHA * xc + mod_c[5] * _mlp(hc2, w_mlp1[l], b_mlp1[l], w_mlp2[l], b_mlp2[l]),
                            ln_post_g[l, 1], ln_post_b[l, 1])

        hl = _modulate(xl, mod_l[0], mod_l[1])
        pl = jnp.split(hl @ w_in[l] + b_in[l], PROJ_SPLITS, axis=-1)
        ol_h, _, _ = _hgrn_branch(pl[1], pl[2], pl[3], pl[4], pl[5], hgrn_norm_g[l], lb_f, lb_b, s_f, s_b)
        mix_l = _merge(pl[0], ol_h, pl[6], w_four_proj[l], w_hgrn_proj[l], w_out[l], b_out[l])
        xl = _layernorm(DEEPNORM_ALPHA * xl + mod_l[2] * mix_l, ln_post_g[l, 0], ln_post_b[l, 0])
        hl2 = _modulate(xl, mod_l[3], mod_l[4])
        xl = _layernorm(DEEPNORM_ALPHA * xl + mod_l[5] * _mlp(hl2, w_mlp1[l], b_mlp1[l], w_mlp2[l], b_mlp2[l]),
                        ln_post_g[l, 1], ln_post_b[l, 1])
    return xl
```

```python
import functools

import numpy as np
import jax
import jax.numpy as jnp
from jax import lax
from jax.experimental import pallas as pl
from jax.experimental.pallas import tpu as pltpu

F32 = jnp.float32
BF16 = jnp.bfloat16

GRID_W = 64
N_GROUPS = 4
GROUP_DIM = 128
N_HEADS = 4
HEAD_DIM = 128
POS_BASE = 10000.0
LN_EPS = 1e-5
RMS_EPS = 1e-6
DEPTH = 1
ALPHA = (2.0 * DEPTH) ** 0.25

RADIX = 128
HGRN_CHUNK = 64
VMEM_LIMIT_BYTES = 56 * 1024 * 1024


def _cparams(n_axes=1):
    return pltpu.CompilerParams(dimension_semantics=("arbitrary",) * n_axes,
                                vmem_limit_bytes=VMEM_LIMIT_BYTES)


def _const_spec(shape):
    nd = len(shape)
    return pl.BlockSpec(shape, lambda *_: (0,) * nd)


def _sigmoid(x):
    return 1.0 / (1.0 + jnp.exp(-x))


def _layernorm(x, g, b):
    mu = jnp.mean(x, axis=-1, keepdims=True)
    xc = x - mu
    var = jnp.mean(xc * xc, axis=-1, keepdims=True)
    return xc * lax.rsqrt(var + LN_EPS) * g + b


def _dot(a, b):
    return jnp.dot(a, b, preferred_element_type=F32)


def _dot_nt(a, b):
    return lax.dot_general(a, b, (((1,), (1,)), ((), ())), preferred_element_type=F32)


def _dot_tn(a, b):
    return lax.dot_general(a, b, (((0,), (0,)), ((), ())), preferred_element_type=F32)


def _mod_kernel(c_ref, w_ref, b_ref, o_ref):
    cs = c_ref[...]
    s = cs * _sigmoid(cs)
    o_ref[...] = jnp.dot(s, w_ref[...], precision=lax.Precision.HIGHEST,
                         preferred_element_type=F32) + b_ref[...]


def _mod_vectors(cc, w_ada, b_ada):
    rows, d = cc.shape
    n = w_ada.shape[1]
    tn = 1536
    return pl.pallas_call(
        _mod_kernel,
        grid=(n // tn,),
        in_specs=[_const_spec((rows, d)),
                  pl.BlockSpec((d, tn), lambda j: (0, j)),
                  pl.BlockSpec((1, tn), lambda j: (0, j))],
        out_specs=pl.BlockSpec((rows, tn), lambda j: (0, j)),
        out_shape=jax.ShapeDtypeStruct((rows, n), F32),
        compiler_params=_cparams(),
        name="mod",
    )(cc, w_ada, b_ada)


def _ln_in_modulated(x_ref, er_ref, ec_ref, lng_ref, lnb_ref, mod_ref, tm):
    x = x_ref[...]
    half = x.shape[1] // 2
    nrow = tm // GRID_W
    left = jnp.concatenate(
        [jnp.broadcast_to(er_ref[r:r + 1, :], (GRID_W, half)) for r in range(nrow)], axis=0)
    right = jnp.concatenate([ec_ref[...]] * nrow, axis=0)
    xp = jnp.concatenate([x[:, :half] + left, x[:, half:] + right], axis=1)
    xl = _layernorm(xp, lng_ref[...], lnb_ref[...])
    hl = xl * (1.0 + mod_ref[1:2, :]) + mod_ref[0:1, :]
    return xl, hl


def _inproj_kernel(x_ref, er_ref, ec_ref, lng_ref, lnb_ref, mod_ref, w_ref, b_ref, dft_ref,
                   l0_ref, l1_ref, xc_ref, xs_ref, q_ref, v_ref, g_ref, *, tm):
    _, hl = _ln_in_modulated(x_ref, er_ref, ec_ref, lng_ref, lnb_ref, mod_ref, tm)
    hb = hl.astype(BF16)
    w4 = N_GROUPS * GROUP_DIM
    u = (_dot(hb, w_ref[:, 0:w4]) + b_ref[:, 0:w4]).astype(BF16)
    dft = dft_ref[...].astype(BF16)
    for gi in range(N_GROUPS):
        sl = slice(gi * GROUP_DIM, (gi + 1) * GROUP_DIM)
        z = _dot(u[:, sl], dft)
        xc_ref[:, sl] = z[:, :GROUP_DIM].astype(BF16)
        xs_ref[:, sl] = z[:, GROUP_DIM:].astype(BF16)
    qp = _dot(hb, w_ref[:, w4:2 * w4]) + b_ref[:, w4:2 * w4]
    q_ref[...] = (qp * _sigmoid(qp)).astype(BF16)
    v_ref[...] = (_dot(hb, w_ref[:, 2 * w4:3 * w4]) + b_ref[:, 2 * w4:3 * w4]).astype(BF16)
    l0 = l0_ref[...]
    l1 = l1_ref[...]
    m = jnp.maximum(l0, l1)
    e0 = jnp.exp(l0 - m)
    lb = e0 / (e0 + jnp.exp(l1 - m))
    fp = _dot(hb, w_ref[:, 3 * w4:5 * w4]) + b_ref[:, 3 * w4:5 * w4]
    g_ref[...] = jnp.log(lb + (1.0 - lb) * _sigmoid(fp))


def _inproj(x, er, ec, lng, lnb, mod, w, b, dft, l0, l1, tm):
    t, d = x.shape
    w4 = N_GROUPS * GROUP_DIM
    nrow = tm // GRID_W
    row = lambda i: (i, 0)
    return pl.pallas_call(
        functools.partial(_inproj_kernel, tm=tm),
        grid=(t // tm,),
        in_specs=[pl.BlockSpec((tm, d), row),
                  pl.BlockSpec((nrow, d // 2), row),
                  _const_spec(ec.shape), _const_spec(lng.shape), _const_spec(lnb.shape),
                  _const_spec(mod.shape), _const_spec(w.shape), _const_spec(b.shape),
                  _const_spec(dft.shape), _const_spec(l0.shape), _const_spec(l1.shape)],
        out_specs=[pl.BlockSpec((tm, w4), row)] * 4 + [pl.BlockSpec((tm, 2 * w4), row)],
        out_shape=[jax.ShapeDtypeStruct((t, w4), BF16)] * 4
        + [jax.ShapeDtypeStruct((t, 2 * w4), F32)],
        compiler_params=_cparams(),
        name="inproj",
    )(x, er, ec, lng, lnb, mod, w, b, dft, l0, l1)


def _dft1_kernel(m_ref, xc_ref, xs_ref, a_ref):
    z = jnp.concatenate([xc_ref[...], xs_ref[...]], axis=0)
    a_ref[...] = _dot(m_ref[...].astype(BF16), z).astype(BF16)


def _dft1(mat, xc2, xs2, tn):
    r, n = xc2.shape
    col = lambda j: (0, j)
    return pl.pallas_call(
        _dft1_kernel,
        grid=(n // tn,),
        in_specs=[_const_spec(mat.shape), pl.BlockSpec((r, tn), col), pl.BlockSpec((r, tn), col)],
        out_specs=pl.BlockSpec((2 * r, tn), col),
        out_shape=jax.ShapeDtypeStruct((2 * r, n), BF16),
        compiler_params=_cparams(),
        name="dft1",
    )(mat, xc2, xs2)


DFT2_BATCH = 8


def _dft2_kernel(g_ref, a_ref, p_ref, o_ref):
    nb, r, w = a_ref.shape[1], a_ref.shape[2], a_ref.shape[3]
    ys = []
    for j in range(nb):
        a = jnp.concatenate([a_ref[0, j], a_ref[1, j]], axis=0)
        ys.append(_dot(g_ref[j].astype(BF16), a).astype(BF16))
    y = _dot(p_ref[...], jnp.concatenate(ys, axis=0))
    o_ref[...] = y.reshape(r, nb, w)


def _dft2(gmat, a4, perm):
    _, r, _, w = a4.shape
    nb = DFT2_BATCH
    return pl.pallas_call(
        _dft2_kernel,
        grid=(r // nb,),
        in_specs=[pl.BlockSpec((nb, r, 2 * r), lambda k: (k, 0, 0)),
                  pl.BlockSpec((2, nb, r, w), lambda k: (0, k, 0, 0)),
                  _const_spec(perm.shape)],
        out_specs=pl.BlockSpec((r, nb, w), lambda k: (0, k, 0)),
        out_shape=jax.ShapeDtypeStruct((r, r, w), F32),
        compiler_params=_cparams(),
        name="dft2",
    )(gmat, a4, perm)


def _hgrn_tables(L, reverse):
    nlev = int(np.log2(L))
    idx = np.arange(L)
    t = idx[:, None]
    i = idx[None, :]
    blocks = [(i >= t) if reverse else (i <= t)]
    for j in range(nlev):
        h = L >> (j + 1)
        mid = (t // (2 * h)) * (2 * h) + h
        upper = t >= mid
        if reverse:
            blk = np.where(upper, (i >= mid) & (i < t), (i >= t) & (i < mid))
        else:
            blk = np.where(upper, (i >= mid) & (i <= t), (i > t) & (i < mid))
        blocks.append(blk)
    blocks.append((i < t) if reverse else (i > t))
    return np.concatenate(blocks, axis=0).astype(np.float32), nlev


def _hgrn_kernel(q_ref, v_ref, g_ref, mall_ref, s0_ref, o_ref, sfin_ref, st_ref,
                 *, nchunk, L, nlev, reverse):
    i = pl.program_id(0)

    @pl.when(i == 0)
    def _():
        st_ref[...] = s0_ref[...]

    row = lax.broadcasted_iota(jnp.int32, (L, L), 0)
    col = lax.broadcasted_iota(jnp.int32, (L, L), 1)
    rowc = lax.broadcasted_iota(jnp.int32, (L, N_HEADS * HEAD_DIM), 0)
    pair_masks, query_rows = [], []
    for j in range(nlev):
        h = L >> (j + 1)
        same = (row // (2 * h)) == (col // (2 * h))
        row_up = (row // h) % 2 == 1
        col_up = (col // h) % 2 == 1
        if reverse:
            pair_masks.append(same & jnp.logical_not(row_up) & col_up)
            query_rows.append((rowc // h) % 2 == 0)
        else:
            pair_masks.append(same & row_up & jnp.logical_not(col_up))
            query_rows.append((rowc // h) % 2 == 1)
    mall = mall_ref[...]
    last = 0 if reverse else L - 1

    order = range(nchunk - 1, -1, -1) if reverse else range(nchunk)
    for c in order:
        rows = slice(c * L, (c + 1) * L)
        g = g_ref[rows, :]
        g_hi = g.astype(BF16)
        g_lo = (g - g_hi.astype(F32)).astype(BF16)
        ex = jnp.exp(_dot(mall, g_hi) + _dot(mall, g_lo))
        q = q_ref[rows, :].astype(F32)
        v = v_ref[rows, :]
        k = 1.0 - jnp.exp(g)
        e_cum = ex[0:L]
        qe = (q * e_cum).astype(BF16)
        ke = (k * ex[(nlev + 1) * L:(nlev + 2) * L]).astype(BF16)
        zs = [(jnp.where(query_rows[j], q, k) * ex[(j + 1) * L:(j + 2) * L]).astype(BF16)
              for j in range(nlev)]
        qk = q * k
        e_last = e_cum[last:last + 1, :]
        for hd in range(N_HEADS):
            sl = slice(hd * HEAD_DIM, (hd + 1) * HEAD_DIM)
            sc = jnp.zeros((L, L), F32)
            for j in range(nlev):
                zj = zs[j][:, sl]
                sc = jnp.where(pair_masks[j], _dot_nt(zj, zj), sc)
            st = st_ref[hd]
            vh = v[:, sl]
            o = _dot(sc.astype(BF16), vh) + _dot_nt(qe[:, sl], st.astype(BF16))
            o = o + jnp.sum(qk[:, sl], axis=-1, keepdims=True) * vh.astype(F32)
            o_ref[rows, sl] = o
            st_ref[hd] = st * e_last[:, sl] + _dot_tn(vh, ke[:, sl])

    @pl.when(i == pl.num_programs(0) - 1)
    def _():
        sfin_ref[...] = st_ref[...]


def _hgrn_scan(q, v, g, s0, reverse, tb):
    t, w = q.shape
    L = HGRN_CHUNK
    mall_np, nlev = _hgrn_tables(L, reverse)
    mall = jnp.asarray(mall_np, dtype=BF16)
    nblk = t // tb
    gcol = 1 if reverse else 0
    if reverse:
        row = lambda i: (nblk - 1 - i, 0)
        grow = lambda i: (nblk - 1 - i, gcol)
    else:
        row = lambda i: (i, 0)
        grow = lambda i: (i, gcol)
    return pl.pallas_call(
        functools.partial(_hgrn_kernel, nchunk=tb // L, L=L, nlev=nlev, reverse=reverse),
        grid=(nblk,),
        in_specs=[pl.BlockSpec((tb, w), row), pl.BlockSpec((tb, w), row),
                  pl.BlockSpec((tb, w), grow), _const_spec(mall.shape), _const_spec(s0.shape)],
        out_specs=[pl.BlockSpec((tb, w), row), _const_spec(s0.shape)],
        out_shape=[jax.ShapeDtypeStruct((t, w), F32), jax.ShapeDtypeStruct(s0.shape, F32)],
        scratch_shapes=[pltpu.VMEM(s0.shape, F32)],
        compiler_params=_cparams(),
        name="hgrn_bwd" if reverse else "hgrn_fwd",
    )(q, v, g, mall, s0)


def _merge_kernel(x_ref, er_ref, ec_ref, lng_ref, lnb_ref, mod_ref, wg_ref, bg_ref, of_ref, ob_ref,
                  ng_ref, four_ref, wfp_ref, whp_ref, wo_ref, bo_ref, pg_ref, pb_ref, o_ref, *, tm):
    xl, hl = _ln_in_modulated(x_ref, er_ref, ec_ref, lng_ref, lnb_ref, mod_ref, tm)
    hb = hl.astype(BF16)
    w4 = N_HEADS * HEAD_DIM
    d = x_ref.shape[1]
    og = _dot(hb, wg_ref[:, 0:w4]) + bg_ref[:, 0:w4]
    o = of_ref[...] + ob_ref[...]
    parts = []
    for hd in range(N_HEADS):
        oh = o[:, hd * HEAD_DIM:(hd + 1) * HEAD_DIM]
        ms = jnp.mean(oh * oh, axis=-1, keepdims=True)
        parts.append(oh * lax.rsqrt(ms + RMS_EPS))
    on = jnp.concatenate(parts, axis=1) * ng_ref[...]
    oh = (on * (og * _sigmoid(og))).astype(BF16)
    g_four = _sigmoid(_dot(hb, wg_ref[:, w4:w4 + d]) + bg_ref[:, w4:w4 + d])
    y = g_four * _dot(four_ref[...].astype(BF16), wfp_ref[...])
    g_hgrn = _sigmoid(_dot(hb, wg_ref[:, w4 + d:w4 + 2 * d]) + bg_ref[:, w4 + d:w4 + 2 * d])
    y = y + g_hgrn * _dot(oh, whp_ref[...])
    mix = _dot(y.astype(BF16), wo_ref[...]) + bo_ref[...]
    o_ref[...] = _layernorm(ALPHA * xl + mod_ref[2:3, :] * mix, pg_ref[...], pb_ref[...])


def _merge(x, er, ec, lng, lnb, mod, wg, bg, o_f, o_b, ng, four, wfp, whp, wo, bo, pg, pb, tm):
    t, d = x.shape
    w4 = N_HEADS * HEAD_DIM
    nrow = tm // GRID_W
    row = lambda i: (i, 0)
    consts = [ec, lng, lnb, mod, wg, bg]
    consts2 = [wfp, whp, wo, bo, pg, pb]
    return pl.pallas_call(
        functools.partial(_merge_kernel, tm=tm),
        grid=(t // tm,),
        in_specs=[pl.BlockSpec((tm, d), row), pl.BlockSpec((nrow, d // 2), row)]
        + [_const_spec(a.shape) for a in consts]
        + [pl.BlockSpec((tm, w4), row), pl.BlockSpec((tm, w4), row), _const_spec(ng.shape),
           pl.BlockSpec((tm, w4), row)]
        + [_const_spec(a.shape) for a in consts2],
        out_specs=pl.BlockSpec((tm, d), row),
        out_shape=jax.ShapeDtypeStruct((t, d), F32),
        compiler_params=_cparams(),
        name="merge",
    )(x, er, *consts, o_f, o_b, ng, four, *consts2)


def _mlp_kernel(x_ref, mod_ref, w1_ref, b1_ref, w2_ref, b2_ref, pg_ref, pb_ref, o_ref, *, nsplit):
    x1 = x_ref[...]
    hb = (x1 * (1.0 + mod_ref[4:5, :]) + mod_ref[3:4, :]).astype(BF16)
    dff = w1_ref.shape[1]
    cw = dff // nsplit
    acc = jnp.zeros(x1.shape, F32)
    for c in range(nsplit):
        a = jnp.maximum(_dot(hb, w1_ref[:, c * cw:(c + 1) * cw]) + b1_ref[:, c * cw:(c + 1) * cw], 0.0)
        acc = acc + _dot((a * a).astype(BF16), w2_ref[c * cw:(c + 1) * cw, :])
    m = acc + b2_ref[...]
    o_ref[...] = _layernorm(ALPHA * x1 + mod_ref[5:6, :] * m, pg_ref[...], pb_ref[...])


def _mlp(x1, mod, w1, b1, w2, b2, pg, pb, tm):
    t, d = x1.shape
    row = lambda i: (i, 0)
    consts = [mod, w1, b1, w2, b2, pg, pb]
    return pl.pallas_call(
        functools.partial(_mlp_kernel, nsplit=4),
        grid=(t // tm,),
        in_specs=[pl.BlockSpec((tm, d), row)] + [_const_spec(a.shape) for a in consts],
        out_specs=pl.BlockSpec((tm, d), row),
        out_shape=jax.ShapeDtypeStruct((t, d), F32),
        compiler_params=_cparams(),
        name="mlp",
    )(x1, *consts)


def _pos_tables(rows, cols, dim):
    quarter = dim // 4
    omega = 1.0 / (POS_BASE ** (jnp.arange(quarter, dtype=F32) / quarter))
    r = jnp.arange(rows, dtype=F32)[:, None] * omega
    cc = jnp.arange(cols, dtype=F32)[:, None] * omega
    er = jnp.concatenate([jnp.sin(r), jnp.cos(r)], axis=-1)
    ec = jnp.concatenate([jnp.sin(cc), jnp.cos(cc)], axis=-1)
    return er, ec


def _dft_constants(t):
    n = RADIX
    kn = np.outer(np.arange(n), np.arange(n)).astype(np.float64)
    c = np.cos(2.0 * np.pi * kn / n)
    s = np.sin(2.0 * np.pi * kn / n)
    chan = np.concatenate([c, s], axis=1)
    stage1 = np.block([[c, -s], [-s, -c]])
    k1 = np.arange(n)[:, None, None]
    k2 = np.arange(n)[None, :, None]
    n2 = np.arange(n)[None, None, :]
    theta = 2.0 * np.pi * ((n2 * (k1 + n * k2)) % t) / t
    scale = 1.0 / np.sqrt(float(t) * GROUP_DIM)
    stage2 = np.concatenate([np.cos(theta), np.sin(theta)], axis=2) * scale
    nb = DFT2_BATCH
    r_out = np.arange(n * nb)
    perm = (np.arange(n * nb)[None, :] == ((r_out % nb) * n + r_out // nb)[:, None])
    as_f32 = lambda a: jnp.asarray(a.astype(np.float32))
    return as_f32(chan), as_f32(stage1), as_f32(stage2), jnp.asarray(perm, dtype=BF16)


def kernel(x, c, ctx, c_ctx, ln_in_g, ln_in_b, w_ada, b_ada, w_in, b_in, hgrn_lb_logits, hgrn_norm_g,
           w_four_proj, w_hgrn_proj, w_out, b_out, w_mlp1, b_mlp1, w_mlp2, b_mlp2, ln_post_g, ln_post_b):
    B, T, D = x.shape
    assert B == 1 and T == RADIX * RADIX and T % GRID_W == 0
    TC = ctx.shape[1]
    w4 = N_HEADS * HEAD_DIM
    row2 = lambda a: a.reshape(1, -1)

    cc = jnp.zeros((8, D), F32).at[0].set(c[0]).at[1].set(c_ctx)
    mod = _mod_vectors(cc, w_ada[0], row2(b_ada[0]))
    mod_l = mod[0].reshape(6, D)
    mod_c = mod[1].reshape(6, D)

    er, ec = _pos_tables(T // GRID_W, GRID_W, D)
    lng, lnb = row2(ln_in_g), row2(ln_in_b)
    w_in_b = w_in[0].astype(BF16)
    b_in2 = row2(b_in[0])
    n_a = 5 * w4
    w_a, b_a = w_in_b[:, :n_a], b_in2[:, :n_a]
    w_g, b_g = w_in_b[:, n_a:], b_in2[:, n_a:]
    l0 = hgrn_lb_logits[:, 0, :].reshape(1, 2 * w4)
    l1 = hgrn_lb_logits[:, 1, :].reshape(1, 2 * w4)
    dft_chan, dft_s1, dft_s2, dft_perm = _dft_constants(T)

    zc = jnp.zeros((TC // GRID_W, D // 2), F32)
    _, _, qc, vc, gc = _inproj(ctx[0], zc, jnp.zeros_like(ec), lng, lnb, mod_c, w_a, b_a, dft_chan,
                               l0, l1, tm=TC)
    s_zero = jnp.zeros((N_HEADS, HEAD_DIM, HEAD_DIM), F32)
    _, s_f = _hgrn_scan(qc, vc, gc, s_zero, False, tb=TC)
    _, s_b = _hgrn_scan(qc, vc, gc, s_zero, True, tb=TC)

    xc, xs, q, v, g = _inproj(x[0], er, ec, lng, lnb, mod_l, w_a, b_a, dft_chan, l0, l1, tm=512)
    a = _dft1(dft_s1, xc.reshape(RADIX, RADIX * w4), xs.reshape(RADIX, RADIX * w4), tn=8192)
    four = _dft2(dft_s2, a.reshape(2, RADIX, RADIX, w4), dft_perm).reshape(T, w4)
    o_f, _ = _hgrn_scan(q, v, g, s_f, False, tb=256)
    o_b, _ = _hgrn_scan(q, v, g, s_b, True, tb=256)

    x1 = _merge(x[0], er, ec, lng, lnb, mod_l, w_g, b_g, o_f, o_b, row2(hgrn_norm_g[0]), four,
                w_four_proj[0].astype(BF16), w_hgrn_proj[0].astype(BF16), w_out[0].astype(BF16),
                row2(b_out[0]), row2(ln_post_g[0, 0]), row2(ln_post_b[0, 0]), tm=512)
    out = _mlp(x1, mod_l, w_mlp1[0].astype(BF16), row2(b_mlp1[0]), w_mlp2[0].astype(BF16),
               row2(b_mlp2[0]), row2(ln_post_g[0, 1]), row2(ln_post_b[0, 1]), tm=512)
    return out[None]
```

```python
import functools

import numpy as np
import jax
import jax.numpy as jnp
from jax import lax
from jax.experimental import pallas as pl
from jax.experimental.pallas import tpu as pltpu

F32 = jnp.float32
BF16 = jnp.bfloat16

GRID_W = 64
N_GROUPS = 4
GROUP_DIM = 128
N_HEADS = 4
HEAD_DIM = 128
POS_BASE = 10000.0
LN_EPS = 1e-5
RMS_EPS = 1e-6
DEPTH = 1
ALPHA = (2.0 * DEPTH) ** 0.25

RADIX = 128
HGRN_CHUNK = 64
SAFE_LOG_DECAY = 80.0
VMEM_LIMIT_BYTES = 56 * 1024 * 1024


def _cparams(n_axes=1):
    return pltpu.CompilerParams(dimension_semantics=("arbitrary",) * n_axes,
                                vmem_limit_bytes=VMEM_LIMIT_BYTES)


def _const_spec(shape):
    nd = len(shape)
    return pl.BlockSpec(shape, lambda *_: (0,) * nd)


def _sigmoid(x):
    return 1.0 / (1.0 + jnp.exp(-x))


def _layernorm(x, g, b):
    mu = jnp.mean(x, axis=-1, keepdims=True)
    xc = x - mu
    var = jnp.mean(xc * xc, axis=-1, keepdims=True)
    return xc * lax.rsqrt(var + LN_EPS) * g + b


def _dot(a, b):
    return jnp.dot(a, b, preferred_element_type=F32)


def _dot_nt(a, b):
    return lax.dot_general(a, b, (((1,), (1,)), ((), ())), preferred_element_type=F32)


def _dot_tn(a, b):
    return lax.dot_general(a, b, (((0,), (0,)), ((), ())), preferred_element_type=F32)


def _mod_kernel(c_ref, w_ref, b_ref, o_ref):
    cs = c_ref[...]
    s = cs * _sigmoid(cs)
    o_ref[...] = jnp.dot(s, w_ref[...], precision=lax.Precision.HIGHEST,
                         preferred_element_type=F32) + b_ref[...]


def _mod_vectors(cc, w_ada, b_ada):
    rows, d = cc.shape
    n = w_ada.shape[1]
    tn = 1536
    return pl.pallas_call(
        _mod_kernel,
        grid=(n // tn,),
        in_specs=[_const_spec((rows, d)),
                  pl.BlockSpec((d, tn), lambda j: (0, j)),
                  pl.BlockSpec((1, tn), lambda j: (0, j))],
        out_specs=pl.BlockSpec((rows, tn), lambda j: (0, j)),
        out_shape=jax.ShapeDtypeStruct((rows, n), F32),
        compiler_params=_cparams(),
        name="mod",
    )(cc, w_ada, b_ada)


def _ln_in_modulated(x_ref, er_ref, ec_ref, lng_ref, lnb_ref, mod_ref, tm):
    x = x_ref[...]
    half = x.shape[1] // 2
    nrow = tm // GRID_W
    left = jnp.concatenate(
        [jnp.broadcast_to(er_ref[r:r + 1, :], (GRID_W, half)) for r in range(nrow)], axis=0)
    right = jnp.concatenate([ec_ref[...]] * nrow, axis=0)
    xp = jnp.concatenate([x[:, :half] + left, x[:, half:] + right], axis=1)
    xl = _layernorm(xp, lng_ref[...], lnb_ref[...])
    hl = xl * (1.0 + mod_ref[1:2, :]) + mod_ref[0:1, :]
    return xl, hl


def _inproj_kernel(x_ref, er_ref, ec_ref, lng_ref, lnb_ref, mod_ref, w_ref, b_ref, dft_ref,
                   l0_ref, l1_ref, xc_ref, xs_ref, q_ref, v_ref, g_ref, *, tm):
    _, hl = _ln_in_modulated(x_ref, er_ref, ec_ref, lng_ref, lnb_ref, mod_ref, tm)
    hb = hl.astype(BF16)
    w4 = N_GROUPS * GROUP_DIM
    u = (_dot(hb, w_ref[:, 0:w4]) + b_ref[:, 0:w4]).astype(BF16)
    dft = dft_ref[...].astype(BF16)
    for gi in range(N_GROUPS):
        sl = slice(gi * GROUP_DIM, (gi + 1) * GROUP_DIM)
        z = _dot(u[:, sl], dft)
        xc_ref[:, sl] = z[:, :GROUP_DIM].astype(BF16)
        xs_ref[:, sl] = z[:, GROUP_DIM:].astype(BF16)
    qp = _dot(hb, w_ref[:, w4:2 * w4]) + b_ref[:, w4:2 * w4]
    q_ref[...] = (qp * _sigmoid(qp)).astype(BF16)
    v_ref[...] = (_dot(hb, w_ref[:, 2 * w4:3 * w4]) + b_ref[:, 2 * w4:3 * w4]).astype(BF16)
    l0 = l0_ref[...]
    l1 = l1_ref[...]
    m = jnp.maximum(l0, l1)
    e0 = jnp.exp(l0 - m)
    lb = e0 / (e0 + jnp.exp(l1 - m))
    fp = _dot(hb, w_ref[:, 3 * w4:5 * w4]) + b_ref[:, 3 * w4:5 * w4]
    g_ref[...] = jnp.log(lb + (1.0 - lb) * _sigmoid(fp))


def _inproj(x, er, ec, lng, lnb, mod, w, b, dft, l0, l1, tm):
    t, d = x.shape
    w4 = N_GROUPS * GROUP_DIM
    nrow = tm // GRID_W
    row = lambda i: (i, 0)
    return pl.pallas_call(
        functools.partial(_inproj_kernel, tm=tm),
        grid=(t // tm,),
        in_specs=[pl.BlockSpec((tm, d), row),
                  pl.BlockSpec((nrow, d // 2), row),
                  _const_spec(ec.shape), _const_spec(lng.shape), _const_spec(lnb.shape),
                  _const_spec(mod.shape), _const_spec(w.shape), _const_spec(b.shape),
                  _const_spec(dft.shape), _const_spec(l0.shape), _const_spec(l1.shape)],
        out_specs=[pl.BlockSpec((tm, w4), row)] * 4 + [pl.BlockSpec((tm, 2 * w4), row)],
        out_shape=[jax.ShapeDtypeStruct((t, w4), BF16)] * 4
        + [jax.ShapeDtypeStruct((t, 2 * w4), F32)],
        compiler_params=_cparams(),
        name="inproj",
    )(x, er, ec, lng, lnb, mod, w, b, dft, l0, l1)


def _dft1_kernel(m_ref, xc_ref, xs_ref, a_ref):
    z = jnp.concatenate([xc_ref[...], xs_ref[...]], axis=0)
    a_ref[...] = _dot(m_ref[...].astype(BF16), z).astype(BF16)


def _dft1(mat, xc2, xs2, tn):
    r, n = xc2.shape
    col = lambda j: (0, j)
    return pl.pallas_call(
        _dft1_kernel,
        grid=(n // tn,),
        in_specs=[_const_spec(mat.shape), pl.BlockSpec((r, tn), col), pl.BlockSpec((r, tn), col)],
        out_specs=pl.BlockSpec((2 * r, tn), col),
        out_shape=jax.ShapeDtypeStruct((2 * r, n), BF16),
        compiler_params=_cparams(),
        name="dft1",
    )(mat, xc2, xs2)


DFT2_BATCH = 8


def _dft2_kernel(g_ref, a_ref, p_ref, o_ref):
    nb, r, w = a_ref.shape[1], a_ref.shape[2], a_ref.shape[3]
    ys = []
    for j in range(nb):
        a = jnp.concatenate([a_ref[0, j], a_ref[1, j]], axis=0)
        ys.append(_dot(g_ref[j].astype(BF16), a).astype(BF16))
    y = _dot(p_ref[...], jnp.concatenate(ys, axis=0))
    o_ref[...] = y.reshape(r, nb, w)


def _dft2(gmat, a4, perm):
    _, r, _, w = a4.shape
    nb = DFT2_BATCH
    return pl.pallas_call(
        _dft2_kernel,
        grid=(r // nb,),
        in_specs=[pl.BlockSpec((nb, r, 2 * r), lambda k: (k, 0, 0)),
                  pl.BlockSpec((2, nb, r, w), lambda k: (0, k, 0, 0)),
                  _const_spec(perm.shape)],
        out_specs=pl.BlockSpec((r, nb, w), lambda k: (0, k, 0)),
        out_shape=jax.ShapeDtypeStruct((r, r, w), F32),
        compiler_params=_cparams(),
        name="dft2",
    )(gmat, a4, perm)


def _hgrn_tables(L, reverse):
    nlev = int(np.log2(L))
    idx = np.arange(L)
    t = idx[:, None]
    i = idx[None, :]
    blocks = [(i >= t) if reverse else (i <= t)]
    for j in range(nlev):
        h = L >> (j + 1)
        mid = (t // (2 * h)) * (2 * h) + h
        upper = t >= mid
        if reverse:
            blk = np.where(upper, (i >= mid) & (i < t), (i >= t) & (i < mid))
        else:
            blk = np.where(upper, (i >= mid) & (i <= t), (i > t) & (i < mid))
        blocks.append(blk)
    blocks.append((i < t) if reverse else (i > t))
    return np.concatenate(blocks, axis=0).astype(np.float32), nlev


def _split_hi_lo(g):
    hi = g.astype(BF16)
    return hi, (g - hi.astype(F32)).astype(BF16)


def _hgrn_chunk_exact(q_ref, v_ref, g_ref, o_ref, st_ref, rows, mall, pair_masks, query_rows,
                      L, nlev, reverse):
    last = 0 if reverse else L - 1
    g = g_ref[rows, :]
    g_hi, g_lo = _split_hi_lo(g)
    ex = jnp.exp(_dot(mall, g_hi) + _dot(mall, g_lo))
    q = q_ref[rows, :].astype(F32)
    v = v_ref[rows, :]
    k = 1.0 - jnp.exp(g)
    e_cum = ex[0:L]
    qe = (q * e_cum).astype(BF16)
    ke = (k * ex[(nlev + 1) * L:(nlev + 2) * L]).astype(BF16)
    zs = [(jnp.where(query_rows[j], q, k) * ex[(j + 1) * L:(j + 2) * L]).astype(BF16)
          for j in range(nlev)]
    qk = q * k
    e_last = e_cum[last:last + 1, :]
    for hd in range(N_HEADS):
        sl = slice(hd * HEAD_DIM, (hd + 1) * HEAD_DIM)
        sc = jnp.zeros((L, L), F32)
        for j in range(nlev):
            zj = zs[j][:, sl]
            sc = jnp.where(pair_masks[j], _dot_nt(zj, zj), sc)
        st = st_ref[hd]
        vh = v[:, sl]
        o = _dot(sc.astype(BF16), vh) + _dot_nt(qe[:, sl], st.astype(BF16))
        o = o + jnp.sum(qk[:, sl], axis=-1, keepdims=True) * vh.astype(F32)
        o_ref[rows, sl] = o
        st_ref[hd] = st * e_last[:, sl] + _dot_tn(vh, ke[:, sl])


def _hgrn_block_fast(q_ref, v_ref, g_ref, trib_ref, o_ref, st_ref, nchunk, L, reverse):
    half = L // 2
    tb = nchunk * L
    chunks = []
    for c in range(nchunk):
        base = c * L
        if reverse:
            chunks.append((slice(base, base + L), slice(base + half, base + L),
                           slice(base, base + half), base + half, base))
        else:
            chunks.append((slice(base, base + L), slice(base, base + half),
                           slice(base + half, base + L), base + half - 1, base + L - 1))
    g = g_ref[...]
    g_hi, g_lo = _split_hi_lo(g)
    b = _dot(trib_ref[...], g_hi) + _dot(trib_ref[...], g_lo)
    q = q_ref[...].astype(F32)
    v = v_ref[...]
    k = 1.0 - jnp.exp(g)
    e_b = jnp.exp(b)
    qe = (q * e_b).astype(BF16)
    c2 = [b[sec] - b[edge:edge + 1, :] for (_, _, sec, edge, _) in chunks]
    q2 = jnp.concatenate([q[ch[2]] * jnp.exp(c2[c]) for c, ch in enumerate(chunks)], axis=0)
    own = []
    for c, (_, fst, _, _, _) in enumerate(chunks):
        own += [c2[c], b[fst]] if reverse else [b[fst], c2[c]]
    kh = (k * jnp.exp(-jnp.concatenate(own, axis=0))).astype(BF16)
    tail = jnp.concatenate([b[last:last + 1, :] - b[rows] for (rows, _, _, _, last) in chunks], axis=0)
    ke = (k * jnp.exp(tail)).astype(BF16)
    lhs = jnp.concatenate([qe, q2.astype(BF16)], axis=0)

    row = lax.broadcasted_iota(jnp.int32, (tb, tb), 0)
    col = lax.broadcasted_iota(jnp.int32, (tb, tb), 1)
    valid = (row // L == col // L) & ((col >= row) if reverse else (col <= row))
    if reverse:
        use_near = (row % L < half) & (col % L < half)
    else:
        use_near = (row % L >= half) & (col % L >= half)

    for hd in range(N_HEADS):
        sl = slice(hd * HEAD_DIM, (hd + 1) * HEAD_DIM)
        s_all = _dot_nt(lhs[:, sl], kh[:, sl])
        far = s_all[0:tb]
        pieces = []
        for c, (_, fst, _, _, _) in enumerate(chunks):
            near_c = s_all[tb + c * half:tb + (c + 1) * half]
            pieces += [near_c, far[fst]] if reverse else [far[fst], near_c]
        near = jnp.concatenate(pieces, axis=0)
        sc = jnp.where(valid, jnp.where(use_near, near, far), 0.0).astype(BF16)
        vh = v[:, sl]
        o_intra = _dot(sc, vh)
        upd = [_dot_tn(vh[rows], ke[rows, sl]) for (rows, _, _, _, _) in chunks]
        st = st_ref[hd]
        o_inter = [None] * nchunk
        for c in (range(nchunk - 1, -1, -1) if reverse else range(nchunk)):
            rows, _, _, _, last = chunks[c]
            o_inter[c] = _dot_nt(qe[rows, sl], st.astype(BF16))
            st = st * e_b[last:last + 1, sl] + upd[c]
        st_ref[hd] = st
        o_ref[:, sl] = o_intra + jnp.concatenate(o_inter, axis=0)


def _hgrn_kernel(q_ref, v_ref, g_ref, mall_ref, trib_ref, s0_ref, o_ref, sfin_ref, st_ref,
                 *, nchunk, L, nlev, reverse):
    i = pl.program_id(0)

    @pl.when(i == 0)
    def _():
        st_ref[...] = s0_ref[...]

    row = lax.broadcasted_iota(jnp.int32, (L, L), 0)
    col = lax.broadcasted_iota(jnp.int32, (L, L), 1)
    order = range(nchunk - 1, -1, -1) if reverse else range(nchunk)
    half = L // 2

    g_all = g_ref[...]
    leaf_decay = jnp.sum(g_all.reshape(g_all.shape[0] // half, half, g_all.shape[1]), axis=1)
    safe = jnp.min(leaf_decay) >= -SAFE_LOG_DECAY

    @pl.when(safe)
    def _():
        _hgrn_block_fast(q_ref, v_ref, g_ref, trib_ref, o_ref, st_ref, nchunk, L, reverse)

    @pl.when(jnp.logical_not(safe))
    def _():
        rowc = lax.broadcasted_iota(jnp.int32, (L, N_HEADS * HEAD_DIM), 0)
        pair_masks, query_rows = [], []
        for j in range(nlev):
            h = L >> (j + 1)
            same = (row // (2 * h)) == (col // (2 * h))
            row_up = (row // h) % 2 == 1
            col_up = (col // h) % 2 == 1
            if reverse:
                pair_masks.append(same & jnp.logical_not(row_up) & col_up)
                query_rows.append((rowc // h) % 2 == 0)
            else:
                pair_masks.append(same & row_up & jnp.logical_not(col_up))
                query_rows.append((rowc // h) % 2 == 1)
        mall = mall_ref[...]
        for c in order:
            _hgrn_chunk_exact(q_ref, v_ref, g_ref, o_ref, st_ref, slice(c * L, (c + 1) * L),
                              mall, pair_masks, query_rows, L, nlev, reverse)

    @pl.when(i == pl.num_programs(0) - 1)
    def _():
        sfin_ref[...] = st_ref[...]


def _hgrn_scan(q, v, g, s0, reverse, tb):
    t, w = q.shape
    L = HGRN_CHUNK
    mall_np, nlev = _hgrn_tables(L, reverse)
    mall = jnp.asarray(mall_np, dtype=BF16)
    trib = jnp.asarray(np.kron(np.eye(tb // L, dtype=np.float32), mall_np[0:L]), dtype=BF16)
    nblk = t // tb
    gcol = 1 if reverse else 0
    if reverse:
        row = lambda i: (nblk - 1 - i, 0)
        grow = lambda i: (nblk - 1 - i, gcol)
    else:
        row = lambda i: (i, 0)
        grow = lambda i: (i, gcol)
    return pl.pallas_call(
        functools.partial(_hgrn_kernel, nchunk=tb // L, L=L, nlev=nlev, reverse=reverse),
        grid=(nblk,),
        in_specs=[pl.BlockSpec((tb, w), row), pl.BlockSpec((tb, w), row),
                  pl.BlockSpec((tb, w), grow), _const_spec(mall.shape), _const_spec(trib.shape),
                  _const_spec(s0.shape)],
        out_specs=[pl.BlockSpec((tb, w), row), _const_spec(s0.shape)],
        out_shape=[jax.ShapeDtypeStruct((t, w), F32), jax.ShapeDtypeStruct(s0.shape, F32)],
        scratch_shapes=[pltpu.VMEM(s0.shape, F32)],
        compiler_params=_cparams(),
        name="hgrn_bwd" if reverse else "hgrn_fwd",
    )(q, v, g, mall, trib, s0)


def _merge_kernel(x_ref, er_ref, ec_ref, lng_ref, lnb_ref, mod_ref, wg_ref, bg_ref, of_ref, ob_ref,
                  ng_ref, four_ref, wfp_ref, whp_ref, wo_ref, bo_ref, pg_ref, pb_ref, o_ref, *, tm):
    xl, hl = _ln_in_modulated(x_ref, er_ref, ec_ref, lng_ref, lnb_ref, mod_ref, tm)
    hb = hl.astype(BF16)
    w4 = N_HEADS * HEAD_DIM
    d = x_ref.shape[1]
    og = _dot(hb, wg_ref[:, 0:w4]) + bg_ref[:, 0:w4]
    o = of_ref[...] + ob_ref[...]
    parts = []
    for hd in range(N_HEADS):
        oh = o[:, hd * HEAD_DIM:(hd + 1) * HEAD_DIM]
        ms = jnp.mean(oh * oh, axis=-1, keepdims=True)
        parts.append(oh * lax.rsqrt(ms + RMS_EPS))
    on = jnp.concatenate(parts, axis=1) * ng_ref[...]
    oh = (on * (og * _sigmoid(og))).astype(BF16)
    g_four = _sigmoid(_dot(hb, wg_ref[:, w4:w4 + d]) + bg_ref[:, w4:w4 + d])
    y = g_four * _dot(four_ref[...].astype(BF16), wfp_ref[...])
    g_hgrn = _sigmoid(_dot(hb, wg_ref[:, w4 + d:w4 + 2 * d]) + bg_ref[:, w4 + d:w4 + 2 * d])
    y = y + g_hgrn * _dot(oh, whp_ref[...])
    mix = _dot(y.astype(BF16), wo_ref[...]) + bo_ref[...]
    o_ref[...] = _layernorm(ALPHA * xl + mod_ref[2:3, :] * mix, pg_ref[...], pb_ref[...])


def _merge(x, er, ec, lng, lnb, mod, wg, bg, o_f, o_b, ng, four, wfp, whp, wo, bo, pg, pb, tm):
    t, d = x.shape
    w4 = N_HEADS * HEAD_DIM
    nrow = tm // GRID_W
    row = lambda i: (i, 0)
    consts = [ec, lng, lnb, mod, wg, bg]
    consts2 = [wfp, whp, wo, bo, pg, pb]
    return pl.pallas_call(
        functools.partial(_merge_kernel, tm=tm),
        grid=(t // tm,),
        in_specs=[pl.BlockSpec((tm, d), row), pl.BlockSpec((nrow, d // 2), row)]
        + [_const_spec(a.shape) for a in consts]
        + [pl.BlockSpec((tm, w4), row), pl.BlockSpec((tm, w4), row), _const_spec(ng.shape),
           pl.BlockSpec((tm, w4), row)]
        + [_const_spec(a.shape) for a in consts2],
        out_specs=pl.BlockSpec((tm, d), row),
        out_shape=jax.ShapeDtypeStruct((t, d), F32),
        compiler_params=_cparams(),
        name="merge",
    )(x, er, *consts, o_f, o_b, ng, four, *consts2)


def _mlp_kernel(x_ref, mod_ref, w1_ref, b1_ref, w2_ref, b2_ref, pg_ref, pb_ref, o_ref, *, nsplit):
    x1 = x_ref[...]
    hb = (x1 * (1.0 + mod_ref[4:5, :]) + mod_ref[3:4, :]).astype(BF16)
    dff = w1_ref.shape[1]
    cw = dff // nsplit
    acc = jnp.zeros(x1.shape, F32)
    for c in range(nsplit):
        a = jnp.maximum(_dot(hb, w1_ref[:, c * cw:(c + 1) * cw]) + b1_ref[:, c * cw:(c + 1) * cw], 0.0)
        acc = acc + _dot((a * a).astype(BF16), w2_ref[c * cw:(c + 1) * cw, :])
    m = acc + b2_ref[...]
    o_ref[...] = _layernorm(ALPHA * x1 + mod_ref[5:6, :] * m, pg_ref[...], pb_ref[...])


def _mlp(x1, mod, w1, b1, w2, b2, pg, pb, tm):
    t, d = x1.shape
    row = lambda i: (i, 0)
    consts = [mod, w1, b1, w2, b2, pg, pb]
    return pl.pallas_call(
        functools.partial(_mlp_kernel, nsplit=4),
        grid=(t // tm,),
        in_specs=[pl.BlockSpec((tm, d), row)] + [_const_spec(a.shape) for a in consts],
        out_specs=pl.BlockSpec((tm, d), row),
        out_shape=jax.ShapeDtypeStruct((t, d), F32),
        compiler_params=_cparams(),
        name="mlp",
    )(x1, *consts)


def _pos_tables(rows, cols, dim):
    quarter = dim // 4
    omega = 1.0 / (POS_BASE ** (jnp.arange(quarter, dtype=F32) / quarter))
    r = jnp.arange(rows, dtype=F32)[:, None] * omega
    cc = jnp.arange(cols, dtype=F32)[:, None] * omega
    er = jnp.concatenate([jnp.sin(r), jnp.cos(r)], axis=-1)
    ec = jnp.concatenate([jnp.sin(cc), jnp.cos(cc)], axis=-1)
    return er, ec


def _dft_constants(t):
    n = RADIX
    kn = np.outer(np.arange(n), np.arange(n)).astype(np.float64)
    c = np.cos(2.0 * np.pi * kn / n)
    s = np.sin(2.0 * np.pi * kn / n)
    chan = np.concatenate([c, s], axis=1)
    stage1 = np.block([[c, -s], [-s, -c]])
    k1 = np.arange(n)[:, None, None]
    k2 = np.arange(n)[None, :, None]
    n2 = np.arange(n)[None, None, :]
    theta = 2.0 * np.pi * ((n2 * (k1 + n * k2)) % t) / t
    scale = 1.0 / np.sqrt(float(t) * GROUP_DIM)
    stage2 = np.concatenate([np.cos(theta), np.sin(theta)], axis=2) * scale
    nb = DFT2_BATCH
    r_out = np.arange(n * nb)
    perm = (np.arange(n * nb)[None, :] == ((r_out % nb) * n + r_out // nb)[:, None])
    as_f32 = lambda a: jnp.asarray(a.astype(np.float32))
    return as_f32(chan), as_f32(stage1), as_f32(stage2), jnp.asarray(perm, dtype=BF16)


def kernel(x, c, ctx, c_ctx, ln_in_g, ln_in_b, w_ada, b_ada, w_in, b_in, hgrn_lb_logits, hgrn_norm_g,
           w_four_proj, w_hgrn_proj, w_out, b_out, w_mlp1, b_mlp1, w_mlp2, b_mlp2, ln_post_g, ln_post_b):
    B, T, D = x.shape
    assert B == 1 and T == RADIX * RADIX and T % GRID_W == 0
    TC = ctx.shape[1]
    w4 = N_HEADS * HEAD_DIM
    row2 = lambda a: a.reshape(1, -1)

    cc = jnp.zeros((8, D), F32).at[0].set(c[0]).at[1].set(c_ctx)
    mod = _mod_vectors(cc, w_ada[0], row2(b_ada[0]))
    mod_l = mod[0].reshape(6, D)
    mod_c = mod[1].reshape(6, D)

    er, ec = _pos_tables(T // GRID_W, GRID_W, D)
    lng, lnb = row2(ln_in_g), row2(ln_in_b)
    w_in_b = w_in[0].astype(BF16)
    b_in2 = row2(b_in[0])
    n_a = 5 * w4
    w_a, b_a = w_in_b[:, :n_a], b_in2[:, :n_a]
    w_g, b_g = w_in_b[:, n_a:], b_in2[:, n_a:]
    l0 = hgrn_lb_logits[:, 0, :].reshape(1, 2 * w4)
    l1 = hgrn_lb_logits[:, 1, :].reshape(1, 2 * w4)
    dft_chan, dft_s1, dft_s2, dft_perm = _dft_constants(T)

    zc = jnp.zeros((TC // GRID_W, D // 2), F32)
    _, _, qc, vc, gc = _inproj(ctx[0], zc, jnp.zeros_like(ec), lng, lnb, mod_c, w_a, b_a, dft_chan,
                               l0, l1, tm=TC)
    s_zero = jnp.zeros((N_HEADS, HEAD_DIM, HEAD_DIM), F32)
    _, s_f = _hgrn_scan(qc, vc, gc, s_zero, False, tb=TC)
    _, s_b = _hgrn_scan(qc, vc, gc, s_zero, True, tb=TC)

    xc, xs, q, v, g = _inproj(x[0], er, ec, lng, lnb, mod_l, w_a, b_a, dft_chan, l0, l1, tm=512)
    a = _dft1(dft_s1, xc.reshape(RADIX, RADIX * w4), xs.reshape(RADIX, RADIX * w4), tn=8192)
    four = _dft2(dft_s2, a.reshape(2, RADIX, RADIX, w4), dft_perm).reshape(T, w4)
    o_f, _ = _hgrn_scan(q, v, g, s_f, False, tb=256)
    o_b, _ = _hgrn_scan(q, v, g, s_b, True, tb=256)

    x1 = _merge(x[0], er, ec, lng, lnb, mod_l, w_g, b_g, o_f, o_b, row2(hgrn_norm_g[0]), four,
                w_four_proj[0].astype(BF16), w_hgrn_proj[0].astype(BF16), w_out[0].astype(BF16),
                row2(b_out[0]), row2(ln_post_g[0, 0]), row2(ln_post_b[0, 0]), tm=512)
    out = _mlp(x1, mod_l, w_mlp1[0].astype(BF16), row2(b_mlp1[0]), w_mlp2[0].astype(BF16),
               row2(b_mlp2[0]), row2(ln_post_g[0, 1]), row2(ln_post_b[0, 1]), tm=512)
    return out[None]
```

```python
import functools

import numpy as np
import jax
import jax.numpy as jnp
from jax import lax
from jax.experimental import pallas as pl
from jax.experimental.pallas import tpu as pltpu

F32 = jnp.float32
BF16 = jnp.bfloat16

GRID_W = 64
N_GROUPS = 4
GROUP_DIM = 128
N_HEADS = 4
HEAD_DIM = 128
POS_BASE = 10000.0
LN_EPS = 1e-5
RMS_EPS = 1e-6
DEPTH = 1
ALPHA = (2.0 * DEPTH) ** 0.25

RADIX = 128
HGRN_CHUNK = 64
SAFE_LOG_DECAY = 80.0
VMEM_LIMIT_BYTES = 56 * 1024 * 1024


def _cparams(n_axes=1):
    return pltpu.CompilerParams(dimension_semantics=("arbitrary",) * n_axes,
                                vmem_limit_bytes=VMEM_LIMIT_BYTES)


def _const_spec(shape):
    nd = len(shape)
    return pl.BlockSpec(shape, lambda *_: (0,) * nd)


def _sigmoid(x):
    return 1.0 / (1.0 + jnp.exp(-x))


def _layernorm(x, g, b):
    mu = jnp.mean(x, axis=-1, keepdims=True)
    xc = x - mu
    var = jnp.mean(xc * xc, axis=-1, keepdims=True)
    return xc * lax.rsqrt(var + LN_EPS) * g + b


def _dot(a, b):
    return jnp.dot(a, b, preferred_element_type=F32)


def _dot_nt(a, b):
    return lax.dot_general(a, b, (((1,), (1,)), ((), ())), preferred_element_type=F32)


def _dot_tn(a, b):
    return lax.dot_general(a, b, (((0,), (0,)), ((), ())), preferred_element_type=F32)


def _mod_kernel(c_ref, w_ref, b_ref, o_ref):
    cs = c_ref[...]
    s = cs * _sigmoid(cs)
    o_ref[...] = jnp.dot(s, w_ref[...], precision=lax.Precision.HIGHEST,
                         preferred_element_type=F32) + b_ref[...]


def _mod_vectors(cc, w_ada, b_ada):
    rows, d = cc.shape
    n = w_ada.shape[1]
    tn = 1536
    return pl.pallas_call(
        _mod_kernel,
        grid=(n // tn,),
        in_specs=[_const_spec((rows, d)),
                  pl.BlockSpec((d, tn), lambda j: (0, j)),
                  pl.BlockSpec((1, tn), lambda j: (0, j))],
        out_specs=pl.BlockSpec((rows, tn), lambda j: (0, j)),
        out_shape=jax.ShapeDtypeStruct((rows, n), F32),
        compiler_params=_cparams(),
        name="mod",
    )(cc, w_ada, b_ada)


def _ln_in_modulated(x_ref, er_ref, ec_ref, lng_ref, lnb_ref, mod_ref, tm):
    x = x_ref[...]
    half = x.shape[1] // 2
    nrow = tm // GRID_W
    left = jnp.concatenate(
        [jnp.broadcast_to(er_ref[r:r + 1, :], (GRID_W, half)) for r in range(nrow)], axis=0)
    right = jnp.concatenate([ec_ref[...]] * nrow, axis=0)
    xp = jnp.concatenate([x[:, :half] + left, x[:, half:] + right], axis=1)
    xl = _layernorm(xp, lng_ref[...], lnb_ref[...])
    hl = xl * (1.0 + mod_ref[1:2, :]) + mod_ref[0:1, :]
    return xl, hl


def _inproj_kernel(x_ref, er_ref, ec_ref, lng_ref, lnb_ref, mod_ref, w_ref, b_ref, dft_ref,
                   l0_ref, l1_ref, xc_ref, xs_ref, q_ref, v_ref, g_ref, *, tm):
    _, hl = _ln_in_modulated(x_ref, er_ref, ec_ref, lng_ref, lnb_ref, mod_ref, tm)
    hb = hl.astype(BF16)
    w4 = N_GROUPS * GROUP_DIM
    u = (_dot(hb, w_ref[:, 0:w4]) + b_ref[:, 0:w4]).astype(BF16)
    dft = dft_ref[...].astype(BF16)
    for gi in range(N_GROUPS):
        sl = slice(gi * GROUP_DIM, (gi + 1) * GROUP_DIM)
        z = _dot(u[:, sl], dft)
        xc_ref[gi] = z[:, :GROUP_DIM].astype(BF16)
        xs_ref[gi] = z[:, GROUP_DIM:].astype(BF16)
    qp = _dot(hb, w_ref[:, w4:2 * w4]) + b_ref[:, w4:2 * w4]
    q_ref[...] = (qp * _sigmoid(qp)).astype(BF16)
    v_ref[...] = (_dot(hb, w_ref[:, 2 * w4:3 * w4]) + b_ref[:, 2 * w4:3 * w4]).astype(BF16)
    l0 = l0_ref[...]
    l1 = l1_ref[...]
    m = jnp.maximum(l0, l1)
    e0 = jnp.exp(l0 - m)
    lb = e0 / (e0 + jnp.exp(l1 - m))
    fp = _dot(hb, w_ref[:, 3 * w4:5 * w4]) + b_ref[:, 3 * w4:5 * w4]
    g_ref[...] = jnp.log(lb + (1.0 - lb) * _sigmoid(fp))


def _inproj(x, er, ec, lng, lnb, mod, w, b, dft, l0, l1, tm):
    t, d = x.shape
    w4 = N_GROUPS * GROUP_DIM
    nrow = tm // GRID_W
    row = lambda i: (i, 0)
    return pl.pallas_call(
        functools.partial(_inproj_kernel, tm=tm),
        grid=(t // tm,),
        in_specs=[pl.BlockSpec((tm, d), row),
                  pl.BlockSpec((nrow, d // 2), row),
                  _const_spec(ec.shape), _const_spec(lng.shape), _const_spec(lnb.shape),
                  _const_spec(mod.shape), _const_spec(w.shape), _const_spec(b.shape),
                  _const_spec(dft.shape), _const_spec(l0.shape), _const_spec(l1.shape)],
        out_specs=[pl.BlockSpec((N_GROUPS, tm, GROUP_DIM), lambda i: (0, i, 0))] * 2
        + [pl.BlockSpec((tm, w4), row)] * 2 + [pl.BlockSpec((tm, 2 * w4), row)],
        out_shape=[jax.ShapeDtypeStruct((N_GROUPS, t, GROUP_DIM), BF16)] * 2
        + [jax.ShapeDtypeStruct((t, w4), BF16)] * 2 + [jax.ShapeDtypeStruct((t, 2 * w4), F32)],
        compiler_params=_cparams(),
        name="inproj",
    )(x, er, ec, lng, lnb, mod, w, b, dft, l0, l1)


DFT1_BATCH = 16


def _dft1_kernel(m_ref, xc_ref, xs_ref, a_ref, sx_ref, sa_ref):
    r, nb, w = xc_ref.shape
    sx_ref[0] = xc_ref[...].astype(F32).reshape(r * nb, w)
    sx_ref[1] = xs_ref[...].astype(F32).reshape(r * nb, w)
    cols = []
    for j in range(nb):
        cols.append(jnp.concatenate([sx_ref[0, pl.ds(j, r, stride=nb), :],
                                     sx_ref[1, pl.ds(j, r, stride=nb), :]], axis=0).astype(BF16))
    a = _dot(m_ref[...].astype(BF16), jnp.concatenate(cols, axis=1))
    for j in range(nb):
        sa_ref[pl.ds(j, 2 * r, stride=nb), :] = a[:, j * w:(j + 1) * w]
    a_ref[...] = sa_ref[...].reshape(2, r, nb, w).astype(BF16)


def _dft1(mat, xc, xs):
    ng, t, w = xc.shape
    r = RADIX
    nb = DFT1_BATCH
    blk = pl.BlockSpec((None, r, nb, w), lambda g, o: (g, 0, o, 0))
    return pl.pallas_call(
        _dft1_kernel,
        grid=(ng, r // nb),
        in_specs=[_const_spec(mat.shape), blk, blk],
        out_specs=pl.BlockSpec((2, r, nb, w), lambda g, o: (0, 0, o, g)),
        out_shape=jax.ShapeDtypeStruct((2, r, r, ng * w), BF16),
        scratch_shapes=[pltpu.VMEM((2, r * nb, w), F32), pltpu.VMEM((2 * r * nb, w), F32)],
        compiler_params=_cparams(2),
        name="dft1",
    )(mat, xc.reshape(ng, r, r, w), xs.reshape(ng, r, r, w))


DFT2_BATCH = 8


def _dft2_kernel(g_ref, a_ref, o_ref, s_ref):
    nb, r, w = a_ref.shape[1], a_ref.shape[2], a_ref.shape[3]
    for j in range(nb):
        a = jnp.concatenate([a_ref[0, j], a_ref[1, j]], axis=0)
        y = _dot(g_ref[j].astype(BF16), a)
        for gi in range(N_GROUPS):
            s_ref[gi, pl.ds(j, r, stride=nb), :] = y[:, gi * GROUP_DIM:(gi + 1) * GROUP_DIM]
    for gi in range(N_GROUPS):
        o_ref[gi] = s_ref[gi].reshape(r, nb, GROUP_DIM)


def _dft2(gmat, a4):
    _, r, _, w = a4.shape
    nb = DFT2_BATCH
    out = pl.pallas_call(
        _dft2_kernel,
        grid=(r // nb,),
        in_specs=[pl.BlockSpec((nb, r, 2 * r), lambda k: (k, 0, 0)),
                  pl.BlockSpec((2, nb, r, w), lambda k: (0, k, 0, 0))],
        out_specs=pl.BlockSpec((N_GROUPS, r, None, nb, GROUP_DIM), lambda k: (0, 0, k, 0, 0)),
        out_shape=jax.ShapeDtypeStruct((N_GROUPS, r, r // nb, nb, GROUP_DIM), F32),
        scratch_shapes=[pltpu.VMEM((N_GROUPS, r * nb, GROUP_DIM), F32)],
        compiler_params=_cparams(),
        name="dft2",
    )(gmat, a4)
    return out.reshape(N_GROUPS, r * r, GROUP_DIM)


def _hgrn_tables(L, reverse):
    nlev = int(np.log2(L))
    idx = np.arange(L)
    t = idx[:, None]
    i = idx[None, :]
    blocks = [(i >= t) if reverse else (i <= t)]
    for j in range(nlev):
        h = L >> (j + 1)
        mid = (t // (2 * h)) * (2 * h) + h
        upper = t >= mid
        if reverse:
            blk = np.where(upper, (i >= mid) & (i < t), (i >= t) & (i < mid))
        else:
            blk = np.where(upper, (i >= mid) & (i <= t), (i > t) & (i < mid))
        blocks.append(blk)
    blocks.append((i < t) if reverse else (i > t))
    return np.concatenate(blocks, axis=0).astype(np.float32), nlev


def _split_hi_lo(g):
    hi = g.astype(BF16)
    return hi, (g - hi.astype(F32)).astype(BF16)


def _hgrn_chunk_exact(q_ref, v_ref, g_ref, o_ref, st_ref, rows, mall, pair_masks, query_rows,
                      L, nlev, reverse):
    last = 0 if reverse else L - 1
    g = g_ref[rows, :]
    g_hi, g_lo = _split_hi_lo(g)
    ex = jnp.exp(_dot(mall, g_hi) + _dot(mall, g_lo))
    q = q_ref[rows, :].astype(F32)
    v = v_ref[rows, :]
    k = 1.0 - jnp.exp(g)
    e_cum = ex[0:L]
    qe = (q * e_cum).astype(BF16)
    ke = (k * ex[(nlev + 1) * L:(nlev + 2) * L]).astype(BF16)
    zs = [(jnp.where(query_rows[j], q, k) * ex[(j + 1) * L:(j + 2) * L]).astype(BF16)
          for j in range(nlev)]
    qk = q * k
    e_last = e_cum[last:last + 1, :]
    for hd in range(N_HEADS):
        sl = slice(hd * HEAD_DIM, (hd + 1) * HEAD_DIM)
        sc = jnp.zeros((L, L), F32)
        for j in range(nlev):
            zj = zs[j][:, sl]
            sc = jnp.where(pair_masks[j], _dot_nt(zj, zj), sc)
        st = st_ref[hd]
        vh = v[:, sl]
        o = _dot(sc.astype(BF16), vh) + _dot_nt(qe[:, sl], st.astype(BF16))
        o = o + jnp.sum(qk[:, sl], axis=-1, keepdims=True) * vh.astype(F32)
        o_ref[rows, sl] = o
        st_ref[hd] = st * e_last[:, sl] + _dot_tn(vh, ke[:, sl])


def _hgrn_block_fast(q_ref, v_ref, g_ref, trib_ref, o_ref, st_ref, nchunk, L, reverse):
    half = L // 2
    tb = nchunk * L
    chunks = []
    for c in range(nchunk):
        base = c * L
        if reverse:
            chunks.append((slice(base, base + L), slice(base + half, base + L),
                           slice(base, base + half), base + half, base))
        else:
            chunks.append((slice(base, base + L), slice(base, base + half),
                           slice(base + half, base + L), base + half - 1, base + L - 1))
    g = g_ref[...]
    g_hi, g_lo = _split_hi_lo(g)
    b = _dot(trib_ref[...], g_hi) + _dot(trib_ref[...], g_lo)
    q = q_ref[...].astype(F32)
    v = v_ref[...]
    k = 1.0 - jnp.exp(g)
    e_b = jnp.exp(b)
    qe = (q * e_b).astype(BF16)
    c2 = [b[sec] - b[edge:edge + 1, :] for (_, _, sec, edge, _) in chunks]
    q2 = jnp.concatenate([q[ch[2]] * jnp.exp(c2[c]) for c, ch in enumerate(chunks)], axis=0)
    own = []
    for c, (_, fst, _, _, _) in enumerate(chunks):
        own += [c2[c], b[fst]] if reverse else [b[fst], c2[c]]
    kh = (k * jnp.exp(-jnp.concatenate(own, axis=0))).astype(BF16)
    tail = jnp.concatenate([b[last:last + 1, :] - b[rows] for (rows, _, _, _, last) in chunks], axis=0)
    ke = (k * jnp.exp(tail)).astype(BF16)
    lhs = jnp.concatenate([qe, q2.astype(BF16)], axis=0)

    row = lax.broadcasted_iota(jnp.int32, (tb, tb), 0)
    col = lax.broadcasted_iota(jnp.int32, (tb, tb), 1)
    valid = (row // L == col // L) & ((col >= row) if reverse else (col <= row))
    if reverse:
        use_near = (row % L < half) & (col % L < half)
    else:
        use_near = (row % L >= half) & (col % L >= half)

    for hd in range(N_HEADS):
        sl = slice(hd * HEAD_DIM, (hd + 1) * HEAD_DIM)
        s_all = _dot_nt(lhs[:, sl], kh[:, sl])
        far = s_all[0:tb]
        pieces = []
        for c, (_, fst, _, _, _) in enumerate(chunks):
            near_c = s_all[tb + c * half:tb + (c + 1) * half]
            pieces += [near_c, far[fst]] if reverse else [far[fst], near_c]
        near = jnp.concatenate(pieces, axis=0)
        sc = jnp.where(valid, jnp.where(use_near, near, far), 0.0).astype(BF16)
        vh = v[:, sl]
        o_intra = _dot(sc, vh)
        upd = [_dot_tn(vh[rows], ke[rows, sl]) for (rows, _, _, _, _) in chunks]
        st = st_ref[hd]
        o_inter = [None] * nchunk
        for c in (range(nchunk - 1, -1, -1) if reverse else range(nchunk)):
            rows, _, _, _, last = chunks[c]
            o_inter[c] = _dot_nt(qe[rows, sl], st.astype(BF16))
            st = st * e_b[last:last + 1, sl] + upd[c]
        st_ref[hd] = st
        o_ref[:, sl] = o_intra + jnp.concatenate(o_inter, axis=0)


def _hgrn_block_exact(q_ref, v_ref, g_ref, mall_ref, o_ref, st_ref, nchunk, L, nlev, reverse):
    row = lax.broadcasted_iota(jnp.int32, (L, L), 0)
    col = lax.broadcasted_iota(jnp.int32, (L, L), 1)
    rowc = lax.broadcasted_iota(jnp.int32, (L, N_HEADS * HEAD_DIM), 0)
    pair_masks, query_rows = [], []
    for j in range(nlev):
        h = L >> (j + 1)
        same = (row // (2 * h)) == (col // (2 * h))
        row_up = (row // h) % 2 == 1
        col_up = (col // h) % 2 == 1
        if reverse:
            pair_masks.append(same & jnp.logical_not(row_up) & col_up)
            query_rows.append((rowc // h) % 2 == 0)
        else:
            pair_masks.append(same & row_up & jnp.logical_not(col_up))
            query_rows.append((rowc // h) % 2 == 1)
    mall = mall_ref[...]
    for c in (range(nchunk - 1, -1, -1) if reverse else range(nchunk)):
        _hgrn_chunk_exact(q_ref, v_ref, g_ref, o_ref, st_ref, slice(c * L, (c + 1) * L),
                          mall, pair_masks, query_rows, L, nlev, reverse)


def _min_leaf_log_decay(g_ref, leaf):
    g = g_ref[...]
    return jnp.min(jnp.sum(g.reshape(g.shape[0] // leaf, leaf, g.shape[1]), axis=1))


def _hgrn_kernel(qf_ref, vf_ref, gf_ref, qb_ref, vb_ref, gb_ref, mallf_ref, mallb_ref,
                 tribf_ref, tribb_ref, s0f_ref, s0b_ref, of_ref, ob_ref, sff_ref, sfb_ref,
                 stf_ref, stb_ref, *, nchunk, L, nlev):
    i = pl.program_id(0)

    @pl.when(i == 0)
    def _():
        stf_ref[...] = s0f_ref[...]
        stb_ref[...] = s0b_ref[...]

    safe = jnp.minimum(_min_leaf_log_decay(gf_ref, L // 2),
                       _min_leaf_log_decay(gb_ref, L // 2)) >= -SAFE_LOG_DECAY

    @pl.when(safe)
    def _():
        _hgrn_block_fast(qf_ref, vf_ref, gf_ref, tribf_ref, of_ref, stf_ref, nchunk, L, False)
        _hgrn_block_fast(qb_ref, vb_ref, gb_ref, tribb_ref, ob_ref, stb_ref, nchunk, L, True)

    @pl.when(jnp.logical_not(safe))
    def _():
        _hgrn_block_exact(qf_ref, vf_ref, gf_ref, mallf_ref, of_ref, stf_ref, nchunk, L, nlev, False)
        _hgrn_block_exact(qb_ref, vb_ref, gb_ref, mallb_ref, ob_ref, stb_ref, nchunk, L, nlev, True)

    @pl.when(i == pl.num_programs(0) - 1)
    def _():
        sff_ref[...] = stf_ref[...]
        sfb_ref[...] = stb_ref[...]


def _hgrn_scan(q, v, g, s0_f, s0_b, tb):
    t, w = q.shape
    L = HGRN_CHUNK
    nblk = t // tb
    consts = []
    for reverse in (False, True):
        mall_np, nlev = _hgrn_tables(L, reverse)
        consts.append((jnp.asarray(mall_np, dtype=BF16),
                       jnp.asarray(np.kron(np.eye(tb // L, dtype=np.float32), mall_np[0:L]), dtype=BF16)))
    (mall_f, trib_f), (mall_b, trib_b) = consts
    fwd = lambda i: (i, 0)
    bwd = lambda i: (nblk - 1 - i, 0)
    bwd_g = lambda i: (nblk - 1 - i, 1)
    blk = lambda m: pl.BlockSpec((tb, w), m)
    return pl.pallas_call(
        functools.partial(_hgrn_kernel, nchunk=tb // L, L=L, nlev=nlev),
        grid=(nblk,),
        in_specs=[blk(fwd), blk(fwd), blk(fwd), blk(bwd), blk(bwd), blk(bwd_g),
                  _const_spec(mall_f.shape), _const_spec(mall_b.shape),
                  _const_spec(trib_f.shape), _const_spec(trib_b.shape),
                  _const_spec(s0_f.shape), _const_spec(s0_b.shape)],
        out_specs=[blk(fwd), blk(bwd), _const_spec(s0_f.shape), _const_spec(s0_b.shape)],
        out_shape=[jax.ShapeDtypeStruct((t, w), F32)] * 2 + [jax.ShapeDtypeStruct(s0_f.shape, F32)] * 2,
        scratch_shapes=[pltpu.VMEM(s0_f.shape, F32)] * 2,
        compiler_params=_cparams(),
        name="hgrn",
    )(q, v, g, q, v, g, mall_f, mall_b, trib_f, trib_b, s0_f, s0_b)


def _merge_kernel(x_ref, er_ref, ec_ref, lng_ref, lnb_ref, mod_ref, wg_ref, bg_ref, of_ref, ob_ref,
                  ng_ref, four_ref, wfp_ref, whp_ref, wo_ref, bo_ref, pg_ref, pb_ref, o_ref, *, tm):
    xl, hl = _ln_in_modulated(x_ref, er_ref, ec_ref, lng_ref, lnb_ref, mod_ref, tm)
    hb = hl.astype(BF16)
    w4 = N_HEADS * HEAD_DIM
    d = x_ref.shape[1]
    og = _dot(hb, wg_ref[:, 0:w4]) + bg_ref[:, 0:w4]
    o = of_ref[...] + ob_ref[...]
    parts = []
    for hd in range(N_HEADS):
        oh = o[:, hd * HEAD_DIM:(hd + 1) * HEAD_DIM]
        ms = jnp.mean(oh * oh, axis=-1, keepdims=True)
        parts.append(oh * lax.rsqrt(ms + RMS_EPS))
    on = jnp.concatenate(parts, axis=1) * ng_ref[...]
    oh = (on * (og * _sigmoid(og))).astype(BF16)
    g_four = _sigmoid(_dot(hb, wg_ref[:, w4:w4 + d]) + bg_ref[:, w4:w4 + d])
    four = jnp.concatenate([four_ref[gi] for gi in range(N_GROUPS)], axis=1)
    y = g_four * _dot(four.astype(BF16), wfp_ref[...])
    g_hgrn = _sigmoid(_dot(hb, wg_ref[:, w4 + d:w4 + 2 * d]) + bg_ref[:, w4 + d:w4 + 2 * d])
    y = y + g_hgrn * _dot(oh, whp_ref[...])
    mix = _dot(y.astype(BF16), wo_ref[...]) + bo_ref[...]
    o_ref[...] = _layernorm(ALPHA * xl + mod_ref[2:3, :] * mix, pg_ref[...], pb_ref[...])


def _merge(x, er, ec, lng, lnb, mod, wg, bg, o_f, o_b, ng, four, wfp, whp, wo, bo, pg, pb, tm):
    t, d = x.shape
    w4 = N_HEADS * HEAD_DIM
    nrow = tm // GRID_W
    row = lambda i: (i, 0)
    consts = [ec, lng, lnb, mod, wg, bg]
    consts2 = [wfp, whp, wo, bo, pg, pb]
    return pl.pallas_call(
        functools.partial(_merge_kernel, tm=tm),
        grid=(t // tm,),
        in_specs=[pl.BlockSpec((tm, d), row), pl.BlockSpec((nrow, d // 2), row)]
        + [_const_spec(a.shape) for a in consts]
        + [pl.BlockSpec((tm, w4), row), pl.BlockSpec((tm, w4), row), _const_spec(ng.shape),
           pl.BlockSpec((N_GROUPS, tm, GROUP_DIM), lambda i: (0, i, 0))]
        + [_const_spec(a.shape) for a in consts2],
        out_specs=pl.BlockSpec((tm, d), row),
        out_shape=jax.ShapeDtypeStruct((t, d), F32),
        compiler_params=_cparams(),
        name="merge",
    )(x, er, *consts, o_f, o_b, ng, four, *consts2)


def _mlp_kernel(x_ref, mod_ref, w1_ref, b1_ref, w2_ref, b2_ref, pg_ref, pb_ref, o_ref, *, nsplit):
    x1 = x_ref[...]
    hb = (x1 * (1.0 + mod_ref[4:5, :]) + mod_ref[3:4, :]).astype(BF16)
    dff = w1_ref.shape[1]
    cw = dff // nsplit
    acc = jnp.zeros(x1.shape, F32)
    for c in range(nsplit):
        a = jnp.maximum(_dot(hb, w1_ref[:, c * cw:(c + 1) * cw]) + b1_ref[:, c * cw:(c + 1) * cw], 0.0)
        acc = acc + _dot((a * a).astype(BF16), w2_ref[c * cw:(c + 1) * cw, :])
    m = acc + b2_ref[...]
    o_ref[...] = _layernorm(ALPHA * x1 + mod_ref[5:6, :] * m, pg_ref[...], pb_ref[...])


def _mlp(x1, mod, w1, b1, w2, b2, pg, pb, tm):
    t, d = x1.shape
    row = lambda i: (i, 0)
    consts = [mod, w1, b1, w2, b2, pg, pb]
    return pl.pallas_call(
        functools.partial(_mlp_kernel, nsplit=4),
        grid=(t // tm,),
        in_specs=[pl.BlockSpec((tm, d), row)] + [_const_spec(a.shape) for a in consts],
        out_specs=pl.BlockSpec((tm, d), row),
        out_shape=jax.ShapeDtypeStruct((t, d), F32),
        compiler_params=_cparams(),
        name="mlp",
    )(x1, *consts)


def _pos_tables(rows, cols, dim):
    quarter = dim // 4
    omega = 1.0 / (POS_BASE ** (jnp.arange(quarter, dtype=F32) / quarter))
    r = jnp.arange(rows, dtype=F32)[:, None] * omega
    cc = jnp.arange(cols, dtype=F32)[:, None] * omega
    er = jnp.concatenate([jnp.sin(r), jnp.cos(r)], axis=-1)
    ec = jnp.concatenate([jnp.sin(cc), jnp.cos(cc)], axis=-1)
    return er, ec


def _dft_constants(t):
    n = RADIX
    kn = np.outer(np.arange(n), np.arange(n)).astype(np.float64)
    c = np.cos(2.0 * np.pi * kn / n)
    s = np.sin(2.0 * np.pi * kn / n)
    chan = np.concatenate([c, s], axis=1)
    stage1 = np.block([[c, -s], [-s, -c]])
    k1 = np.arange(n)[:, None, None]
    k2 = np.arange(n)[None, :, None]
    n2 = np.arange(n)[None, None, :]
    theta = 2.0 * np.pi * ((n2 * (k1 + n * k2)) % t) / t
    scale = 1.0 / np.sqrt(float(t) * GROUP_DIM)
    stage2 = np.concatenate([np.cos(theta), np.sin(theta)], axis=2) * scale
    as_f32 = lambda a: jnp.asarray(a.astype(np.float32))
    return as_f32(chan), as_f32(stage1), as_f32(stage2)


def kernel(x, c, ctx, c_ctx, ln_in_g, ln_in_b, w_ada, b_ada, w_in, b_in, hgrn_lb_logits, hgrn_norm_g,
           w_four_proj, w_hgrn_proj, w_out, b_out, w_mlp1, b_mlp1, w_mlp2, b_mlp2, ln_post_g, ln_post_b):
    B, T, D = x.shape
    assert B == 1 and T == RADIX * RADIX and T % GRID_W == 0
    TC = ctx.shape[1]
    w4 = N_HEADS * HEAD_DIM
    row2 = lambda a: a.reshape(1, -1)

    cc = jnp.zeros((8, D), F32).at[0].set(c[0]).at[1].set(c_ctx)
    mod = _mod_vectors(cc, w_ada[0], row2(b_ada[0]))
    mod_l = mod[0].reshape(6, D)
    mod_c = mod[1].reshape(6, D)

    er, ec = _pos_tables(T // GRID_W, GRID_W, D)
    lng, lnb = row2(ln_in_g), row2(ln_in_b)
    w_in_b = w_in[0].astype(BF16)
    b_in2 = row2(b_in[0])
    n_a = 5 * w4
    w_a, b_a = w_in_b[:, :n_a], b_in2[:, :n_a]
    w_g, b_g = w_in_b[:, n_a:], b_in2[:, n_a:]
    l0 = hgrn_lb_logits[:, 0, :].reshape(1, 2 * w4)
    l1 = hgrn_lb_logits[:, 1, :].reshape(1, 2 * w4)
    dft_chan, dft_s1, dft_s2 = _dft_constants(T)

    zc = jnp.zeros((TC // GRID_W, D // 2), F32)
    _, _, qc, vc, gc = _inproj(ctx[0], zc, jnp.zeros_like(ec), lng, lnb, mod_c, w_a, b_a, dft_chan,
                               l0, l1, tm=TC)
    s_zero = jnp.zeros((N_HEADS, HEAD_DIM, HEAD_DIM), F32)
    _, _, s_f, s_b = _hgrn_scan(qc, vc, gc, s_zero, s_zero, tb=TC)

    xc, xs, q, v, g = _inproj(x[0], er, ec, lng, lnb, mod_l, w_a, b_a, dft_chan, l0, l1, tm=512)
    four = _dft2(dft_s2, _dft1(dft_s1, xc, xs))
    o_f, o_b, _, _ = _hgrn_scan(q, v, g, s_f, s_b, tb=256)

    x1 = _merge(x[0], er, ec, lng, lnb, mod_l, w_g, b_g, o_f, o_b, row2(hgrn_norm_g[0]), four,
                w_four_proj[0].astype(BF16), w_hgrn_proj[0].astype(BF16), w_out[0].astype(BF16),
                row2(b_out[0]), row2(ln_post_g[0, 0]), row2(ln_post_b[0, 0]), tm=512)
    out = _mlp(x1, mod_l, w_mlp1[0].astype(BF16), row2(b_mlp1[0]), w_mlp2[0].astype(BF16),
               row2(b_mlp2[0]), row2(ln_post_g[0, 1]), row2(ln_post_b[0, 1]), tm=512)
    return out[None]
```

```python
import functools

import numpy as np
import jax
import jax.numpy as jnp
from jax import lax
from jax.experimental import pallas as pl
from jax.experimental.pallas import tpu as pltpu

F32 = jnp.float32
BF16 = jnp.bfloat16

GRID_W = 64
N_GROUPS = 4
GROUP_DIM = 128
N_HEADS = 4
HEAD_DIM = 128
POS_BASE = 10000.0
LN_EPS = 1e-5
RMS_EPS = 1e-6
DEPTH = 1
ALPHA = (2.0 * DEPTH) ** 0.25

RADIX = 128
HGRN_CHUNK = 64
SUB_ROWS = 256
SAFE_LOG_DECAY = 80.0
VMEM_LIMIT_BYTES = 56 * 1024 * 1024


def _cparams(n_axes=1):
    return pltpu.CompilerParams(dimension_semantics=("arbitrary",) * n_axes,
                                vmem_limit_bytes=VMEM_LIMIT_BYTES)


def _const_spec(shape):
    nd = len(shape)
    return pl.BlockSpec(shape, lambda *_: (0,) * nd, pipeline_mode=pl.Buffered(1))


def _sigmoid(x):
    return 1.0 / (1.0 + jnp.exp(-x))


def _layernorm(x, g, b):
    mu = jnp.mean(x, axis=-1, keepdims=True)
    xc = x - mu
    var = jnp.mean(xc * xc, axis=-1, keepdims=True)
    return xc * lax.rsqrt(var + LN_EPS) * g + b


def _dot(a, b):
    return jnp.dot(a, b, preferred_element_type=F32)


def _dot_nt(a, b):
    return lax.dot_general(a, b, (((1,), (1,)), ((), ())), preferred_element_type=F32)


def _dot_tn(a, b):
    return lax.dot_general(a, b, (((0,), (0,)), ((), ())), preferred_element_type=F32)


def _mod_kernel(c_ref, w_ref, b_ref, o_ref):
    cs = c_ref[...]
    s = cs * _sigmoid(cs)
    o_ref[...] = jnp.dot(s, w_ref[...], precision=lax.Precision.HIGHEST,
                         preferred_element_type=F32) + b_ref[...]


def _mod_vectors(cc, w_ada, b_ada):
    rows, d = cc.shape
    n = w_ada.shape[1]
    tn = 1536
    return pl.pallas_call(
        _mod_kernel,
        grid=(n // tn,),
        in_specs=[_const_spec((rows, d)),
                  pl.BlockSpec((d, tn), lambda j: (0, j)),
                  pl.BlockSpec((1, tn), lambda j: (0, j))],
        out_specs=pl.BlockSpec((rows, tn), lambda j: (0, j)),
        out_shape=jax.ShapeDtypeStruct((rows, n), F32),
        compiler_params=_cparams(),
        name="mod",
    )(cc, w_ada, b_ada)


def _ln_in_modulated(x_ref, er_ref, ec_ref, lng_ref, lnb_ref, mod_ref, r0, nr):
    x = x_ref[r0:r0 + nr, :]
    half = x.shape[1] // 2
    nrow = nr // GRID_W
    e0 = r0 // GRID_W
    left = jnp.concatenate(
        [jnp.broadcast_to(er_ref[e0 + r:e0 + r + 1, :], (GRID_W, half)) for r in range(nrow)], axis=0)
    right = jnp.concatenate([ec_ref[...]] * nrow, axis=0)
    xp = jnp.concatenate([x[:, :half] + left, x[:, half:] + right], axis=1)
    xl = _layernorm(xp, lng_ref[...], lnb_ref[...])
    hl = xl * (1.0 + mod_ref[1:2, :]) + mod_ref[0:1, :]
    return xl, hl


def _inproj_kernel(x_ref, er_ref, ec_ref, lng_ref, lnb_ref, mod_ref, w_ref, b_ref, dft_ref,
                   l0_ref, l1_ref, xc_ref, xs_ref, q_ref, v_ref, g_ref, *, tm):
    w4 = N_GROUPS * GROUP_DIM
    dft = dft_ref[...].astype(BF16)
    l0 = l0_ref[...]
    l1 = l1_ref[...]
    m = jnp.maximum(l0, l1)
    e0 = jnp.exp(l0 - m)
    lb = e0 / (e0 + jnp.exp(l1 - m))
    sub = min(tm, SUB_ROWS)
    for r0 in range(0, tm, sub):
        rows = slice(r0, r0 + sub)
        _, hl = _ln_in_modulated(x_ref, er_ref, ec_ref, lng_ref, lnb_ref, mod_ref, r0, sub)
        hb = hl.astype(BF16)
        fp = _dot(hb, w_ref[:, 3 * w4:5 * w4]) + b_ref[:, 3 * w4:5 * w4]
        g_ref[rows, :] = jnp.log(lb + (1.0 - lb) * _sigmoid(fp))
        qp = _dot(hb, w_ref[:, w4:2 * w4]) + b_ref[:, w4:2 * w4]
        q_ref[rows, :] = (qp * _sigmoid(qp)).astype(BF16)
        u = (_dot(hb, w_ref[:, 0:w4]) + b_ref[:, 0:w4]).astype(BF16)
        for gi in range(N_GROUPS):
            z = _dot(u[:, gi * GROUP_DIM:(gi + 1) * GROUP_DIM], dft)
            xc_ref[gi, rows, :] = z[:, :GROUP_DIM].astype(BF16)
            xs_ref[gi, rows, :] = z[:, GROUP_DIM:].astype(BF16)
        v_ref[rows, :] = (_dot(hb, w_ref[:, 2 * w4:3 * w4]) + b_ref[:, 2 * w4:3 * w4]).astype(BF16)


def _inproj(x, er, ec, lng, lnb, mod, w, b, dft, l0, l1, tm):
    t, d = x.shape
    w4 = N_GROUPS * GROUP_DIM
    nrow = tm // GRID_W
    row = lambda i: (i, 0)
    return pl.pallas_call(
        functools.partial(_inproj_kernel, tm=tm),
        grid=(t // tm,),
        in_specs=[pl.BlockSpec((tm, d), row),
                  pl.BlockSpec((nrow, d // 2), row),
                  _const_spec(ec.shape), _const_spec(lng.shape), _const_spec(lnb.shape),
                  _const_spec(mod.shape), _const_spec(w.shape), _const_spec(b.shape),
                  _const_spec(dft.shape), _const_spec(l0.shape), _const_spec(l1.shape)],
        out_specs=[pl.BlockSpec((N_GROUPS, tm, GROUP_DIM), lambda i: (0, i, 0))] * 2
        + [pl.BlockSpec((tm, w4), row)] * 2 + [pl.BlockSpec((tm, 2 * w4), row)],
        out_shape=[jax.ShapeDtypeStruct((N_GROUPS, t, GROUP_DIM), BF16)] * 2
        + [jax.ShapeDtypeStruct((t, w4), BF16)] * 2 + [jax.ShapeDtypeStruct((t, 2 * w4), F32)],
        compiler_params=_cparams(),
        name="inproj",
    )(x, er, ec, lng, lnb, mod, w, b, dft, l0, l1)


DFT1_BATCH = 16


def _dft1_kernel(m_ref, xc_ref, xs_ref, a_ref, sx_ref, sa_ref):
    r, nb, w = xc_ref.shape
    sx_ref[0] = xc_ref[...].astype(F32).reshape(r * nb, w)
    sx_ref[1] = xs_ref[...].astype(F32).reshape(r * nb, w)
    cols = []
    for j in range(nb):
        cols.append(jnp.concatenate([sx_ref[0, pl.ds(j, r, stride=nb), :],
                                     sx_ref[1, pl.ds(j, r, stride=nb), :]], axis=0).astype(BF16))
    a = _dot(m_ref[...].astype(BF16), jnp.concatenate(cols, axis=1))
    for j in range(nb):
        sa_ref[pl.ds(j, 2 * r, stride=nb), :] = a[:, j * w:(j + 1) * w]
    a_ref[...] = sa_ref[...].reshape(2, r, nb, w).astype(BF16)


def _dft1(mat, xc, xs):
    ng, t, w = xc.shape
    r = RADIX
    nb = DFT1_BATCH
    blk = pl.BlockSpec((None, r, nb, w), lambda g, o: (g, 0, o, 0))
    return pl.pallas_call(
        _dft1_kernel,
        grid=(ng, r // nb),
        in_specs=[_const_spec(mat.shape), blk, blk],
        out_specs=pl.BlockSpec((2, r, nb, w), lambda g, o: (0, 0, o, g)),
        out_shape=jax.ShapeDtypeStruct((2, r, r, ng * w), BF16),
        scratch_shapes=[pltpu.VMEM((2, r * nb, w), F32), pltpu.VMEM((2 * r * nb, w), F32)],
        compiler_params=_cparams(2),
        name="dft1",
    )(mat, xc.reshape(ng, r, r, w), xs.reshape(ng, r, r, w))


DFT2_BATCH = 8


def _dft2_kernel(g_ref, a_ref, o_ref, s_ref):
    nb, r, w = a_ref.shape[1], a_ref.shape[2], a_ref.shape[3]
    for j in range(nb):
        a = jnp.concatenate([a_ref[0, j], a_ref[1, j]], axis=0)
        y = _dot(g_ref[j].astype(BF16), a)
        for gi in range(N_GROUPS):
            s_ref[gi, pl.ds(j, r, stride=nb), :] = y[:, gi * GROUP_DIM:(gi + 1) * GROUP_DIM]
    for gi in range(N_GROUPS):
        o_ref[gi] = s_ref[gi].reshape(r, nb, GROUP_DIM)


def _dft2(gmat, a4):
    _, r, _, w = a4.shape
    nb = DFT2_BATCH
    out = pl.pallas_call(
        _dft2_kernel,
        grid=(r // nb,),
        in_specs=[pl.BlockSpec((nb, r, 2 * r), lambda k: (k, 0, 0)),
                  pl.BlockSpec((2, nb, r, w), lambda k: (0, k, 0, 0))],
        out_specs=pl.BlockSpec((N_GROUPS, r, None, nb, GROUP_DIM), lambda k: (0, 0, k, 0, 0)),
        out_shape=jax.ShapeDtypeStruct((N_GROUPS, r, r // nb, nb, GROUP_DIM), F32),
        scratch_shapes=[pltpu.VMEM((N_GROUPS, r * nb, GROUP_DIM), F32)],
        compiler_params=_cparams(),
        name="dft2",
    )(gmat, a4)
    return out.reshape(N_GROUPS, r * r, GROUP_DIM)


def _hgrn_tables(L, reverse):
    nlev = int(np.log2(L))
    idx = np.arange(L)
    t = idx[:, None]
    i = idx[None, :]
    blocks = [(i >= t) if reverse else (i <= t)]
    for j in range(nlev):
        h = L >> (j + 1)
        mid = (t // (2 * h)) * (2 * h) + h
        upper = t >= mid
        if reverse:
            blk = np.where(upper, (i >= mid) & (i < t), (i >= t) & (i < mid))
        else:
            blk = np.where(upper, (i >= mid) & (i <= t), (i > t) & (i < mid))
        blocks.append(blk)
    blocks.append((i < t) if reverse else (i > t))
    return np.concatenate(blocks, axis=0).astype(np.float32), nlev


def _split_hi_lo(g):
    hi = g.astype(BF16)
    return hi, (g - hi.astype(F32)).astype(BF16)


def _hgrn_chunk_exact(q_ref, v_ref, g_ref, o_ref, st_ref, rows, mall, pair_masks, query_rows,
                      L, nlev, reverse):
    last = 0 if reverse else L - 1
    g = g_ref[rows, :]
    g_hi, g_lo = _split_hi_lo(g)
    ex = jnp.exp(_dot(mall, g_hi) + _dot(mall, g_lo))
    q = q_ref[rows, :].astype(F32)
    v = v_ref[rows, :]
    k = 1.0 - jnp.exp(g)
    e_cum = ex[0:L]
    qe = (q * e_cum).astype(BF16)
    ke = (k * ex[(nlev + 1) * L:(nlev + 2) * L]).astype(BF16)
    zs = [(jnp.where(query_rows[j], q, k) * ex[(j + 1) * L:(j + 2) * L]).astype(BF16)
          for j in range(nlev)]
    qk = q * k
    e_last = e_cum[last:last + 1, :]
    for hd in range(N_HEADS):
        sl = slice(hd * HEAD_DIM, (hd + 1) * HEAD_DIM)
        sc = jnp.zeros((L, L), F32)
        for j in range(nlev):
            zj = zs[j][:, sl]
            sc = jnp.where(pair_masks[j], _dot_nt(zj, zj), sc)
        st = st_ref[hd]
        vh = v[:, sl]
        o = _dot(sc.astype(BF16), vh) + _dot_nt(qe[:, sl], st.astype(BF16))
        o = o + jnp.sum(qk[:, sl], axis=-1, keepdims=True) * vh.astype(F32)
        o_ref[rows, sl] = o
        st_ref[hd] = st * e_last[:, sl] + _dot_tn(vh, ke[:, sl])


def _hgrn_block_fast(q_ref, v_ref, g_ref, trib_ref, o_ref, st_ref, nchunk, L, reverse):
    half = L // 2
    tb = nchunk * L
    chunks = []
    for c in range(nchunk):
        base = c * L
        if reverse:
            chunks.append((slice(base, base + L), slice(base + half, base + L),
                           slice(base, base + half), base + half, base))
        else:
            chunks.append((slice(base, base + L), slice(base, base + half),
                           slice(base + half, base + L), base + half - 1, base + L - 1))
    g = g_ref[...]
    g_hi, g_lo = _split_hi_lo(g)
    b = _dot(trib_ref[...], g_hi) + _dot(trib_ref[...], g_lo)
    q = q_ref[...].astype(F32)
    v = v_ref[...]
    k = 1.0 - jnp.exp(g)
    e_b = jnp.exp(b)
    qe = (q * e_b).astype(BF16)
    c2 = [b[sec] - b[edge:edge + 1, :] for (_, _, sec, edge, _) in chunks]
    q2 = jnp.concatenate([q[ch[2]] * jnp.exp(c2[c]) for c, ch in enumerate(chunks)], axis=0)
    own = []
    for c, (_, fst, _, _, _) in enumerate(chunks):
        own += [c2[c], b[fst]] if reverse else [b[fst], c2[c]]
    kh = (k * jnp.exp(-jnp.concatenate(own, axis=0))).astype(BF16)
    tail = jnp.concatenate([b[last:last + 1, :] - b[rows] for (rows, _, _, _, last) in chunks], axis=0)
    ke = (k * jnp.exp(tail)).astype(BF16)
    lhs = jnp.concatenate([qe, q2.astype(BF16)], axis=0)

    row = lax.broadcasted_iota(jnp.int32, (tb, tb), 0)
    col = lax.broadcasted_iota(jnp.int32, (tb, tb), 1)
    valid = (row // L == col // L) & ((col >= row) if reverse else (col <= row))
    if reverse:
        use_near = (row % L < half) & (col % L < half)
    else:
        use_near = (row % L >= half) & (col % L >= half)

    for hd in range(N_HEADS):
        sl = slice(hd * HEAD_DIM, (hd + 1) * HEAD_DIM)
        s_all = _dot_nt(lhs[:, sl], kh[:, sl])
        far = s_all[0:tb]
        pieces = []
        for c, (_, fst, _, _, _) in enumerate(chunks):
            near_c = s_all[tb + c * half:tb + (c + 1) * half]
            pieces += [near_c, far[fst]] if reverse else [far[fst], near_c]
        near = jnp.concatenate(pieces, axis=0)
        sc = jnp.where(valid, jnp.where(use_near, near, far), 0.0).astype(BF16)
        vh = v[:, sl]
        o_intra = _dot(sc, vh)
        upd = [_dot_tn(vh[rows], ke[rows, sl]) for (rows, _, _, _, _) in chunks]
        st = st_ref[hd]
        o_inter = [None] * nchunk
        for c in (range(nchunk - 1, -1, -1) if reverse else range(nchunk)):
            rows, _, _, _, last = chunks[c]
            o_inter[c] = _dot_nt(qe[rows, sl], st.astype(BF16))
            st = st * e_b[last:last + 1, sl] + upd[c]
        st_ref[hd] = st
        o_ref[:, sl] = o_intra + jnp.concatenate(o_inter, axis=0)


def _hgrn_block_exact(q_ref, v_ref, g_ref, mall_ref, o_ref, st_ref, nchunk, L, nlev, reverse):
    row = lax.broadcasted_iota(jnp.int32, (L, L), 0)
    col = lax.broadcasted_iota(jnp.int32, (L, L), 1)
    rowc = lax.broadcasted_iota(jnp.int32, (L, N_HEADS * HEAD_DIM), 0)
    pair_masks, query_rows = [], []
    for j in range(nlev):
        h = L >> (j + 1)
        same = (row // (2 * h)) == (col // (2 * h))
        row_up = (row // h) % 2 == 1
        col_up = (col // h) % 2 == 1
        if reverse:
            pair_masks.append(same & jnp.logical_not(row_up) & col_up)
            query_rows.append((rowc // h) % 2 == 0)
        else:
            pair_masks.append(same & row_up & jnp.logical_not(col_up))
            query_rows.append((rowc // h) % 2 == 1)
    mall = mall_ref[...]
    for c in (range(nchunk - 1, -1, -1) if reverse else range(nchunk)):
        _hgrn_chunk_exact(q_ref, v_ref, g_ref, o_ref, st_ref, slice(c * L, (c + 1) * L),
                          mall, pair_masks, query_rows, L, nlev, reverse)


def _min_leaf_log_decay(g_ref, leaf):
    g = g_ref[...]
    return jnp.min(jnp.sum(g.reshape(g.shape[0] // leaf, leaf, g.shape[1]), axis=1))


def _hgrn_kernel(qf_ref, vf_ref, gf_ref, qb_ref, vb_ref, gb_ref, mallf_ref, mallb_ref,
                 tribf_ref, tribb_ref, s0f_ref, s0b_ref, of_ref, ob_ref, sff_ref, sfb_ref,
                 stf_ref, stb_ref, *, nchunk, L, nlev):
    i = pl.program_id(0)

    @pl.when(i == 0)
    def _():
        stf_ref[...] = s0f_ref[...]
        stb_ref[...] = s0b_ref[...]

    safe = jnp.minimum(_min_leaf_log_decay(gf_ref, L // 2),
                       _min_leaf_log_decay(gb_ref, L // 2)) >= -SAFE_LOG_DECAY

    @pl.when(safe)
    def _():
        _hgrn_block_fast(qf_ref, vf_ref, gf_ref, tribf_ref, of_ref, stf_ref, nchunk, L, False)
        _hgrn_block_fast(qb_ref, vb_ref, gb_ref, tribb_ref, ob_ref, stb_ref, nchunk, L, True)

    @pl.when(jnp.logical_not(safe))
    def _():
        _hgrn_block_exact(qf_ref, vf_ref, gf_ref, mallf_ref, of_ref, stf_ref, nchunk, L, nlev, False)
        _hgrn_block_exact(qb_ref, vb_ref, gb_ref, mallb_ref, ob_ref, stb_ref, nchunk, L, nlev, True)

    @pl.when(i == pl.num_programs(0) - 1)
    def _():
        sff_ref[...] = stf_ref[...]
        sfb_ref[...] = stb_ref[...]


def _hgrn_scan(q, v, g, s0_f, s0_b, tb):
    t, w = q.shape
    L = HGRN_CHUNK
    nblk = t // tb
    consts = []
    for reverse in (False, True):
        mall_np, nlev = _hgrn_tables(L, reverse)
        consts.append((jnp.asarray(mall_np, dtype=BF16),
                       jnp.asarray(np.kron(np.eye(tb // L, dtype=np.float32), mall_np[0:L]), dtype=BF16)))
    (mall_f, trib_f), (mall_b, trib_b) = consts
    fwd = lambda i: (i, 0)
    bwd = lambda i: (nblk - 1 - i, 0)
    bwd_g = lambda i: (nblk - 1 - i, 1)
    blk = lambda m: pl.BlockSpec((tb, w), m)
    return pl.pallas_call(
        functools.partial(_hgrn_kernel, nchunk=tb // L, L=L, nlev=nlev),
        grid=(nblk,),
        in_specs=[blk(fwd), blk(fwd), blk(fwd), blk(bwd), blk(bwd), blk(bwd_g),
                  _const_spec(mall_f.shape), _const_spec(mall_b.shape),
                  _const_spec(trib_f.shape), _const_spec(trib_b.shape),
                  _const_spec(s0_f.shape), _const_spec(s0_b.shape)],
        out_specs=[blk(fwd), blk(bwd)] + [pl.BlockSpec(s0_f.shape, lambda i: (0, 0, 0))] * 2,
        out_shape=[jax.ShapeDtypeStruct((t, w), F32)] * 2 + [jax.ShapeDtypeStruct(s0_f.shape, F32)] * 2,
        scratch_shapes=[pltpu.VMEM(s0_f.shape, F32)] * 2,
        compiler_params=_cparams(),
        name="hgrn",
    )(q, v, g, q, v, g, mall_f, mall_b, trib_f, trib_b, s0_f, s0_b)


def _merge_kernel(x_ref, er_ref, ec_ref, lng_ref, lnb_ref, mod_ref, wg_ref, bg_ref, of_ref, ob_ref,
                  ng_ref, four_ref, wfp_ref, whp_ref, wo_ref, bo_ref, pg_ref, pb_ref, o_ref, *, tm):
    w4 = N_HEADS * HEAD_DIM
    d = x_ref.shape[1]
    sub = min(tm, SUB_ROWS)
    for r0 in range(0, tm, sub):
        rows = slice(r0, r0 + sub)
        xl, hl = _ln_in_modulated(x_ref, er_ref, ec_ref, lng_ref, lnb_ref, mod_ref, r0, sub)
        hb = hl.astype(BF16)
        og = _dot(hb, wg_ref[:, 0:w4]) + bg_ref[:, 0:w4]
        o = of_ref[rows, :] + ob_ref[rows, :]
        parts = []
        for hd in range(N_HEADS):
            oh = o[:, hd * HEAD_DIM:(hd + 1) * HEAD_DIM]
            ms = jnp.mean(oh * oh, axis=-1, keepdims=True)
            parts.append(oh * lax.rsqrt(ms + RMS_EPS))
        on = jnp.concatenate(parts, axis=1) * ng_ref[...]
        oh = (on * (og * _sigmoid(og))).astype(BF16)
        g_four = _sigmoid(_dot(hb, wg_ref[:, w4:w4 + d]) + bg_ref[:, w4:w4 + d])
        four = jnp.concatenate([four_ref[gi, rows, :] for gi in range(N_GROUPS)], axis=1)
        y = g_four * _dot(four.astype(BF16), wfp_ref[...])
        g_hgrn = _sigmoid(_dot(hb, wg_ref[:, w4 + d:w4 + 2 * d]) + bg_ref[:, w4 + d:w4 + 2 * d])
        y = y + g_hgrn * _dot(oh, whp_ref[...])
        mix = _dot(y.astype(BF16), wo_ref[...]) + bo_ref[...]
        o_ref[rows, :] = _layernorm(ALPHA * xl + mod_ref[2:3, :] * mix, pg_ref[...], pb_ref[...])


def _merge(x, er, ec, lng, lnb, mod, wg, bg, o_f, o_b, ng, four, wfp, whp, wo, bo, pg, pb, tm):
    t, d = x.shape
    w4 = N_HEADS * HEAD_DIM
    nrow = tm // GRID_W
    row = lambda i: (i, 0)
    consts = [ec, lng, lnb, mod, wg, bg]
    consts2 = [wfp, whp, wo, bo, pg, pb]
    return pl.pallas_call(
        functools.partial(_merge_kernel, tm=tm),
        grid=(t // tm,),
        in_specs=[pl.BlockSpec((tm, d), row), pl.BlockSpec((nrow, d // 2), row)]
        + [_const_spec(a.shape) for a in consts]
        + [pl.BlockSpec((tm, w4), row), pl.BlockSpec((tm, w4), row), _const_spec(ng.shape),
           pl.BlockSpec((N_GROUPS, tm, GROUP_DIM), lambda i: (0, i, 0))]
        + [_const_spec(a.shape) for a in consts2],
        out_specs=pl.BlockSpec((tm, d), row),
        out_shape=jax.ShapeDtypeStruct((t, d), F32),
        compiler_params=_cparams(),
        name="merge",
    )(x, er, *consts, o_f, o_b, ng, four, *consts2)


def _mlp_kernel(x_ref, mod_ref, w1_ref, b1_ref, w2_ref, b2_ref, pg_ref, pb_ref, o_ref, *, nsplit):
    dff = w1_ref.shape[1]
    cw = dff // nsplit
    tm = x_ref.shape[0]
    sub = min(tm, SUB_ROWS)
    for r0 in range(0, tm, sub):
        rows = slice(r0, r0 + sub)
        x1 = x_ref[rows, :]
        hb = (x1 * (1.0 + mod_ref[4:5, :]) + mod_ref[3:4, :]).astype(BF16)
        acc = jnp.zeros(x1.shape, F32)
        for c in range(nsplit):
            cs = slice(c * cw, (c + 1) * cw)
            a = jnp.maximum(_dot(hb, w1_ref[:, cs]) + b1_ref[:, cs], 0.0)
            acc = acc + _dot((a * a).astype(BF16), w2_ref[cs, :])
        m = acc + b2_ref[...]
        o_ref[rows, :] = _layernorm(ALPHA * x1 + mod_ref[5:6, :] * m, pg_ref[...], pb_ref[...])


def _mlp(x1, mod, w1, b1, w2, b2, pg, pb, tm):
    t, d = x1.shape
    row = lambda i: (i, 0)
    consts = [mod, w1, b1, w2, b2, pg, pb]
    return pl.pallas_call(
        functools.partial(_mlp_kernel, nsplit=4),
        grid=(t // tm,),
        in_specs=[pl.BlockSpec((tm, d), row)] + [_const_spec(a.shape) for a in consts],
        out_specs=pl.BlockSpec((tm, d), row),
        out_shape=jax.ShapeDtypeStruct((t, d), F32),
        compiler_params=_cparams(),
        name="mlp",
    )(x1, *consts)


def _pos_tables(rows, cols, dim):
    quarter = dim // 4
    omega = 1.0 / (POS_BASE ** (jnp.arange(quarter, dtype=F32) / quarter))
    r = jnp.arange(rows, dtype=F32)[:, None] * omega
    cc = jnp.arange(cols, dtype=F32)[:, None] * omega
    er = jnp.concatenate([jnp.sin(r), jnp.cos(r)], axis=-1)
    ec = jnp.concatenate([jnp.sin(cc), jnp.cos(cc)], axis=-1)
    return er, ec


def _dft_constants(t):
    n = RADIX
    kn = np.outer(np.arange(n), np.arange(n)).astype(np.float64)
    c = np.cos(2.0 * np.pi * kn / n)
    s = np.sin(2.0 * np.pi * kn / n)
    chan = np.concatenate([c, s], axis=1)
    stage1 = np.block([[c, -s], [-s, -c]])
    k1 = np.arange(n)[:, None, None]
    k2 = np.arange(n)[None, :, None]
    n2 = np.arange(n)[None, None, :]
    theta = 2.0 * np.pi * ((n2 * (k1 + n * k2)) % t) / t
    scale = 1.0 / np.sqrt(float(t) * GROUP_DIM)
    stage2 = np.concatenate([np.cos(theta), np.sin(theta)], axis=2) * scale
    as_f32 = lambda a: jnp.asarray(a.astype(np.float32))
    return as_f32(chan), as_f32(stage1), as_f32(stage2)


def kernel(x, c, ctx, c_ctx, ln_in_g, ln_in_b, w_ada, b_ada, w_in, b_in, hgrn_lb_logits, hgrn_norm_g,
           w_four_proj, w_hgrn_proj, w_out, b_out, w_mlp1, b_mlp1, w_mlp2, b_mlp2, ln_post_g, ln_post_b):
    B, T, D = x.shape
    assert B == 1 and T == RADIX * RADIX and T % GRID_W == 0
    TC = ctx.shape[1]
    w4 = N_HEADS * HEAD_DIM
    row2 = lambda a: a.reshape(1, -1)

    cc = jnp.zeros((8, D), F32).at[0].set(c[0]).at[1].set(c_ctx)
    mod = _mod_vectors(cc, w_ada[0], row2(b_ada[0]))
    mod_l = mod[0].reshape(6, D)
    mod_c = mod[1].reshape(6, D)

    er, ec = _pos_tables(T // GRID_W, GRID_W, D)
    lng, lnb = row2(ln_in_g), row2(ln_in_b)
    w_in_b = w_in[0].astype(BF16)
    b_in2 = row2(b_in[0])
    n_a = 5 * w4
    w_a, b_a = w_in_b[:, :n_a], b_in2[:, :n_a]
    w_g, b_g = w_in_b[:, n_a:], b_in2[:, n_a:]
    l0 = hgrn_lb_logits[:, 0, :].reshape(1, 2 * w4)
    l1 = hgrn_lb_logits[:, 1, :].reshape(1, 2 * w4)
    dft_chan, dft_s1, dft_s2 = _dft_constants(T)

    zc = jnp.zeros((TC // GRID_W, D // 2), F32)
    _, _, qc, vc, gc = _inproj(ctx[0], zc, jnp.zeros_like(ec), lng, lnb, mod_c, w_a, b_a, dft_chan,
                               l0, l1, tm=TC)
    s_zero = jnp.zeros((N_HEADS, HEAD_DIM, HEAD_DIM), F32)
    _, _, s_f, s_b = _hgrn_scan(qc, vc, gc, s_zero, s_zero, tb=TC)

    xc, xs, q, v, g = _inproj(x[0], er, ec, lng, lnb, mod_l, w_a, b_a, dft_chan, l0, l1, tm=1024)
    four = _dft2(dft_s2, _dft1(dft_s1, xc, xs))
    o_f, o_b, _, _ = _hgrn_scan(q, v, g, s_f, s_b, tb=256)

    x1 = _merge(x[0], er, ec, lng, lnb, mod_l, w_g, b_g, o_f, o_b, row2(hgrn_norm_g[0]), four,
                w_four_proj[0].astype(BF16), w_hgrn_proj[0].astype(BF16), w_out[0].astype(BF16),
                row2(b_out[0]), row2(ln_post_g[0, 0]), row2(ln_post_b[0, 0]), tm=1024)
    out = _mlp(x1, mod_l, w_mlp1[0].astype(BF16), row2(b_mlp1[0]), w_mlp2[0].astype(BF16),
               row2(b_mlp2[0]), row2(ln_post_g[0, 1]), row2(ln_post_b[0, 1]), tm=1024)
    return out[None]
```

```python
import functools

import numpy as np
import jax
import jax.numpy as jnp
from jax import lax
from jax.experimental import pallas as pl
from jax.experimental.pallas import tpu as pltpu

F32 = jnp.float32
BF16 = jnp.bfloat16

GRID_W = 64
N_GROUPS = 4
GROUP_DIM = 128
N_HEADS = 4
HEAD_DIM = 128
POS_BASE = 10000.0
LN_EPS = 1e-5
RMS_EPS = 1e-6
DEPTH = 1
ALPHA = (2.0 * DEPTH) ** 0.25

RADIX = 128
HGRN_CHUNK = 64
SUB_ROWS = 256
SAFE_LOG_DECAY = 80.0
VMEM_LIMIT_BYTES = 56 * 1024 * 1024


def _cparams(n_axes=1):
    return pltpu.CompilerParams(dimension_semantics=("arbitrary",) * n_axes,
                                vmem_limit_bytes=VMEM_LIMIT_BYTES)


def _const_spec(shape):
    nd = len(shape)
    return pl.BlockSpec(shape, lambda *_: (0,) * nd, pipeline_mode=pl.Buffered(1))


def _sigmoid(x):
    return 1.0 / (1.0 + jnp.exp(-x))


def _layernorm(x, g, b):
    mu = jnp.mean(x, axis=-1, keepdims=True)
    xc = x - mu
    var = jnp.mean(xc * xc, axis=-1, keepdims=True)
    return xc * lax.rsqrt(var + LN_EPS) * g + b


def _dot(a, b):
    return jnp.dot(a, b, preferred_element_type=F32)


def _dot_nt(a, b):
    return lax.dot_general(a, b, (((1,), (1,)), ((), ())), preferred_element_type=F32)


def _dot_tn(a, b):
    return lax.dot_general(a, b, (((0,), (0,)), ((), ())), preferred_element_type=F32)


def _mod_kernel(cl_ref, cx_ref, w_ref, b_ref, ol_ref, ox_ref):
    w = w_ref[...]
    for c_ref, o_ref in ((cl_ref, ol_ref), (cx_ref, ox_ref)):
        cs = c_ref[...]
        s = cs * _sigmoid(cs)
        o_ref[...] = jnp.sum(s * w, axis=0, keepdims=True) + b_ref[...]


def _mod_vectors(c_lat, c_ctx, w_ada, b_ada):
    d, n = w_ada.shape
    tn = 1536
    col = lambda j: (0, j)
    return pl.pallas_call(
        _mod_kernel,
        grid=(n // tn,),
        in_specs=[_const_spec((d, 1)), _const_spec((d, 1)),
                  pl.BlockSpec((d, tn), col), pl.BlockSpec((1, tn), col)],
        out_specs=[pl.BlockSpec((1, tn), col)] * 2,
        out_shape=[jax.ShapeDtypeStruct((1, n), F32)] * 2,
        compiler_params=_cparams(),
        name="mod",
    )(c_lat.reshape(d, 1), c_ctx.reshape(d, 1), w_ada, b_ada)


def _ln_in_modulated(x_ref, er_ref, ec_ref, lng_ref, lnb_ref, mod_ref, r0, nr):
    x = x_ref[r0:r0 + nr, :]
    half = x.shape[1] // 2
    nrow = nr // GRID_W
    e0 = r0 // GRID_W
    left = jnp.concatenate(
        [jnp.broadcast_to(er_ref[e0 + r:e0 + r + 1, :], (GRID_W, half)) for r in range(nrow)], axis=0)
    right = jnp.concatenate([ec_ref[...]] * nrow, axis=0)
    xp = jnp.concatenate([x[:, :half] + left, x[:, half:] + right], axis=1)
    xl = _layernorm(xp, lng_ref[...], lnb_ref[...])
    hl = xl * (1.0 + mod_ref[1:2, :]) + mod_ref[0:1, :]
    return xl, hl


def _inproj_kernel(x_ref, er_ref, ec_ref, lng_ref, lnb_ref, mod_ref, w_ref, b_ref, dft_ref,
                   l0_ref, l1_ref, xc_ref, xs_ref, q_ref, v_ref, g_ref, *, tm):
    w4 = N_GROUPS * GROUP_DIM
    dft = dft_ref[...].astype(BF16)
    l0 = l0_ref[...]
    l1 = l1_ref[...]
    m = jnp.maximum(l0, l1)
    e0 = jnp.exp(l0 - m)
    lb = e0 / (e0 + jnp.exp(l1 - m))
    sub = min(tm, SUB_ROWS)
    for r0 in range(0, tm, sub):
        rows = slice(r0, r0 + sub)
        _, hl = _ln_in_modulated(x_ref, er_ref, ec_ref, lng_ref, lnb_ref, mod_ref, r0, sub)
        hb = hl.astype(BF16)
        fp = _dot(hb, w_ref[:, 3 * w4:5 * w4]) + b_ref[:, 3 * w4:5 * w4]
        g_ref[rows, :] = jnp.log(lb + (1.0 - lb) * _sigmoid(fp))
        qp = _dot(hb, w_ref[:, w4:2 * w4]) + b_ref[:, w4:2 * w4]
        q_ref[rows, :] = (qp * _sigmoid(qp)).astype(BF16)
        u = (_dot(hb, w_ref[:, 0:w4]) + b_ref[:, 0:w4]).astype(BF16)
        for gi in range(N_GROUPS):
            z = _dot(u[:, gi * GROUP_DIM:(gi + 1) * GROUP_DIM], dft)
            xc_ref[gi, rows, :] = z[:, :GROUP_DIM].astype(BF16)
            xs_ref[gi, rows, :] = z[:, GROUP_DIM:].astype(BF16)
        v_ref[rows, :] = (_dot(hb, w_ref[:, 2 * w4:3 * w4]) + b_ref[:, 2 * w4:3 * w4]).astype(BF16)


def _inproj(x, er, ec, lng, lnb, mod, w, b, dft, l0, l1, tm):
    t, d = x.shape
    w4 = N_GROUPS * GROUP_DIM
    nrow = tm // GRID_W
    row = lambda i: (i, 0)
    return pl.pallas_call(
        functools.partial(_inproj_kernel, tm=tm),
        grid=(t // tm,),
        in_specs=[pl.BlockSpec((tm, d), row),
                  pl.BlockSpec((nrow, d // 2), row),
                  _const_spec(ec.shape), _const_spec(lng.shape), _const_spec(lnb.shape),
                  _const_spec(mod.shape), _const_spec(w.shape), _const_spec(b.shape),
                  _const_spec(dft.shape), _const_spec(l0.shape), _const_spec(l1.shape)],
        out_specs=[pl.BlockSpec((N_GROUPS, tm, GROUP_DIM), lambda i: (0, i, 0))] * 2
        + [pl.BlockSpec((tm, w4), row)] * 2 + [pl.BlockSpec((tm, 2 * w4), row)],
        out_shape=[jax.ShapeDtypeStruct((N_GROUPS, t, GROUP_DIM), BF16)] * 2
        + [jax.ShapeDtypeStruct((t, w4), BF16)] * 2 + [jax.ShapeDtypeStruct((t, 2 * w4), F32)],
        compiler_params=_cparams(),
        name="inproj",
    )(x, er, ec, lng, lnb, mod, w, b, dft, l0, l1)


DFT_BATCH = 16
F32_SUBLANES = 8


def _dft1_kernel(m_ref, xc_ref, xs_ref, a_ref, sx_ref, sa_ref):
    r, nb, w = xc_ref.shape
    hs = F32_SUBLANES
    for p, x_ref in enumerate((xc_ref, xs_ref)):
        x = x_ref[...].astype(F32)
        for h in range(nb // hs):
            sx_ref[p, h] = x[:, h * hs:(h + 1) * hs, :].reshape(r * hs, w)
    cols = []
    for j in range(nb):
        cols.append(jnp.concatenate(
            [sx_ref[p, j // hs, pl.ds(j % hs, r, stride=hs), :] for p in range(2)], axis=0).astype(BF16))
    a = _dot(m_ref[...].astype(BF16), jnp.concatenate(cols, axis=1))
    for j in range(nb):
        sa_ref[j // hs, pl.ds(j % hs, 2 * r, stride=hs), :] = a[:, j * w:(j + 1) * w]
    a_ref[...] = jnp.concatenate([sa_ref[h].reshape(2, r, hs, w) for h in range(nb // hs)],
                                 axis=2).astype(BF16)


def _dft1(mat, xc, xs):
    ng, t, w = xc.shape
    r = RADIX
    nb = DFT_BATCH
    nh = nb // F32_SUBLANES
    blk = pl.BlockSpec((None, r, nb, w), lambda g, o: (g, 0, o, 0))
    return pl.pallas_call(
        _dft1_kernel,
        grid=(ng, r // nb),
        in_specs=[_const_spec(mat.shape), blk, blk],
        out_specs=pl.BlockSpec((2, r, nb, w), lambda g, o: (0, 0, o, g)),
        out_shape=jax.ShapeDtypeStruct((2, r, r, ng * w), BF16),
        scratch_shapes=[pltpu.VMEM((2, nh, r * F32_SUBLANES, w), F32),
                        pltpu.VMEM((nh, 2 * r * F32_SUBLANES, w), F32)],
        compiler_params=_cparams(2),
        name="dft1",
    )(mat, xc.reshape(ng, r, r, w), xs.reshape(ng, r, r, w))


def _dft2_kernel(cs_ref, tw_ref, a_ref, o_ref, s_ref):
    nb, r, w = a_ref.shape[1], a_ref.shape[2], a_ref.shape[3]
    hs = F32_SUBLANES
    cos_a, sin_a = cs_ref[0], cs_ref[1]
    for j in range(nb):
        cos_b, sin_b = tw_ref[j:j + 1, 0:r], tw_ref[j:j + 1, r:2 * r]
        gmat = jnp.concatenate([cos_a * cos_b - sin_a * sin_b, sin_a * cos_b + cos_a * sin_b], axis=1)
        a = jnp.concatenate([a_ref[0, j], a_ref[1, j]], axis=0)
        y = _dot(gmat.astype(BF16), a)
        for gi in range(N_GROUPS):
            s_ref[gi, j // hs, pl.ds(j % hs, r, stride=hs), :] = y[:, gi * GROUP_DIM:(gi + 1) * GROUP_DIM]
    for gi in range(N_GROUPS):
        o_ref[gi] = jnp.concatenate([s_ref[gi, h].reshape(r, hs, GROUP_DIM) for h in range(nb // hs)],
                                    axis=1).astype(BF16)


def _dft2(base, twiddle, a4):
    _, r, _, w = a4.shape
    nb = DFT_BATCH
    out = pl.pallas_call(
        _dft2_kernel,
        grid=(r // nb,),
        in_specs=[_const_spec(base.shape),
                  pl.BlockSpec((nb, 2 * r), lambda k: (k, 0)),
                  pl.BlockSpec((2, nb, r, w), lambda k: (0, k, 0, 0))],
        out_specs=pl.BlockSpec((N_GROUPS, r, None, nb, GROUP_DIM), lambda k: (0, 0, k, 0, 0)),
        out_shape=jax.ShapeDtypeStruct((N_GROUPS, r, r // nb, nb, GROUP_DIM), BF16),
        scratch_shapes=[pltpu.VMEM((N_GROUPS, nb // F32_SUBLANES, r * F32_SUBLANES, GROUP_DIM), F32)],
        compiler_params=_cparams(),
        name="dft2",
    )(base, twiddle, a4)
    return out.reshape(N_GROUPS, r * r, GROUP_DIM)


def _hgrn_tables(L, reverse):
    nlev = int(np.log2(L))
    idx = np.arange(L)
    t = idx[:, None]
    i = idx[None, :]
    blocks = [(i >= t) if reverse else (i <= t)]
    for j in range(nlev):
        h = L >> (j + 1)
        mid = (t // (2 * h)) * (2 * h) + h
        upper = t >= mid
        if reverse:
            blk = np.where(upper, (i >= mid) & (i < t), (i >= t) & (i < mid))
        else:
            blk = np.where(upper, (i >= mid) & (i <= t), (i > t) & (i < mid))
        blocks.append(blk)
    blocks.append((i < t) if reverse else (i > t))
    return np.concatenate(blocks, axis=0).astype(np.float32), nlev


def _as_column(row):
    n = row.shape[1]
    return jnp.broadcast_to(row, (n, n)).T


def _split_hi_lo(g):
    hi = g.astype(BF16)
    return hi, (g - hi.astype(F32)).astype(BF16)


def _hgrn_chunk_exact(q_ref, v_ref, g_ref, o_ref, st_ref, rows, mall, pair_masks, query_rows,
                      L, nlev, reverse):
    last = 0 if reverse else L - 1
    g = g_ref[rows, :]
    g_hi, g_lo = _split_hi_lo(g)
    ex = jnp.exp(_dot(mall, g_hi) + _dot(mall, g_lo))
    q = q_ref[rows, :].astype(F32)
    v = v_ref[rows, :]
    k = 1.0 - jnp.exp(g)
    e_cum = ex[0:L]
    qe = (q * e_cum).astype(BF16)
    ke = (k * ex[(nlev + 1) * L:(nlev + 2) * L]).astype(BF16)
    zs = [(jnp.where(query_rows[j], q, k) * ex[(j + 1) * L:(j + 2) * L]).astype(BF16)
          for j in range(nlev)]
    qk = q * k
    e_last = e_cum[last:last + 1, :]
    for hd in range(N_HEADS):
        sl = slice(hd * HEAD_DIM, (hd + 1) * HEAD_DIM)
        sc = jnp.zeros((L, L), F32)
        for j in range(nlev):
            zj = zs[j][:, sl]
            sc = jnp.where(pair_masks[j], _dot_nt(zj, zj), sc)
        st = st_ref[hd]
        vh = v[:, sl]
        o = _dot(sc.astype(BF16), vh) + _dot(qe[:, sl], st.astype(BF16))
        o = o + jnp.sum(qk[:, sl], axis=-1, keepdims=True) * vh.astype(F32)
        o_ref[rows, sl] = o
        st_ref[hd] = st * _as_column(e_last[:, sl]) + _dot_tn(ke[:, sl], vh)


def _hgrn_block_fast(q_ref, v_ref, g_ref, trib_ref, o_ref, st_ref, nchunk, L, reverse):
    half = L // 2
    tb = nchunk * L
    chunks = []
    for c in range(nchunk):
        base = c * L
        if reverse:
            chunks.append((slice(base, base + L), slice(base + half, base + L),
                           slice(base, base + half), base + half, base))
        else:
            chunks.append((slice(base, base + L), slice(base, base + half),
                           slice(base + half, base + L), base + half - 1, base + L - 1))
    g = g_ref[...]
    g_hi, g_lo = _split_hi_lo(g)
    b = _dot(trib_ref[...], g_hi) + _dot(trib_ref[...], g_lo)
    q = q_ref[...].astype(F32)
    v = v_ref[...]
    k = 1.0 - jnp.exp(g)
    e_b = jnp.exp(b)
    qe = (q * e_b).astype(BF16)
    c2 = [b[sec] - b[edge:edge + 1, :] for (_, _, sec, edge, _) in chunks]
    q2 = jnp.concatenate([q[ch[2]] * jnp.exp(c2[c]) for c, ch in enumerate(chunks)], axis=0)
    own = []
    for c, (_, fst, _, _, _) in enumerate(chunks):
        own += [c2[c], b[fst]] if reverse else [b[fst], c2[c]]
    kh = (k * jnp.exp(-jnp.concatenate(own, axis=0))).astype(BF16)
    tail = jnp.concatenate([b[last:last + 1, :] - b[rows] for (rows, _, _, _, last) in chunks], axis=0)
    ke = (k * jnp.exp(tail)).astype(BF16)
    lhs = jnp.concatenate([qe, q2.astype(BF16)], axis=0)

    row = lax.broadcasted_iota(jnp.int32, (tb, tb), 0)
    col = lax.broadcasted_iota(jnp.int32, (tb, tb), 1)
    valid = (row // L == col // L) & ((col >= row) if reverse else (col <= row))
    if reverse:
        use_near = (row % L < half) & (col % L < half)
    else:
        use_near = (row % L >= half) & (col % L >= half)

    for hd in range(N_HEADS):
        sl = slice(hd * HEAD_DIM, (hd + 1) * HEAD_DIM)
        s_all = _dot_nt(lhs[:, sl], kh[:, sl])
        far = s_all[0:tb]
        pieces = []
        for c, (_, fst, _, _, _) in enumerate(chunks):
            near_c = s_all[tb + c * half:tb + (c + 1) * half]
            pieces += [near_c, far[fst]] if reverse else [far[fst], near_c]
        near = jnp.concatenate(pieces, axis=0)
        sc = jnp.where(valid, jnp.where(use_near, near, far), 0.0).astype(BF16)
        vh = v[:, sl]
        o_intra = _dot(sc, vh)
        upd = [_dot_tn(ke[rows, sl], vh[rows]) for (rows, _, _, _, _) in chunks]
        st = st_ref[hd]
        o_inter = [None] * nchunk
        for c in (range(nchunk - 1, -1, -1) if reverse else range(nchunk)):
            rows, _, _, _, last = chunks[c]
            o_inter[c] = _dot(qe[rows, sl], st.astype(BF16))
            st = st * _as_column(e_b[last:last + 1, sl]) + upd[c]
        st_ref[hd] = st
        o_ref[:, sl] = o_intra + jnp.concatenate(o_inter, axis=0)


def _hgrn_block_exact(q_ref, v_ref, g_ref, mall_ref, o_ref, st_ref, nchunk, L, nlev, reverse):
    row = lax.broadcasted_iota(jnp.int32, (L, L), 0)
    col = lax.broadcasted_iota(jnp.int32, (L, L), 1)
    rowc = lax.broadcasted_iota(jnp.int32, (L, N_HEADS * HEAD_DIM), 0)
    pair_masks, query_rows = [], []
    for j in range(nlev):
        h = L >> (j + 1)
        same = (row // (2 * h)) == (col // (2 * h))
        row_up = (row // h) % 2 == 1
        col_up = (col // h) % 2 == 1
        if reverse:
            pair_masks.append(same & jnp.logical_not(row_up) & col_up)
            query_rows.append((rowc // h) % 2 == 0)
        else:
            pair_masks.append(same & row_up & jnp.logical_not(col_up))
            query_rows.append((rowc // h) % 2 == 1)
    mall = mall_ref[...]
    for c in (range(nchunk - 1, -1, -1) if reverse else range(nchunk)):
        _hgrn_chunk_exact(q_ref, v_ref, g_ref, o_ref, st_ref, slice(c * L, (c + 1) * L),
                          mall, pair_masks, query_rows, L, nlev, reverse)


def _min_leaf_log_decay(g_ref, leaf):
    g = g_ref[...]
    return jnp.min(jnp.sum(g.reshape(g.shape[0] // leaf, leaf, g.shape[1]), axis=1))


def _hgrn_kernel(qf_ref, vf_ref, gf_ref, qb_ref, vb_ref, gb_ref, mallf_ref, mallb_ref,
                 tribf_ref, tribb_ref, s0f_ref, s0b_ref, of_ref, ob_ref, sff_ref, sfb_ref,
                 stf_ref, stb_ref, *, nchunk, L, nlev):
    i = pl.program_id(0)

    @pl.when(i == 0)
    def _():
        stf_ref[...] = s0f_ref[...]
        stb_ref[...] = s0b_ref[...]

    safe = jnp.minimum(_min_leaf_log_decay(gf_ref, L // 2),
                       _min_leaf_log_decay(gb_ref, L // 2)) >= -SAFE_LOG_DECAY

    @pl.when(safe)
    def _():
        _hgrn_block_fast(qf_ref, vf_ref, gf_ref, tribf_ref, of_ref, stf_ref, nchunk, L, False)
        _hgrn_block_fast(qb_ref, vb_ref, gb_ref, tribb_ref, ob_ref, stb_ref, nchunk, L, True)

    @pl.when(jnp.logical_not(safe))
    def _():
        _hgrn_block_exact(qf_ref, vf_ref, gf_ref, mallf_ref, of_ref, stf_ref, nchunk, L, nlev, False)
        _hgrn_block_exact(qb_ref, vb_ref, gb_ref, mallb_ref, ob_ref, stb_ref, nchunk, L, nlev, True)

    @pl.when(i == pl.num_programs(0) - 1)
    def _():
        sff_ref[...] = stf_ref[...]
        sfb_ref[...] = stb_ref[...]


def _hgrn_scan(q, v, g, s0_f, s0_b, tb):
    t, w = q.shape
    L = HGRN_CHUNK
    nblk = t // tb
    consts = []
    for reverse in (False, True):
        mall_np, nlev = _hgrn_tables(L, reverse)
        consts.append((jnp.asarray(mall_np, dtype=BF16),
                       jnp.asarray(np.kron(np.eye(tb // L, dtype=np.float32), mall_np[0:L]), dtype=BF16)))
    (mall_f, trib_f), (mall_b, trib_b) = consts
    fwd = lambda i: (i, 0)
    bwd = lambda i: (nblk - 1 - i, 0)
    bwd_g = lambda i: (nblk - 1 - i, 1)
    blk = lambda m: pl.BlockSpec((tb, w), m)
    return pl.pallas_call(
        functools.partial(_hgrn_kernel, nchunk=tb // L, L=L, nlev=nlev),
        grid=(nblk,),
        in_specs=[blk(fwd), blk(fwd), blk(fwd), blk(bwd), blk(bwd), blk(bwd_g),
                  _const_spec(mall_f.shape), _const_spec(mall_b.shape),
                  _const_spec(trib_f.shape), _const_spec(trib_b.shape),
                  _const_spec(s0_f.shape), _const_spec(s0_b.shape)],
        out_specs=[blk(fwd), blk(bwd)] + [pl.BlockSpec(s0_f.shape, lambda i: (0, 0, 0))] * 2,
        out_shape=[jax.ShapeDtypeStruct((t, w), F32)] * 2 + [jax.ShapeDtypeStruct(s0_f.shape, F32)] * 2,
        scratch_shapes=[pltpu.VMEM(s0_f.shape, F32)] * 2,
        compiler_params=_cparams(),
        name="hgrn",
    )(q, v, g, q, v, g, mall_f, mall_b, trib_f, trib_b, s0_f, s0_b)


def _merge_kernel(x_ref, er_ref, ec_ref, lng_ref, lnb_ref, mod_ref, wg_ref, bg_ref, of_ref, ob_ref,
                  ng_ref, four_ref, wfp_ref, whp_ref, wo_ref, bo_ref, pg_ref, pb_ref, o_ref, *, tm):
    w4 = N_HEADS * HEAD_DIM
    d = x_ref.shape[1]
    sub = min(tm, SUB_ROWS)
    for r0 in range(0, tm, sub):
        rows = slice(r0, r0 + sub)
        xl, hl = _ln_in_modulated(x_ref, er_ref, ec_ref, lng_ref, lnb_ref, mod_ref, r0, sub)
        hb = hl.astype(BF16)
        og = _dot(hb, wg_ref[:, 0:w4]) + bg_ref[:, 0:w4]
        o = of_ref[rows, :] + ob_ref[rows, :]
        parts = []
        for hd in range(N_HEADS):
            oh = o[:, hd * HEAD_DIM:(hd + 1) * HEAD_DIM]
            ms = jnp.mean(oh * oh, axis=-1, keepdims=True)
            parts.append(oh * lax.rsqrt(ms + RMS_EPS))
        on = jnp.concatenate(parts, axis=1) * ng_ref[...]
        oh = (on * (og * _sigmoid(og))).astype(BF16)
        g_four = _sigmoid(_dot(hb, wg_ref[:, w4:w4 + d]) + bg_ref[:, w4:w4 + d])
        four = jnp.concatenate([four_ref[gi, rows, :] for gi in range(N_GROUPS)], axis=1)
        y = g_four * _dot(four.astype(BF16), wfp_ref[...])
        g_hgrn = _sigmoid(_dot(hb, wg_ref[:, w4 + d:w4 + 2 * d]) + bg_ref[:, w4 + d:w4 + 2 * d])
        y = y + g_hgrn * _dot(oh, whp_ref[...])
        mix = _dot(y.astype(BF16), wo_ref[...]) + bo_ref[...]
        o_ref[rows, :] = _layernorm(ALPHA * xl + mod_ref[2:3, :] * mix, pg_ref[...], pb_ref[...])


def _merge(x, er, ec, lng, lnb, mod, wg, bg, o_f, o_b, ng, four, wfp, whp, wo, bo, pg, pb, tm):
    t, d = x.shape
    w4 = N_HEADS * HEAD_DIM
    nrow = tm // GRID_W
    row = lambda i: (i, 0)
    consts = [ec, lng, lnb, mod, wg, bg]
    consts2 = [wfp, whp, wo, bo, pg, pb]
    return pl.pallas_call(
        functools.partial(_merge_kernel, tm=tm),
        grid=(t // tm,),
        in_specs=[pl.BlockSpec((tm, d), row), pl.BlockSpec((nrow, d // 2), row)]
        + [_const_spec(a.shape) for a in consts]
        + [pl.BlockSpec((tm, w4), row), pl.BlockSpec((tm, w4), row), _const_spec(ng.shape),
           pl.BlockSpec((N_GROUPS, tm, GROUP_DIM), lambda i: (0, i, 0))]
        + [_const_spec(a.shape) for a in consts2],
        out_specs=pl.BlockSpec((tm, d), row),
        out_shape=jax.ShapeDtypeStruct((t, d), F32),
        compiler_params=_cparams(),
        name="merge",
    )(x, er, *consts, o_f, o_b, ng, four, *consts2)


def _mlp_kernel(x_ref, mod_ref, w1_ref, b1_ref, w2_ref, b2_ref, pg_ref, pb_ref, o_ref, *, nsplit):
    dff = w1_ref.shape[1]
    cw = dff // nsplit
    tm = x_ref.shape[0]
    sub = min(tm, SUB_ROWS)
    for r0 in range(0, tm, sub):
        rows = slice(r0, r0 + sub)
        x1 = x_ref[rows, :]
        hb = (x1 * (1.0 + mod_ref[4:5, :]) + mod_ref[3:4, :]).astype(BF16)
        acc = jnp.zeros(x1.shape, F32)
        for c in range(nsplit):
            cs = slice(c * cw, (c + 1) * cw)
            a = jnp.maximum(_dot(hb, w1_ref[:, cs]) + b1_ref[:, cs], 0.0)
            acc = acc + _dot((a * a).astype(BF16), w2_ref[cs, :])
        m = acc + b2_ref[...]
        o_ref[rows, :] = _layernorm(ALPHA * x1 + mod_ref[5:6, :] * m, pg_ref[...], pb_ref[...])


def _mlp(x1, mod, w1, b1, w2, b2, pg, pb, tm):
    t, d = x1.shape
    row = lambda i: (i, 0)
    consts = [mod, w1, b1, w2, b2, pg, pb]
    return pl.pallas_call(
        functools.partial(_mlp_kernel, nsplit=4),
        grid=(t // tm,),
        in_specs=[pl.BlockSpec((tm, d), row)] + [_const_spec(a.shape) for a in consts],
        out_specs=pl.BlockSpec((tm, d), row),
        out_shape=jax.ShapeDtypeStruct((t, d), F32),
        compiler_params=_cparams(),
        name="mlp",
    )(x1, *consts)


def _pos_tables(rows, cols, dim):
    quarter = dim // 4
    omega = 1.0 / (POS_BASE ** (np.arange(quarter, dtype=np.float64) / quarter))
    r = np.arange(rows, dtype=np.float64)[:, None] * omega
    cc = np.arange(cols, dtype=np.float64)[:, None] * omega
    er = np.concatenate([np.sin(r), np.cos(r)], axis=-1)
    ec = np.concatenate([np.sin(cc), np.cos(cc)], axis=-1)
    return jnp.asarray(er.astype(np.float32)), jnp.asarray(ec.astype(np.float32))


def _dft_constants(t):
    n = RADIX
    kn = np.outer(np.arange(n), np.arange(n)).astype(np.float64)
    c = np.cos(2.0 * np.pi * kn / n)
    s = np.sin(2.0 * np.pi * kn / n)
    chan = np.concatenate([c, s], axis=1)
    stage1 = np.block([[c, -s], [-s, -c]])
    scale = 1.0 / np.sqrt(float(t) * GROUP_DIM)
    base2 = np.stack([c, s]) * scale
    beta = 2.0 * np.pi * kn / t
    twiddle = np.concatenate([np.cos(beta), np.sin(beta)], axis=1)
    as_f32 = lambda a: jnp.asarray(a.astype(np.float32))
    return as_f32(chan), as_f32(stage1), as_f32(base2), as_f32(twiddle)


def kernel(x, c, ctx, c_ctx, ln_in_g, ln_in_b, w_ada, b_ada, w_in, b_in, hgrn_lb_logits, hgrn_norm_g,
           w_four_proj, w_hgrn_proj, w_out, b_out, w_mlp1, b_mlp1, w_mlp2, b_mlp2, ln_post_g, ln_post_b):
    B, T, D = x.shape
    assert B == 1 and T == RADIX * RADIX and T % GRID_W == 0
    TC = ctx.shape[1]
    w4 = N_HEADS * HEAD_DIM
    row2 = lambda a: a.reshape(1, -1)

    mod_l, mod_c = _mod_vectors(c[0], c_ctx, w_ada[0], row2(b_ada[0]))
    mod_l = mod_l.reshape(6, D)
    mod_c = mod_c.reshape(6, D)

    er, ec = _pos_tables(T // GRID_W, GRID_W, D)
    lng, lnb = row2(ln_in_g), row2(ln_in_b)
    w_in_b = w_in[0].astype(BF16)
    b_in2 = row2(b_in[0])
    n_a = 5 * w4
    w_a, b_a = w_in_b[:, :n_a], b_in2[:, :n_a]
    w_g, b_g = w_in_b[:, n_a:], b_in2[:, n_a:]
    l0 = hgrn_lb_logits[:, 0, :].reshape(1, 2 * w4)
    l1 = hgrn_lb_logits[:, 1, :].reshape(1, 2 * w4)
    dft_chan, dft_s1, dft_base2, dft_tw = _dft_constants(T)

    zc = jnp.zeros((TC // GRID_W, D // 2), F32)
    _, _, qc, vc, gc = _inproj(ctx[0], zc, jnp.zeros_like(ec), lng, lnb, mod_c, w_a, b_a, dft_chan,
                               l0, l1, tm=TC)
    s_zero = jnp.zeros((N_HEADS, HEAD_DIM, HEAD_DIM), F32)
    _, _, s_f, s_b = _hgrn_scan(qc, vc, gc, s_zero, s_zero, tb=TC)

    xc, xs, q, v, g = _inproj(x[0], er, ec, lng, lnb, mod_l, w_a, b_a, dft_chan, l0, l1, tm=1024)
    four = _dft2(dft_base2, dft_tw, _dft1(dft_s1, xc, xs))
    o_f, o_b, _, _ = _hgrn_scan(q, v, g, s_f, s_b, tb=256)

    x1 = _merge(x[0], er, ec, lng, lnb, mod_l, w_g, b_g, o_f, o_b, row2(hgrn_norm_g[0]), four,
                w_four_proj[0].astype(BF16), w_hgrn_proj[0].astype(BF16), w_out[0].astype(BF16),
                row2(b_out[0]), row2(ln_post_g[0, 0]), row2(ln_post_b[0, 0]), tm=1024)
    out = _mlp(x1, mod_l, w_mlp1[0].astype(BF16), row2(b_mlp1[0]), w_mlp2[0].astype(BF16),
               row2(b_mlp2[0]), row2(ln_post_g[0, 1]), row2(ln_post_b[0, 1]), tm=1024)
    return out[None]
```

```python
import functools

import numpy as np
import jax
import jax.numpy as jnp
from jax import lax
from jax.experimental import pallas as pl
from jax.experimental.pallas import tpu as pltpu

F32 = jnp.float32
BF16 = jnp.bfloat16

GRID_W = 64
N_GROUPS = 4
GROUP_DIM = 128
N_HEADS = 4
HEAD_DIM = 128
POS_BASE = 10000.0
LN_EPS = 1e-5
RMS_EPS = 1e-6
DEPTH = 1
ALPHA = (2.0 * DEPTH) ** 0.25

RADIX = 128
HGRN_CHUNK = 64
HGRN_SUB_BLOCK = 256
SUB_ROWS = 256
SAFE_LOG_DECAY = 80.0
VMEM_LIMIT_BYTES = 56 * 1024 * 1024


def _cparams(n_axes=1):
    return pltpu.CompilerParams(dimension_semantics=("arbitrary",) * n_axes,
                                vmem_limit_bytes=VMEM_LIMIT_BYTES)


def _const_spec(shape):
    nd = len(shape)
    return pl.BlockSpec(shape, lambda *_: (0,) * nd, pipeline_mode=pl.Buffered(1))


def _const_cols(arr, width, j):
    return pl.BlockSpec((arr.shape[0], width), lambda *_: (0, j), pipeline_mode=pl.Buffered(1))


def _sigmoid(x):
    return 1.0 / (1.0 + jnp.exp(-x))


def _layernorm(x, g, b):
    mu = jnp.mean(x, axis=-1, keepdims=True)
    xc = x - mu
    var = jnp.mean(xc * xc, axis=-1, keepdims=True)
    return xc * lax.rsqrt(var + LN_EPS) * g + b


def _dot(a, b):
    return jnp.dot(a, b, preferred_element_type=F32)


def _dot_nt(a, b):
    return lax.dot_general(a, b, (((1,), (1,)), ((), ())), preferred_element_type=F32)


def _dot_tn(a, b):
    return lax.dot_general(a, b, (((0,), (0,)), ((), ())), preferred_element_type=F32)


def _mod_kernel(cl_ref, cx_ref, w_ref, b_ref, ol_ref, ox_ref):
    w = w_ref[...]
    for c_ref, o_ref in ((cl_ref, ol_ref), (cx_ref, ox_ref)):
        cs = c_ref[...]
        s = cs * _sigmoid(cs)
        o_ref[...] = jnp.sum(s * w, axis=0, keepdims=True) + b_ref[...]


def _mod_vectors(c_lat, c_ctx, w_ada, b_ada):
    d, n = w_ada.shape
    tn = 1536
    col = lambda j: (0, j)
    return pl.pallas_call(
        _mod_kernel,
        grid=(n // tn,),
        in_specs=[_const_spec((d, 1)), _const_spec((d, 1)),
                  pl.BlockSpec((d, tn), col), pl.BlockSpec((1, tn), col)],
        out_specs=[pl.BlockSpec((1, tn), col)] * 2,
        out_shape=[jax.ShapeDtypeStruct((1, n), F32)] * 2,
        compiler_params=_cparams(),
        name="mod",
    )(c_lat.reshape(d, 1), c_ctx.reshape(d, 1), w_ada, b_ada)


def _ln_in_modulated(x_ref, er_ref, ec_ref, lng_ref, lnb_ref, mod_ref, r0, nr):
    x = x_ref[r0:r0 + nr, :]
    half = x.shape[1] // 2
    nrow = nr // GRID_W
    e0 = r0 // GRID_W
    left = jnp.concatenate(
        [jnp.broadcast_to(er_ref[e0 + r:e0 + r + 1, :], (GRID_W, half)) for r in range(nrow)], axis=0)
    right = jnp.concatenate([ec_ref[...]] * nrow, axis=0)
    xp = jnp.concatenate([x[:, :half] + left, x[:, half:] + right], axis=1)
    xl = _layernorm(xp, lng_ref[...], lnb_ref[...])
    hl = xl * (1.0 + mod_ref[1:2, :]) + mod_ref[0:1, :]
    return xl, hl


def _inproj_kernel(x_ref, er_ref, ec_ref, lng_ref, lnb_ref, mod_ref, w_ref, b_ref, dft_ref,
                   l0_ref, l1_ref, xc_ref, xs_ref, q_ref, v_ref, g_ref, *, tm):
    w4 = N_GROUPS * GROUP_DIM
    dft = dft_ref[...].astype(BF16)
    l0 = l0_ref[...]
    l1 = l1_ref[...]
    m = jnp.maximum(l0, l1)
    e0 = jnp.exp(l0 - m)
    lb = e0 / (e0 + jnp.exp(l1 - m))
    sub = min(tm, SUB_ROWS)
    for r0 in range(0, tm, sub):
        rows = slice(r0, r0 + sub)
        _, hl = _ln_in_modulated(x_ref, er_ref, ec_ref, lng_ref, lnb_ref, mod_ref, r0, sub)
        hb = hl.astype(BF16)
        fp = _dot(hb, w_ref[:, 3 * w4:5 * w4]) + b_ref[:, 3 * w4:5 * w4]
        g_ref[rows, :] = jnp.log(lb + (1.0 - lb) * _sigmoid(fp))
        qp = _dot(hb, w_ref[:, w4:2 * w4]) + b_ref[:, w4:2 * w4]
        q_ref[rows, :] = (qp * _sigmoid(qp)).astype(BF16)
        u = (_dot(hb, w_ref[:, 0:w4]) + b_ref[:, 0:w4]).astype(BF16)
        for gi in range(N_GROUPS):
            z = _dot(u[:, gi * GROUP_DIM:(gi + 1) * GROUP_DIM], dft)
            xc_ref[gi, rows, :] = z[:, :GROUP_DIM].astype(BF16)
            xs_ref[gi, rows, :] = z[:, GROUP_DIM:].astype(BF16)
        v_ref[rows, :] = (_dot(hb, w_ref[:, 2 * w4:3 * w4]) + b_ref[:, 2 * w4:3 * w4]).astype(BF16)


def _inproj(x, er, ec, lng, lnb, mod, w, b, dft, l0, l1, tm):
    t, d = x.shape
    w4 = N_GROUPS * GROUP_DIM
    nrow = tm // GRID_W
    row = lambda i: (i, 0)
    return pl.pallas_call(
        functools.partial(_inproj_kernel, tm=tm),
        grid=(t // tm,),
        in_specs=[pl.BlockSpec((tm, d), row),
                  pl.BlockSpec((nrow, d // 2), row),
                  _const_spec(ec.shape), _const_spec(lng.shape), _const_spec(lnb.shape),
                  _const_spec(mod.shape), _const_cols(w, 5 * w4, 0), _const_cols(b, 5 * w4, 0),
                  _const_spec(dft.shape), _const_spec(l0.shape), _const_spec(l1.shape)],
        out_specs=[pl.BlockSpec((N_GROUPS, tm, GROUP_DIM), lambda i: (0, i, 0))] * 2
        + [pl.BlockSpec((tm, w4), row)] * 2 + [pl.BlockSpec((tm, 2 * w4), row)],
        out_shape=[jax.ShapeDtypeStruct((N_GROUPS, t, GROUP_DIM), BF16)] * 2
        + [jax.ShapeDtypeStruct((t, w4), BF16)] * 2 + [jax.ShapeDtypeStruct((t, 2 * w4), F32)],
        compiler_params=_cparams(),
        name="inproj",
    )(x, er, ec, lng, lnb, mod, w, b, dft, l0, l1)


DFT_BATCH = 16
F32_SUBLANES = 8


def _dft1_kernel(m_ref, xc_ref, xs_ref, a_ref, sx_ref, sa_ref):
    r, nb, w = xc_ref.shape
    hs = F32_SUBLANES
    for p, x_ref in enumerate((xc_ref, xs_ref)):
        x = x_ref[...].astype(F32)
        for h in range(nb // hs):
            sx_ref[p, h] = x[:, h * hs:(h + 1) * hs, :].reshape(r * hs, w)
    cols = []
    for j in range(nb):
        cols.append(jnp.concatenate(
            [sx_ref[p, j // hs, pl.ds(j % hs, r, stride=hs), :] for p in range(2)], axis=0).astype(BF16))
    a = _dot(m_ref[...].astype(BF16), jnp.concatenate(cols, axis=1))
    for j in range(nb):
        sa_ref[j // hs, pl.ds(j % hs, 2 * r, stride=hs), :] = a[:, j * w:(j + 1) * w]
    a_ref[...] = jnp.concatenate([sa_ref[h].reshape(2, r, hs, w) for h in range(nb // hs)],
                                 axis=2).astype(BF16)


def _dft1(mat, xc, xs):
    ng, t, w = xc.shape
    r = RADIX
    nb = 2 * DFT_BATCH
    nh = nb // F32_SUBLANES
    blk = pl.BlockSpec((None, r, nb, w), lambda g, o: (g, 0, o, 0))
    return pl.pallas_call(
        _dft1_kernel,
        grid=(ng, r // nb),
        in_specs=[_const_spec(mat.shape), blk, blk],
        out_specs=pl.BlockSpec((2, r, nb, w), lambda g, o: (0, 0, o, g)),
        out_shape=jax.ShapeDtypeStruct((2, r, r, ng * w), BF16),
        scratch_shapes=[pltpu.VMEM((2, nh, r * F32_SUBLANES, w), F32),
                        pltpu.VMEM((nh, 2 * r * F32_SUBLANES, w), F32)],
        compiler_params=_cparams(2),
        name="dft1",
    )(mat, xc.reshape(ng, r, r, w), xs.reshape(ng, r, r, w))


def _dft2_kernel(cs_ref, tw_ref, a_ref, o_ref, s_ref):
    nb, r, w = a_ref.shape[1], a_ref.shape[2], a_ref.shape[3]
    hs = F32_SUBLANES
    cos_a, sin_a = cs_ref[0], cs_ref[1]
    for j in range(nb):
        cos_b, sin_b = tw_ref[j:j + 1, 0:r], tw_ref[j:j + 1, r:2 * r]
        gmat = jnp.concatenate([cos_a * cos_b - sin_a * sin_b, sin_a * cos_b + cos_a * sin_b], axis=1)
        a = jnp.concatenate([a_ref[0, j], a_ref[1, j]], axis=0)
        y = _dot(gmat.astype(BF16), a)
        for gi in range(N_GROUPS):
            s_ref[gi, j // hs, pl.ds(j % hs, r, stride=hs), :] = y[:, gi * GROUP_DIM:(gi + 1) * GROUP_DIM]
    for gi in range(N_GROUPS):
        o_ref[gi] = jnp.concatenate([s_ref[gi, h].reshape(r, hs, GROUP_DIM) for h in range(nb // hs)],
                                    axis=1).astype(BF16)


def _dft2(base, twiddle, a4):
    _, r, _, w = a4.shape
    nb = DFT_BATCH
    out = pl.pallas_call(
        _dft2_kernel,
        grid=(r // nb,),
        in_specs=[_const_spec(base.shape),
                  pl.BlockSpec((nb, 2 * r), lambda k: (k, 0)),
                  pl.BlockSpec((2, nb, r, w), lambda k: (0, k, 0, 0))],
        out_specs=pl.BlockSpec((N_GROUPS, r, None, nb, GROUP_DIM), lambda k: (0, 0, k, 0, 0)),
        out_shape=jax.ShapeDtypeStruct((N_GROUPS, r, r // nb, nb, GROUP_DIM), BF16),
        scratch_shapes=[pltpu.VMEM((N_GROUPS, nb // F32_SUBLANES, r * F32_SUBLANES, GROUP_DIM), F32)],
        compiler_params=_cparams(),
        name="dft2",
    )(base, twiddle, a4)
    return out.reshape(N_GROUPS, r * r, GROUP_DIM)


def _hgrn_tables(L, reverse):
    nlev = int(np.log2(L))
    idx = np.arange(L)
    t = idx[:, None]
    i = idx[None, :]
    blocks = [(i >= t) if reverse else (i <= t)]
    for j in range(nlev):
        h = L >> (j + 1)
        mid = (t // (2 * h)) * (2 * h) + h
        upper = t >= mid
        if reverse:
            blk = np.where(upper, (i >= mid) & (i < t), (i >= t) & (i < mid))
        else:
            blk = np.where(upper, (i >= mid) & (i <= t), (i > t) & (i < mid))
        blocks.append(blk)
    blocks.append((i < t) if reverse else (i > t))
    return np.concatenate(blocks, axis=0).astype(np.float32), nlev


def _as_column(row):
    n = row.shape[1]
    return jnp.broadcast_to(row, (n, n)).T


def _split_hi_lo(g):
    hi = g.astype(BF16)
    return hi, (g - hi.astype(F32)).astype(BF16)


def _hgrn_chunk_exact(q_ref, v_ref, g_ref, o_ref, st_ref, rows, mall, pair_masks, query_rows,
                      L, nlev, reverse):
    last = 0 if reverse else L - 1
    g = g_ref[rows, :]
    g_hi, g_lo = _split_hi_lo(g)
    ex = jnp.exp(_dot(mall, g_hi) + _dot(mall, g_lo))
    q = q_ref[rows, :].astype(F32)
    v = v_ref[rows, :]
    k = 1.0 - jnp.exp(g)
    e_cum = ex[0:L]
    qe = (q * e_cum).astype(BF16)
    ke = (k * ex[(nlev + 1) * L:(nlev + 2) * L]).astype(BF16)
    zs = [(jnp.where(query_rows[j], q, k) * ex[(j + 1) * L:(j + 2) * L]).astype(BF16)
          for j in range(nlev)]
    qk = q * k
    e_last = e_cum[last:last + 1, :]
    for hd in range(N_HEADS):
        sl = slice(hd * HEAD_DIM, (hd + 1) * HEAD_DIM)
        sc = jnp.zeros((L, L), F32)
        for j in range(nlev):
            zj = zs[j][:, sl]
            sc = jnp.where(pair_masks[j], _dot_nt(zj, zj), sc)
        st = st_ref[hd]
        vh = v[:, sl]
        o = _dot(sc.astype(BF16), vh) + _dot(qe[:, sl], st.astype(BF16))
        o = o + jnp.sum(qk[:, sl], axis=-1, keepdims=True) * vh.astype(F32)
        o_ref[rows, sl] = o
        st_ref[hd] = st * _as_column(e_last[:, sl]) + _dot_tn(ke[:, sl], vh)


def _hgrn_block_fast(q_ref, v_ref, g_ref, trib_ref, o_ref, st_ref, r0, nchunk, L, reverse):
    half = L // 2
    tb = nchunk * L
    blk = slice(r0, r0 + tb)
    chunks = []
    for c in range(nchunk):
        base = c * L
        if reverse:
            chunks.append((slice(base, base + L), slice(base + half, base + L),
                           slice(base, base + half), base + half, base))
        else:
            chunks.append((slice(base, base + L), slice(base, base + half),
                           slice(base + half, base + L), base + half - 1, base + L - 1))
    g = g_ref[blk, :]
    g_hi, g_lo = _split_hi_lo(g)
    b = _dot(trib_ref[...], g_hi) + _dot(trib_ref[...], g_lo)
    q = q_ref[blk, :].astype(F32)
    v = v_ref[blk, :]
    k = 1.0 - jnp.exp(g)
    e_b = jnp.exp(b)
    qe = (q * e_b).astype(BF16)
    c2 = [b[sec] - b[edge:edge + 1, :] for (_, _, sec, edge, _) in chunks]
    q2 = jnp.concatenate([q[ch[2]] * jnp.exp(c2[c]) for c, ch in enumerate(chunks)], axis=0)
    own = []
    for c, (_, fst, _, _, _) in enumerate(chunks):
        own += [c2[c], b[fst]] if reverse else [b[fst], c2[c]]
    kh = (k * jnp.exp(-jnp.concatenate(own, axis=0))).astype(BF16)
    tail = jnp.concatenate([b[last:last + 1, :] - b[rows] for (rows, _, _, _, last) in chunks], axis=0)
    ke = (k * jnp.exp(tail)).astype(BF16)
    lhs = jnp.concatenate([qe, q2.astype(BF16)], axis=0)

    row = lax.broadcasted_iota(jnp.int32, (tb, tb), 0)
    col = lax.broadcasted_iota(jnp.int32, (tb, tb), 1)
    valid = (row // L == col // L) & ((col >= row) if reverse else (col <= row))
    if reverse:
        use_near = (row % L < half) & (col % L < half)
    else:
        use_near = (row % L >= half) & (col % L >= half)

    for hd in range(N_HEADS):
        sl = slice(hd * HEAD_DIM, (hd + 1) * HEAD_DIM)
        s_all = _dot_nt(lhs[:, sl], kh[:, sl])
        far = s_all[0:tb]
        pieces = []
        for c, (_, fst, _, _, _) in enumerate(chunks):
            near_c = s_all[tb + c * half:tb + (c + 1) * half]
            pieces += [near_c, far[fst]] if reverse else [far[fst], near_c]
        near = jnp.concatenate(pieces, axis=0)
        sc = jnp.where(valid, jnp.where(use_near, near, far), 0.0).astype(BF16)
        vh = v[:, sl]
        o_intra = _dot(sc, vh)
        upd = [_dot_tn(ke[rows, sl], vh[rows]) for (rows, _, _, _, _) in chunks]
        st = st_ref[hd]
        o_inter = [None] * nchunk
        for c in (range(nchunk - 1, -1, -1) if reverse else range(nchunk)):
            rows, _, _, _, last = chunks[c]
            o_inter[c] = _dot(qe[rows, sl], st.astype(BF16))
            st = st * _as_column(e_b[last:last + 1, sl]) + upd[c]
        st_ref[hd] = st
        o_ref[blk, sl] = o_intra + jnp.concatenate(o_inter, axis=0)


def _hgrn_block_exact(q_ref, v_ref, g_ref, mall_ref, o_ref, st_ref, nchunk, L, nlev, reverse):
    row = lax.broadcasted_iota(jnp.int32, (L, L), 0)
    col = lax.broadcasted_iota(jnp.int32, (L, L), 1)
    rowc = lax.broadcasted_iota(jnp.int32, (L, N_HEADS * HEAD_DIM), 0)
    pair_masks, query_rows = [], []
    for j in range(nlev):
        h = L >> (j + 1)
        same = (row // (2 * h)) == (col // (2 * h))
        row_up = (row // h) % 2 == 1
        col_up = (col // h) % 2 == 1
        if reverse:
            pair_masks.append(same & jnp.logical_not(row_up) & col_up)
            query_rows.append((rowc // h) % 2 == 0)
        else:
            pair_masks.append(same & row_up & jnp.logical_not(col_up))
            query_rows.append((rowc // h) % 2 == 1)
    mall = mall_ref[...]
    for c in (range(nchunk - 1, -1, -1) if reverse else range(nchunk)):
        _hgrn_chunk_exact(q_ref, v_ref, g_ref, o_ref, st_ref, slice(c * L, (c + 1) * L),
                          mall, pair_masks, query_rows, L, nlev, reverse)


def _min_leaf_log_decay(g_ref, leaf):
    g = g_ref[...]
    return jnp.min(jnp.sum(g.reshape(g.shape[0] // leaf, leaf, g.shape[1]), axis=1))


def _hgrn_kernel(qf_ref, vf_ref, gf_ref, qb_ref, vb_ref, gb_ref, mallf_ref, mallb_ref,
                 tribf_ref, tribb_ref, s0f_ref, s0b_ref, of_ref, ob_ref, sff_ref, sfb_ref,
                 stf_ref, stb_ref, *, nsub, nchunk, L, nlev):
    i = pl.program_id(0)

    @pl.when(i == 0)
    def _():
        stf_ref[...] = s0f_ref[...]
        stb_ref[...] = s0b_ref[...]

    safe = jnp.minimum(_min_leaf_log_decay(gf_ref, L // 2),
                       _min_leaf_log_decay(gb_ref, L // 2)) >= -SAFE_LOG_DECAY

    @pl.when(safe)
    def _():
        for j in range(nsub):
            _hgrn_block_fast(qf_ref, vf_ref, gf_ref, tribf_ref, of_ref, stf_ref,
                             j * nchunk * L, nchunk, L, False)
            _hgrn_block_fast(qb_ref, vb_ref, gb_ref, tribb_ref, ob_ref, stb_ref,
                             (nsub - 1 - j) * nchunk * L, nchunk, L, True)

    @pl.when(jnp.logical_not(safe))
    def _():
        nc = nsub * nchunk
        _hgrn_block_exact(qf_ref, vf_ref, gf_ref, mallf_ref, of_ref, stf_ref, nc, L, nlev, False)
        _hgrn_block_exact(qb_ref, vb_ref, gb_ref, mallb_ref, ob_ref, stb_ref, nc, L, nlev, True)

    @pl.when(i == pl.num_programs(0) - 1)
    def _():
        sff_ref[...] = stf_ref[...]
        sfb_ref[...] = stb_ref[...]


def _hgrn_scan(q, v, g, s0_f, s0_b, tb):
    t, w = q.shape
    L = HGRN_CHUNK
    nblk = t // tb
    sb = min(tb, HGRN_SUB_BLOCK)
    consts = []
    for reverse in (False, True):
        mall_np, nlev = _hgrn_tables(L, reverse)
        consts.append((jnp.asarray(mall_np, dtype=BF16),
                       jnp.asarray(np.kron(np.eye(sb // L, dtype=np.float32), mall_np[0:L]), dtype=BF16)))
    (mall_f, trib_f), (mall_b, trib_b) = consts
    fwd = lambda i: (i, 0)
    bwd = lambda i: (nblk - 1 - i, 0)
    bwd_g = lambda i: (nblk - 1 - i, 1)
    blk = lambda m: pl.BlockSpec((tb, w), m)
    return pl.pallas_call(
        functools.partial(_hgrn_kernel, nsub=tb // sb, nchunk=sb // L, L=L, nlev=nlev),
        grid=(nblk,),
        in_specs=[blk(fwd), blk(fwd), blk(fwd), blk(bwd), blk(bwd), blk(bwd_g),
                  _const_spec(mall_f.shape), _const_spec(mall_b.shape),
                  _const_spec(trib_f.shape), _const_spec(trib_b.shape),
                  _const_spec(s0_f.shape), _const_spec(s0_b.shape)],
        out_specs=[blk(fwd), blk(bwd)] + [pl.BlockSpec(s0_f.shape, lambda i: (0, 0, 0))] * 2,
        out_shape=[jax.ShapeDtypeStruct((t, w), F32)] * 2 + [jax.ShapeDtypeStruct(s0_f.shape, F32)] * 2,
        scratch_shapes=[pltpu.VMEM(s0_f.shape, F32)] * 2,
        compiler_params=_cparams(),
        name="hgrn",
    )(q, v, g, q, v, g, mall_f, mall_b, trib_f, trib_b, s0_f, s0_b)


def _merge_kernel(x_ref, er_ref, ec_ref, lng_ref, lnb_ref, mod_ref, wg_ref, bg_ref, of_ref, ob_ref,
                  ng_ref, four_ref, wfp_ref, whp_ref, wo_ref, bo_ref, pg_ref, pb_ref, o_ref, *, tm):
    w4 = N_HEADS * HEAD_DIM
    d = x_ref.shape[1]
    sub = min(tm, SUB_ROWS)
    for r0 in range(0, tm, sub):
        rows = slice(r0, r0 + sub)
        xl, hl = _ln_in_modulated(x_ref, er_ref, ec_ref, lng_ref, lnb_ref, mod_ref, r0, sub)
        hb = hl.astype(BF16)
        og = _dot(hb, wg_ref[:, 0:w4]) + bg_ref[:, 0:w4]
        o = of_ref[rows, :] + ob_ref[rows, :]
        parts = []
        for hd in range(N_HEADS):
            oh = o[:, hd * HEAD_DIM:(hd + 1) * HEAD_DIM]
            ms = jnp.mean(oh * oh, axis=-1, keepdims=True)
            parts.append(oh * lax.rsqrt(ms + RMS_EPS))
        on = jnp.concatenate(parts, axis=1) * ng_ref[...]
        oh = (on * (og * _sigmoid(og))).astype(BF16)
        g_four = _sigmoid(_dot(hb, wg_ref[:, w4:w4 + d]) + bg_ref[:, w4:w4 + d])
        four = jnp.concatenate([four_ref[gi, rows, :] for gi in range(N_GROUPS)], axis=1)
        y = g_four * _dot(four.astype(BF16), wfp_ref[...])
        g_hgrn = _sigmoid(_dot(hb, wg_ref[:, w4 + d:w4 + 2 * d]) + bg_ref[:, w4 + d:w4 + 2 * d])
        y = y + g_hgrn * _dot(oh, whp_ref[...])
        mix = _dot(y.astype(BF16), wo_ref[...]) + bo_ref[...]
        o_ref[rows, :] = _layernorm(ALPHA * xl + mod_ref[2:3, :] * mix, pg_ref[...], pb_ref[...])


def _merge(x, er, ec, lng, lnb, mod, wg, bg, o_f, o_b, ng, four, wfp, whp, wo, bo, pg, pb, tm):
    t, d = x.shape
    w4 = N_HEADS * HEAD_DIM
    nrow = tm // GRID_W
    row = lambda i: (i, 0)
    consts = [ec, lng, lnb, mod]
    consts2 = [wfp, whp, wo, bo, pg, pb]
    return pl.pallas_call(
        functools.partial(_merge_kernel, tm=tm),
        grid=(t // tm,),
        in_specs=[pl.BlockSpec((tm, d), row), pl.BlockSpec((nrow, d // 2), row)]
        + [_const_spec(a.shape) for a in consts]
        + [_const_cols(wg, w4 + 2 * d, 1), _const_cols(bg, w4 + 2 * d, 1)]
        + [pl.BlockSpec((tm, w4), row), pl.BlockSpec((tm, w4), row), _const_spec(ng.shape),
           pl.BlockSpec((N_GROUPS, tm, GROUP_DIM), lambda i: (0, i, 0))]
        + [_const_spec(a.shape) for a in consts2],
        out_specs=pl.BlockSpec((tm, d), row),
        out_shape=jax.ShapeDtypeStruct((t, d), F32),
        compiler_params=_cparams(),
        name="merge",
    )(x, er, *consts, wg, bg, o_f, o_b, ng, four, *consts2)


def _mlp_kernel(x_ref, mod_ref, w1_ref, b1_ref, w2_ref, b2_ref, pg_ref, pb_ref, o_ref, *, nsplit):
    dff = w1_ref.shape[1]
    cw = dff // nsplit
    tm = x_ref.shape[0]
    sub = min(tm, SUB_ROWS)
    for r0 in range(0, tm, sub):
        rows = slice(r0, r0 + sub)
        x1 = x_ref[rows, :]
        hb = (x1 * (1.0 + mod_ref[4:5, :]) + mod_ref[3:4, :]).astype(BF16)
        acc = jnp.zeros(x1.shape, F32)
        for c in range(nsplit):
            cs = slice(c * cw, (c + 1) * cw)
            a = jnp.maximum(_dot(hb, w1_ref[:, cs]) + b1_ref[:, cs], 0.0)
            acc = acc + _dot((a * a).astype(BF16), w2_ref[cs, :])
        m = acc + b2_ref[...]
        o_ref[rows, :] = _layernorm(ALPHA * x1 + mod_ref[5:6, :] * m, pg_ref[...], pb_ref[...])


def _mlp(x1, mod, w1, b1, w2, b2, pg, pb, tm):
    t, d = x1.shape
    row = lambda i: (i, 0)
    consts = [mod, w1, b1, w2, b2, pg, pb]
    return pl.pallas_call(
        functools.partial(_mlp_kernel, nsplit=4),
        grid=(t // tm,),
        in_specs=[pl.BlockSpec((tm, d), row)] + [_const_spec(a.shape) for a in consts],
        out_specs=pl.BlockSpec((tm, d), row),
        out_shape=jax.ShapeDtypeStruct((t, d), F32),
        compiler_params=_cparams(),
        name="mlp",
    )(x1, *consts)


def _pos_tables(rows, cols, dim):
    quarter = dim // 4
    omega = 1.0 / (POS_BASE ** (np.arange(quarter, dtype=np.float64) / quarter))
    r = np.arange(rows, dtype=np.float64)[:, None] * omega
    cc = np.arange(cols, dtype=np.float64)[:, None] * omega
    er = np.concatenate([np.sin(r), np.cos(r)], axis=-1)
    ec = np.concatenate([np.sin(cc), np.cos(cc)], axis=-1)
    return jnp.asarray(er.astype(np.float32)), jnp.asarray(ec.astype(np.float32))


def _dft_constants(t):
    n = RADIX
    kn = np.outer(np.arange(n), np.arange(n)).astype(np.float64)
    c = np.cos(2.0 * np.pi * kn / n)
    s = np.sin(2.0 * np.pi * kn / n)
    chan = np.concatenate([c, s], axis=1)
    stage1 = np.block([[c, -s], [-s, -c]])
    scale = 1.0 / np.sqrt(float(t) * GROUP_DIM)
    base2 = np.stack([c, s]) * scale
    beta = 2.0 * np.pi * kn / t
    twiddle = np.concatenate([np.cos(beta), np.sin(beta)], axis=1)
    as_f32 = lambda a: jnp.asarray(a.astype(np.float32))
    return as_f32(chan), as_f32(stage1), as_f32(base2), as_f32(twiddle)


def kernel(x, c, ctx, c_ctx, ln_in_g, ln_in_b, w_ada, b_ada, w_in, b_in, hgrn_lb_logits, hgrn_norm_g,
           w_four_proj, w_hgrn_proj, w_out, b_out, w_mlp1, b_mlp1, w_mlp2, b_mlp2, ln_post_g, ln_post_b):
    B, T, D = x.shape
    assert B == 1 and T == RADIX * RADIX and T % GRID_W == 0
    TC = ctx.shape[1]
    w4 = N_HEADS * HEAD_DIM
    row2 = lambda a: a.reshape(1, -1)

    mod_l, mod_c = _mod_vectors(c[0], c_ctx, w_ada[0], row2(b_ada[0]))
    mod_l = mod_l.reshape(6, D)
    mod_c = mod_c.reshape(6, D)

    er, ec = _pos_tables(T // GRID_W, GRID_W, D)
    lng, lnb = row2(ln_in_g), row2(ln_in_b)
    w_in_b = w_in[0].astype(BF16)
    b_in2 = row2(b_in[0])
    assert w_in_b.shape[1] == 2 * 5 * w4
    w_a = w_g = w_in_b
    b_a = b_g = b_in2
    l0 = hgrn_lb_logits[:, 0, :].reshape(1, 2 * w4)
    l1 = hgrn_lb_logits[:, 1, :].reshape(1, 2 * w4)
    dft_chan, dft_s1, dft_base2, dft_tw = _dft_constants(T)

    zc = jnp.zeros((TC // GRID_W, D // 2), F32)
    _, _, qc, vc, gc = _inproj(ctx[0], zc, jnp.zeros_like(ec), lng, lnb, mod_c, w_a, b_a, dft_chan,
                               l0, l1, tm=TC)
    s_zero = jnp.zeros((N_HEADS, HEAD_DIM, HEAD_DIM), F32)
    _, _, s_f, s_b = _hgrn_scan(qc, vc, gc, s_zero, s_zero, tb=TC)

    xc, xs, q, v, g = _inproj(x[0], er, ec, lng, lnb, mod_l, w_a, b_a, dft_chan, l0, l1, tm=1024)
    four = _dft2(dft_base2, dft_tw, _dft1(dft_s1, xc, xs))
    o_f, o_b, _, _ = _hgrn_scan(q, v, g, s_f, s_b, tb=512)

    x1 = _merge(x[0], er, ec, lng, lnb, mod_l, w_g, b_g, o_f, o_b, row2(hgrn_norm_g[0]), four,
                w_four_proj[0].astype(BF16), w_hgrn_proj[0].astype(BF16), w_out[0].astype(BF16),
                row2(b_out[0]), row2(ln_post_g[0, 0]), row2(ln_post_b[0, 0]), tm=1024)
    out = _mlp(x1, mod_l, w_mlp1[0].astype(BF16), row2(b_mlp1[0]), w_mlp2[0].astype(BF16),
               row2(b_mlp2[0]), row2(ln_post_g[0, 1]), row2(ln_post_b[0, 1]), tm=1024)
    return out[None]
```

```python
import functools

import numpy as np
import jax
import jax.numpy as jnp
from jax import lax
from jax.experimental import pallas as pl
from jax.experimental.pallas import tpu as pltpu

F32 = jnp.float32
BF16 = jnp.bfloat16

GRID_W = 64
N_GROUPS = 4
GROUP_DIM = 128
N_HEADS = 4
HEAD_DIM = 128
POS_BASE = 10000.0
LN_EPS = 1e-5
RMS_EPS = 1e-6
DEPTH = 1
ALPHA = (2.0 * DEPTH) ** 0.25

RADIX = 128
HGRN_CHUNK = 64
HGRN_SUB_BLOCK = 256
SUB_ROWS = 256
SAFE_LOG_DECAY = 80.0
VMEM_LIMIT_BYTES = 56 * 1024 * 1024


def _cparams(n_axes=1):
    return pltpu.CompilerParams(dimension_semantics=("arbitrary",) * n_axes,
                                vmem_limit_bytes=VMEM_LIMIT_BYTES)


def _const_spec(shape):
    nd = len(shape)
    return pl.BlockSpec(shape, lambda *_: (0,) * nd, pipeline_mode=pl.Buffered(1))


def _const_cols(arr, width, j):
    return pl.BlockSpec((arr.shape[0], width), lambda *_: (0, j), pipeline_mode=pl.Buffered(1))


def _sigmoid(x):
    return 1.0 / (1.0 + jnp.exp(-x))


def _layernorm(x, g, b):
    mu = jnp.mean(x, axis=-1, keepdims=True)
    xc = x - mu
    var = jnp.mean(xc * xc, axis=-1, keepdims=True)
    return xc * lax.rsqrt(var + LN_EPS) * g + b


def _dot(a, b):
    return jnp.dot(a, b, preferred_element_type=F32)


def _dot_nt(a, b):
    return lax.dot_general(a, b, (((1,), (1,)), ((), ())), preferred_element_type=F32)


def _dot_tn(a, b):
    return lax.dot_general(a, b, (((0,), (0,)), ((), ())), preferred_element_type=F32)


def _mod_kernel(cl_ref, cx_ref, w_ref, b_ref, ol_ref, ox_ref):
    w = w_ref[...]
    for c_ref, o_ref in ((cl_ref, ol_ref), (cx_ref, ox_ref)):
        cs = c_ref[...]
        s = cs * _sigmoid(cs)
        o_ref[...] = jnp.sum(s * w, axis=0, keepdims=True) + b_ref[...]


def _mod_vectors(c_lat, c_ctx, w_ada, b_ada):
    d, n = w_ada.shape
    tn = 1536
    col = lambda j: (0, j)
    return pl.pallas_call(
        _mod_kernel,
        grid=(n // tn,),
        in_specs=[_const_spec((d, 1)), _const_spec((d, 1)),
                  pl.BlockSpec((d, tn), col), pl.BlockSpec((1, tn), col)],
        out_specs=[pl.BlockSpec((1, tn), col)] * 2,
        out_shape=[jax.ShapeDtypeStruct((1, n), F32)] * 2,
        compiler_params=_cparams(),
        name="mod",
    )(c_lat.reshape(d, 1), c_ctx.reshape(d, 1), w_ada, b_ada)


def _ln_in_modulated(x_ref, er_ref, ec_ref, lng_ref, lnb_ref, mod_ref, r0, nr):
    x = x_ref[r0:r0 + nr, :]
    half = x.shape[1] // 2
    nrow = nr // GRID_W
    e0 = r0 // GRID_W
    left = jnp.concatenate(
        [jnp.broadcast_to(er_ref[e0 + r:e0 + r + 1, :], (GRID_W, half)) for r in range(nrow)], axis=0)
    right = jnp.concatenate([ec_ref[...]] * nrow, axis=0)
    xp = jnp.concatenate([x[:, :half] + left, x[:, half:] + right], axis=1)
    xl = _layernorm(xp, lng_ref[...], lnb_ref[...])
    hl = xl * (1.0 + mod_ref[1:2, :]) + mod_ref[0:1, :]
    return xl, hl


def _inproj_kernel(x_ref, er_ref, ec_ref, lng_ref, lnb_ref, mod_ref, w_ref, b_ref, dft_ref,
                   l0_ref, l1_ref, xc_ref, xs_ref, q_ref, v_ref, g_ref, *, tm):
    w4 = N_GROUPS * GROUP_DIM
    dft = dft_ref[...].astype(BF16)
    l0 = l0_ref[...]
    l1 = l1_ref[...]
    m = jnp.maximum(l0, l1)
    e0 = jnp.exp(l0 - m)
    lb = e0 / (e0 + jnp.exp(l1 - m))
    sub = min(tm, SUB_ROWS)
    for r0 in range(0, tm, sub):
        rows = slice(r0, r0 + sub)
        _, hl = _ln_in_modulated(x_ref, er_ref, ec_ref, lng_ref, lnb_ref, mod_ref, r0, sub)
        hb = hl.astype(BF16)
        fp = _dot(hb, w_ref[:, 3 * w4:5 * w4]) + b_ref[:, 3 * w4:5 * w4]
        g_ref[rows, :] = jnp.log(lb + (1.0 - lb) * _sigmoid(fp))
        qp = _dot(hb, w_ref[:, w4:2 * w4]) + b_ref[:, w4:2 * w4]
        q_ref[rows, :] = (qp * _sigmoid(qp)).astype(BF16)
        u = (_dot(hb, w_ref[:, 0:w4]) + b_ref[:, 0:w4]).astype(BF16)
        for gi in range(N_GROUPS):
            z = _dot(u[:, gi * GROUP_DIM:(gi + 1) * GROUP_DIM], dft)
            xc_ref[gi, rows, :] = z[:, :GROUP_DIM].astype(BF16)
            xs_ref[gi, rows, :] = z[:, GROUP_DIM:].astype(BF16)
        v_ref[rows, :] = (_dot(hb, w_ref[:, 2 * w4:3 * w4]) + b_ref[:, 2 * w4:3 * w4]).astype(BF16)


def _inproj(x, er, ec, lng, lnb, mod, w, b, dft, l0, l1, tm):
    t, d = x.shape
    w4 = N_GROUPS * GROUP_DIM
    nrow = tm // GRID_W
    row = lambda i: (i, 0)
    return pl.pallas_call(
        functools.partial(_inproj_kernel, tm=tm),
        grid=(t // tm,),
        in_specs=[pl.BlockSpec((tm, d), row),
                  pl.BlockSpec((nrow, d // 2), row),
                  _const_spec(ec.shape), _const_spec(lng.shape), _const_spec(lnb.shape),
                  _const_spec(mod.shape), _const_cols(w, 5 * w4, 0), _const_cols(b, 5 * w4, 0),
                  _const_spec(dft.shape), _const_spec(l0.shape), _const_spec(l1.shape)],
        out_specs=[pl.BlockSpec((N_GROUPS, tm, GROUP_DIM), lambda i: (0, i, 0))] * 2
        + [pl.BlockSpec((tm, w4), row)] * 2 + [pl.BlockSpec((tm, 2 * w4), row)],
        out_shape=[jax.ShapeDtypeStruct((N_GROUPS, t, GROUP_DIM), BF16)] * 2
        + [jax.ShapeDtypeStruct((t, w4), BF16)] * 2 + [jax.ShapeDtypeStruct((t, 2 * w4), F32)],
        compiler_params=_cparams(),
        name="inproj",
    )(x, er, ec, lng, lnb, mod, w, b, dft, l0, l1)


DFT_BATCH = 16
F32_SUBLANES = 8


def _dft1_kernel(m_ref, xc_ref, xs_ref, a_ref, sx_ref, sa_ref):
    r, nb, w = xc_ref.shape
    hs = F32_SUBLANES
    for p, x_ref in enumerate((xc_ref, xs_ref)):
        x = x_ref[...].astype(F32)
        for h in range(nb // hs):
            sx_ref[p, h] = x[:, h * hs:(h + 1) * hs, :].reshape(r * hs, w)
    cols = []
    for j in range(nb):
        cols.append(jnp.concatenate(
            [sx_ref[p, j // hs, pl.ds(j % hs, r, stride=hs), :] for p in range(2)], axis=0).astype(BF16))
    a = _dot(m_ref[...].astype(BF16), jnp.concatenate(cols, axis=1))
    for j in range(nb):
        sa_ref[j // hs, pl.ds(j % hs, 2 * r, stride=hs), :] = a[:, j * w:(j + 1) * w]
    a_ref[...] = jnp.concatenate([sa_ref[h].reshape(2, r, hs, w) for h in range(nb // hs)],
                                 axis=2).astype(BF16)


def _dft1(mat, xc, xs):
    ng, t, w = xc.shape
    r = RADIX
    nb = 2 * DFT_BATCH
    nh = nb // F32_SUBLANES
    blk = pl.BlockSpec((None, r, nb, w), lambda g, o: (g, 0, o, 0))
    return pl.pallas_call(
        _dft1_kernel,
        grid=(ng, r // nb),
        in_specs=[_const_spec(mat.shape), blk, blk],
        out_specs=pl.BlockSpec((2, r, nb, w), lambda g, o: (0, 0, o, g)),
        out_shape=jax.ShapeDtypeStruct((2, r, r, ng * w), BF16),
        scratch_shapes=[pltpu.VMEM((2, nh, r * F32_SUBLANES, w), F32),
                        pltpu.VMEM((nh, 2 * r * F32_SUBLANES, w), F32)],
        compiler_params=_cparams(2),
        name="dft1",
    )(mat, xc.reshape(ng, r, r, w), xs.reshape(ng, r, r, w))


def _dft2_kernel(cs_ref, tw_ref, a_ref, o_ref, s_ref):
    nb, r, w = a_ref.shape[1], a_ref.shape[2], a_ref.shape[3]
    hs = F32_SUBLANES
    cos_a, sin_a = cs_ref[0], cs_ref[1]
    for j in range(nb):
        cos_b, sin_b = tw_ref[j:j + 1, 0:r], tw_ref[j:j + 1, r:2 * r]
        gmat = jnp.concatenate([cos_a * cos_b - sin_a * sin_b, sin_a * cos_b + cos_a * sin_b], axis=1)
        a = jnp.concatenate([a_ref[0, j], a_ref[1, j]], axis=0)
        y = _dot(gmat.astype(BF16), a)
        for gi in range(N_GROUPS):
            s_ref[gi, j // hs, pl.ds(j % hs, r, stride=hs), :] = y[:, gi * GROUP_DIM:(gi + 1) * GROUP_DIM]
    for gi in range(N_GROUPS):
        o_ref[gi] = jnp.concatenate([s_ref[gi, h].reshape(r, hs, GROUP_DIM) for h in range(nb // hs)],
                                    axis=1).astype(BF16)


def _dft2(base, twiddle, a4):
    _, r, _, w = a4.shape
    nb = DFT_BATCH
    out = pl.pallas_call(
        _dft2_kernel,
        grid=(r // nb,),
        in_specs=[_const_spec(base.shape),
                  pl.BlockSpec((nb, 2 * r), lambda k: (k, 0)),
                  pl.BlockSpec((2, nb, r, w), lambda k: (0, k, 0, 0))],
        out_specs=pl.BlockSpec((N_GROUPS, r, None, nb, GROUP_DIM), lambda k: (0, 0, k, 0, 0)),
        out_shape=jax.ShapeDtypeStruct((N_GROUPS, r, r // nb, nb, GROUP_DIM), BF16),
        scratch_shapes=[pltpu.VMEM((N_GROUPS, nb // F32_SUBLANES, r * F32_SUBLANES, GROUP_DIM), F32)],
        compiler_params=_cparams(),
        name="dft2",
    )(base, twiddle, a4)
    return out.reshape(N_GROUPS, r * r, GROUP_DIM)


def _hgrn_tables(L, reverse):
    nlev = int(np.log2(L))
    idx = np.arange(L)
    t = idx[:, None]
    i = idx[None, :]
    blocks = [(i >= t) if reverse else (i <= t)]
    for j in range(nlev):
        h = L >> (j + 1)
        mid = (t // (2 * h)) * (2 * h) + h
        upper = t >= mid
        if reverse:
            blk = np.where(upper, (i >= mid) & (i < t), (i >= t) & (i < mid))
        else:
            blk = np.where(upper, (i >= mid) & (i <= t), (i > t) & (i < mid))
        blocks.append(blk)
    blocks.append((i < t) if reverse else (i > t))
    return np.concatenate(blocks, axis=0).astype(np.float32), nlev


def _as_column(row):
    n = row.shape[1]
    return jnp.broadcast_to(row, (n, n)).T


def _split_hi_lo(g):
    hi = g.astype(BF16)
    return hi, (g - hi.astype(F32)).astype(BF16)


def _hgrn_chunk_exact(q_ref, v_ref, g_ref, o_ref, st_ref, rows, mall, pair_masks, query_rows,
                      L, nlev, reverse):
    last = 0 if reverse else L - 1
    g = g_ref[rows, :]
    g_hi, g_lo = _split_hi_lo(g)
    ex = jnp.exp(_dot(mall, g_hi) + _dot(mall, g_lo))
    q = q_ref[rows, :].astype(F32)
    v = v_ref[rows, :]
    k = 1.0 - jnp.exp(g)
    e_cum = ex[0:L]
    qe = (q * e_cum).astype(BF16)
    ke = (k * ex[(nlev + 1) * L:(nlev + 2) * L]).astype(BF16)
    zs = [(jnp.where(query_rows[j], q, k) * ex[(j + 1) * L:(j + 2) * L]).astype(BF16)
          for j in range(nlev)]
    qk = q * k
    e_last = e_cum[last:last + 1, :]
    for hd in range(N_HEADS):
        sl = slice(hd * HEAD_DIM, (hd + 1) * HEAD_DIM)
        sc = jnp.zeros((L, L), F32)
        for j in range(nlev):
            zj = zs[j][:, sl]
            sc = jnp.where(pair_masks[j], _dot_nt(zj, zj), sc)
        st = st_ref[hd]
        vh = v[:, sl]
        o = _dot(sc.astype(BF16), vh) + _dot(qe[:, sl], st.astype(BF16))
        o = o + jnp.sum(qk[:, sl], axis=-1, keepdims=True) * vh.astype(F32)
        o_ref[rows, sl] = o
        st_ref[hd] = st * _as_column(e_last[:, sl]) + _dot_tn(ke[:, sl], vh)


def _hgrn_block_fast(q_ref, v_ref, g_ref, trib_ref, o_ref, st_ref, r0, nchunk, L, reverse):
    half = L // 2
    tb = nchunk * L
    blk = slice(r0, r0 + tb)
    chunks = []
    for c in range(nchunk):
        base = c * L
        if reverse:
            chunks.append((slice(base, base + L), slice(base + half, base + L),
                           slice(base, base + half), base + half, base))
        else:
            chunks.append((slice(base, base + L), slice(base, base + half),
                           slice(base + half, base + L), base + half - 1, base + L - 1))
    g = g_ref[blk, :]
    g_hi, g_lo = _split_hi_lo(g)
    b = jnp.concatenate([_dot(trib_ref[...], jnp.concatenate([g_hi[rows], g_lo[rows]], axis=0))
                         for (rows, _, _, _, _) in chunks], axis=0)
    q = q_ref[blk, :].astype(F32)
    v = v_ref[blk, :]
    k = 1.0 - jnp.exp(g)
    e_b = jnp.exp(b)
    qe = (q * e_b).astype(BF16)
    c2 = [b[sec] - b[edge:edge + 1, :] for (_, _, sec, edge, _) in chunks]
    q2 = jnp.concatenate([q[ch[2]] * jnp.exp(c2[c]) for c, ch in enumerate(chunks)], axis=0)
    own = []
    for c, (_, fst, _, _, _) in enumerate(chunks):
        own += [c2[c], b[fst]] if reverse else [b[fst], c2[c]]
    kh = (k * jnp.exp(-jnp.concatenate(own, axis=0))).astype(BF16)
    tail = jnp.concatenate([b[last:last + 1, :] - b[rows] for (rows, _, _, _, last) in chunks], axis=0)
    ke = (k * jnp.exp(tail)).astype(BF16)
    lhs = jnp.concatenate([qe, q2.astype(BF16)], axis=0)

    row = lax.broadcasted_iota(jnp.int32, (tb, tb), 0)
    col = lax.broadcasted_iota(jnp.int32, (tb, tb), 1)
    valid = (row // L == col // L) & ((col >= row) if reverse else (col <= row))
    if reverse:
        use_near = (row % L < half) & (col % L < half)
    else:
        use_near = (row % L >= half) & (col % L >= half)

    own_chunk = (lax.broadcasted_iota(jnp.int32, (tb, nchunk * HEAD_DIM), 0) // L
                 == lax.broadcasted_iota(jnp.int32, (tb, nchunk * HEAD_DIM), 1) // HEAD_DIM)

    for hd in range(N_HEADS):
        sl = slice(hd * HEAD_DIM, (hd + 1) * HEAD_DIM)
        s_all = _dot_nt(lhs[:, sl], kh[:, sl])
        far = s_all[0:tb]
        pieces = []
        for c, (_, fst, _, _, _) in enumerate(chunks):
            near_c = s_all[tb + c * half:tb + (c + 1) * half]
            pieces += [near_c, far[fst]] if reverse else [far[fst], near_c]
        near = jnp.concatenate(pieces, axis=0)
        sc = jnp.where(valid, jnp.where(use_near, near, far), 0.0).astype(BF16)
        vh = v[:, sl]
        o_intra = _dot(sc, vh)
        v_bd = jnp.where(own_chunk, jnp.concatenate([vh] * nchunk, axis=1), jnp.zeros((), BF16))
        upd_all = _dot_tn(ke[:, sl], v_bd)
        upd = [upd_all[:, c * HEAD_DIM:(c + 1) * HEAD_DIM] for c in range(nchunk)]
        st = st_ref[hd]
        o_inter = [None] * nchunk
        for c in (range(nchunk - 1, -1, -1) if reverse else range(nchunk)):
            rows, _, _, _, last = chunks[c]
            o_inter[c] = _dot(qe[rows, sl], st.astype(BF16))
            st = st * _as_column(e_b[last:last + 1, sl]) + upd[c]
        st_ref[hd] = st
        o_ref[blk, sl] = o_intra + jnp.concatenate(o_inter, axis=0)


def _hgrn_block_exact(q_ref, v_ref, g_ref, mall_ref, o_ref, st_ref, nchunk, L, nlev, reverse):
    row = lax.broadcasted_iota(jnp.int32, (L, L), 0)
    col = lax.broadcasted_iota(jnp.int32, (L, L), 1)
    rowc = lax.broadcasted_iota(jnp.int32, (L, N_HEADS * HEAD_DIM), 0)
    pair_masks, query_rows = [], []
    for j in range(nlev):
        h = L >> (j + 1)
        same = (row // (2 * h)) == (col // (2 * h))
        row_up = (row // h) % 2 == 1
        col_up = (col // h) % 2 == 1
        if reverse:
            pair_masks.append(same & jnp.logical_not(row_up) & col_up)
            query_rows.append((rowc // h) % 2 == 0)
        else:
            pair_masks.append(same & row_up & jnp.logical_not(col_up))
            query_rows.append((rowc // h) % 2 == 1)
    mall = mall_ref[...]
    for c in (range(nchunk - 1, -1, -1) if reverse else range(nchunk)):
        _hgrn_chunk_exact(q_ref, v_ref, g_ref, o_ref, st_ref, slice(c * L, (c + 1) * L),
                          mall, pair_masks, query_rows, L, nlev, reverse)


def _min_leaf_log_decay(g_ref, leaf):
    g = g_ref[...]
    return jnp.min(jnp.sum(g.reshape(g.shape[0] // leaf, leaf, g.shape[1]), axis=1))


def _hgrn_kernel(qf_ref, vf_ref, gf_ref, qb_ref, vb_ref, gb_ref, mallf_ref, mallb_ref,
                 tribf_ref, tribb_ref, s0f_ref, s0b_ref, of_ref, ob_ref, sff_ref, sfb_ref,
                 stf_ref, stb_ref, *, nsub, nchunk, L, nlev):
    i = pl.program_id(0)

    @pl.when(i == 0)
    def _():
        stf_ref[...] = s0f_ref[...]
        stb_ref[...] = s0b_ref[...]

    safe = jnp.minimum(_min_leaf_log_decay(gf_ref, L // 2),
                       _min_leaf_log_decay(gb_ref, L // 2)) >= -SAFE_LOG_DECAY

    @pl.when(safe)
    def _():
        for j in range(nsub):
            _hgrn_block_fast(qf_ref, vf_ref, gf_ref, tribf_ref, of_ref, stf_ref,
                             j * nchunk * L, nchunk, L, False)
            _hgrn_block_fast(qb_ref, vb_ref, gb_ref, tribb_ref, ob_ref, stb_ref,
                             (nsub - 1 - j) * nchunk * L, nchunk, L, True)

    @pl.when(jnp.logical_not(safe))
    def _():
        nc = nsub * nchunk
        _hgrn_block_exact(qf_ref, vf_ref, gf_ref, mallf_ref, of_ref, stf_ref, nc, L, nlev, False)
        _hgrn_block_exact(qb_ref, vb_ref, gb_ref, mallb_ref, ob_ref, stb_ref, nc, L, nlev, True)

    @pl.when(i == pl.num_programs(0) - 1)
    def _():
        sff_ref[...] = stf_ref[...]
        sfb_ref[...] = stb_ref[...]


def _hgrn_scan(q, v, g, s0_f, s0_b, tb):
    t, w = q.shape
    L = HGRN_CHUNK
    nblk = t // tb
    sb = min(tb, HGRN_SUB_BLOCK)
    consts = []
    for reverse in (False, True):
        mall_np, nlev = _hgrn_tables(L, reverse)
        consts.append((jnp.asarray(mall_np, dtype=BF16),
                       jnp.asarray(np.concatenate([mall_np[0:L]] * 2, axis=1), dtype=BF16)))
    (mall_f, trib_f), (mall_b, trib_b) = consts
    fwd = lambda i: (i, 0)
    bwd = lambda i: (nblk - 1 - i, 0)
    bwd_g = lambda i: (nblk - 1 - i, 1)
    blk = lambda m: pl.BlockSpec((tb, w), m)
    return pl.pallas_call(
        functools.partial(_hgrn_kernel, nsub=tb // sb, nchunk=sb // L, L=L, nlev=nlev),
        grid=(nblk,),
        in_specs=[blk(fwd), blk(fwd), blk(fwd), blk(bwd), blk(bwd), blk(bwd_g),
                  _const_spec(mall_f.shape), _const_spec(mall_b.shape),
                  _const_spec(trib_f.shape), _const_spec(trib_b.shape),
                  _const_spec(s0_f.shape), _const_spec(s0_b.shape)],
        out_specs=[blk(fwd), blk(bwd)] + [pl.BlockSpec(s0_f.shape, lambda i: (0, 0, 0))] * 2,
        out_shape=[jax.ShapeDtypeStruct((t, w), F32)] * 2 + [jax.ShapeDtypeStruct(s0_f.shape, F32)] * 2,
        scratch_shapes=[pltpu.VMEM(s0_f.shape, F32)] * 2,
        compiler_params=_cparams(),
        name="hgrn",
    )(q, v, g, q, v, g, mall_f, mall_b, trib_f, trib_b, s0_f, s0_b)


def _merge_kernel(x_ref, er_ref, ec_ref, lng_ref, lnb_ref, mod_ref, wg_ref, bg_ref, of_ref, ob_ref,
                  ng_ref, four_ref, wfp_ref, whp_ref, wo_ref, bo_ref, pg_ref, pb_ref, o_ref, *, tm):
    w4 = N_HEADS * HEAD_DIM
    d = x_ref.shape[1]
    sub = min(tm, SUB_ROWS)
    for r0 in range(0, tm, sub):
        rows = slice(r0, r0 + sub)
        xl, hl = _ln_in_modulated(x_ref, er_ref, ec_ref, lng_ref, lnb_ref, mod_ref, r0, sub)
        hb = hl.astype(BF16)
        og = _dot(hb, wg_ref[:, 0:w4]) + bg_ref[:, 0:w4]
        o = of_ref[rows, :] + ob_ref[rows, :]
        parts = []
        for hd in range(N_HEADS):
            oh = o[:, hd * HEAD_DIM:(hd + 1) * HEAD_DIM]
            ms = jnp.mean(oh * oh, axis=-1, keepdims=True)
            parts.append(oh * lax.rsqrt(ms + RMS_EPS))
        on = jnp.concatenate(parts, axis=1) * ng_ref[...]
        oh = (on * (og * _sigmoid(og))).astype(BF16)
        g_four = _sigmoid(_dot(hb, wg_ref[:, w4:w4 + d]) + bg_ref[:, w4:w4 + d])
        four = jnp.concatenate([four_ref[gi, rows, :] for gi in range(N_GROUPS)], axis=1)
        y = g_four * _dot(four.astype(BF16), wfp_ref[...])
        g_hgrn = _sigmoid(_dot(hb, wg_ref[:, w4 + d:w4 + 2 * d]) + bg_ref[:, w4 + d:w4 + 2 * d])
        y = y + g_hgrn * _dot(oh, whp_ref[...])
        mix = _dot(y.astype(BF16), wo_ref[...]) + bo_ref[...]
        o_ref[rows, :] = _layernorm(ALPHA * xl + mod_ref[2:3, :] * mix, pg_ref[...], pb_ref[...])


def _merge(x, er, ec, lng, lnb, mod, wg, bg, o_f, o_b, ng, four, wfp, whp, wo, bo, pg, pb, tm):
    t, d = x.shape
    w4 = N_HEADS * HEAD_DIM
    nrow = tm // GRID_W
    row = lambda i: (i, 0)
    consts = [ec, lng, lnb, mod]
    consts2 = [wfp, whp, wo, bo, pg, pb]
    return pl.pallas_call(
        functools.partial(_merge_kernel, tm=tm),
        grid=(t // tm,),
        in_specs=[pl.BlockSpec((tm, d), row), pl.BlockSpec((nrow, d // 2), row)]
        + [_const_spec(a.shape) for a in consts]
        + [_const_cols(wg, w4 + 2 * d, 1), _const_cols(bg, w4 + 2 * d, 1)]
        + [pl.BlockSpec((tm, w4), row), pl.BlockSpec((tm, w4), row), _const_spec(ng.shape),
           pl.BlockSpec((N_GROUPS, tm, GROUP_DIM), lambda i: (0, i, 0))]
        + [_const_spec(a.shape) for a in consts2],
        out_specs=pl.BlockSpec((tm, d), row),
        out_shape=jax.ShapeDtypeStruct((t, d), F32),
        compiler_params=_cparams(),
        name="merge",
    )(x, er, *consts, wg, bg, o_f, o_b, ng, four, *consts2)


MLP_WEIGHT_STEPS = 8


def _mlp_kernel(x_ref, mod_ref, w1_ref, b1_ref, w2_ref, b2_ref, pg_ref, pb_ref, o_ref,
                w1s_ref, w2s_ref, *, nsplit):
    i = pl.program_id(0)
    dff = w1s_ref.shape[1]
    cw = dff // nsplit
    tm = x_ref.shape[0]

    @pl.when(i < MLP_WEIGHT_STEPS)
    def _():
        r1, r2 = w1_ref.shape[0], w2_ref.shape[0]
        w1s_ref[pl.ds(pl.multiple_of(i * r1, r1), r1), :] = w1_ref[...].astype(BF16)
        w2s_ref[pl.ds(pl.multiple_of(i * r2, r2), r2), :] = w2_ref[...].astype(BF16)

    @pl.when(i >= MLP_WEIGHT_STEPS)
    def _():
        sub = min(tm, SUB_ROWS)
        for r0 in range(0, tm, sub):
            rows = slice(r0, r0 + sub)
            x1 = x_ref[rows, :]
            hb = (x1 * (1.0 + mod_ref[4:5, :]) + mod_ref[3:4, :]).astype(BF16)
            acc = jnp.zeros(x1.shape, F32)
            for c in range(nsplit):
                cs = slice(c * cw, (c + 1) * cw)
                a = jnp.maximum(_dot(hb, w1s_ref[:, cs]) + b1_ref[:, cs], 0.0)
                acc = acc + _dot((a * a).astype(BF16), w2s_ref[cs, :])
            m = acc + b2_ref[...]
            o_ref[rows, :] = _layernorm(ALPHA * x1 + mod_ref[5:6, :] * m, pg_ref[...], pb_ref[...])


def _mlp(x1, mod, w1, b1, w2, b2, pg, pb, tm):
    t, d = x1.shape
    dff = w1.shape[1]
    nw = MLP_WEIGHT_STEPS
    row = lambda i: (jnp.maximum(i - nw, 0), 0)
    wrow = lambda i: (jnp.minimum(i, nw - 1), 0)
    cs = _const_spec
    return pl.pallas_call(
        functools.partial(_mlp_kernel, nsplit=4),
        grid=(nw + t // tm,),
        in_specs=[pl.BlockSpec((tm, d), row), cs(mod.shape),
                  pl.BlockSpec((d // nw, dff), wrow), cs(b1.shape),
                  pl.BlockSpec((dff // nw, d), wrow), cs(b2.shape), cs(pg.shape), cs(pb.shape)],
        out_specs=pl.BlockSpec((tm, d), row),
        out_shape=jax.ShapeDtypeStruct((t, d), F32),
        scratch_shapes=[pltpu.VMEM((d, dff), BF16), pltpu.VMEM((dff, d), BF16)],
        compiler_params=_cparams(),
        name="mlp",
    )(x1, mod, w1, b1, w2, b2, pg, pb)


def _pos_tables(rows, cols, dim):
    quarter = dim // 4
    omega = 1.0 / (POS_BASE ** (np.arange(quarter, dtype=np.float64) / quarter))
    r = np.arange(rows, dtype=np.float64)[:, None] * omega
    cc = np.arange(cols, dtype=np.float64)[:, None] * omega
    er = np.concatenate([np.sin(r), np.cos(r)], axis=-1)
    ec = np.concatenate([np.sin(cc), np.cos(cc)], axis=-1)
    return jnp.asarray(er.astype(np.float32)), jnp.asarray(ec.astype(np.float32))


def _dft_constants(t):
    n = RADIX
    kn = np.outer(np.arange(n), np.arange(n)).astype(np.float64)
    c = np.cos(2.0 * np.pi * kn / n)
    s = np.sin(2.0 * np.pi * kn / n)
    chan = np.concatenate([c, s], axis=1)
    stage1 = np.block([[c, -s], [-s, -c]])
    scale = 1.0 / np.sqrt(float(t) * GROUP_DIM)
    base2 = np.stack([c, s]) * scale
    beta = 2.0 * np.pi * kn / t
    twiddle = np.concatenate([np.cos(beta), np.sin(beta)], axis=1)
    as_f32 = lambda a: jnp.asarray(a.astype(np.float32))
    return as_f32(chan), as_f32(stage1), as_f32(base2), as_f32(twiddle)


def kernel(x, c, ctx, c_ctx, ln_in_g, ln_in_b, w_ada, b_ada, w_in, b_in, hgrn_lb_logits, hgrn_norm_g,
           w_four_proj, w_hgrn_proj, w_out, b_out, w_mlp1, b_mlp1, w_mlp2, b_mlp2, ln_post_g, ln_post_b):
    B, T, D = x.shape
    assert B == 1 and T == RADIX * RADIX and T % GRID_W == 0
    TC = ctx.shape[1]
    w4 = N_HEADS * HEAD_DIM
    row2 = lambda a: a.reshape(1, -1)

    mod_l, mod_c = _mod_vectors(c[0], c_ctx, w_ada[0], row2(b_ada[0]))
    mod_l = mod_l.reshape(6, D)
    mod_c = mod_c.reshape(6, D)

    er, ec = _pos_tables(T // GRID_W, GRID_W, D)
    lng, lnb = row2(ln_in_g), row2(ln_in_b)
    w_in_b = w_in[0].astype(BF16)
    b_in2 = row2(b_in[0])
    assert w_in_b.shape[1] == 2 * 5 * w4
    w_a = w_g = w_in_b
    b_a = b_g = b_in2
    l0 = hgrn_lb_logits[:, 0, :].reshape(1, 2 * w4)
    l1 = hgrn_lb_logits[:, 1, :].reshape(1, 2 * w4)
    dft_chan, dft_s1, dft_base2, dft_tw = _dft_constants(T)

    zc = jnp.zeros((TC // GRID_W, D // 2), F32)
    _, _, qc, vc, gc = _inproj(ctx[0], zc, jnp.zeros_like(ec), lng, lnb, mod_c, w_a, b_a, dft_chan,
                               l0, l1, tm=TC)
    s_zero = jnp.zeros((N_HEADS, HEAD_DIM, HEAD_DIM), F32)
    _, _, s_f, s_b = _hgrn_scan(qc, vc, gc, s_zero, s_zero, tb=TC)

    xc, xs, q, v, g = _inproj(x[0], er, ec, lng, lnb, mod_l, w_a, b_a, dft_chan, l0, l1, tm=1024)
    four = _dft2(dft_base2, dft_tw, _dft1(dft_s1, xc, xs))
    o_f, o_b, _, _ = _hgrn_scan(q, v, g, s_f, s_b, tb=512)

    x1 = _merge(x[0], er, ec, lng, lnb, mod_l, w_g, b_g, o_f, o_b, row2(hgrn_norm_g[0]), four,
                w_four_proj[0].astype(BF16), w_hgrn_proj[0].astype(BF16), w_out[0].astype(BF16),
                row2(b_out[0]), row2(ln_post_g[0, 0]), row2(ln_post_b[0, 0]), tm=1024)
    out = _mlp(x1, mod_l, w_mlp1[0], row2(b_mlp1[0]), w_mlp2[0],
               row2(b_mlp2[0]), row2(ln_post_g[0, 1]), row2(ln_post_b[0, 1]), tm=1024)
    return out[None]
```

```python
import functools

import numpy as np
import jax
import jax.numpy as jnp
from jax import lax
from jax.experimental import pallas as pl
from jax.experimental.pallas import tpu as pltpu

F32 = jnp.float32
BF16 = jnp.bfloat16

GRID_W = 64
N_GROUPS = 4
GROUP_DIM = 128
N_HEADS = 4
HEAD_DIM = 128
POS_BASE = 10000.0
LN_EPS = 1e-5
RMS_EPS = 1e-6
DEPTH = 1
ALPHA = (2.0 * DEPTH) ** 0.25

RADIX = 128
HGRN_CHUNK = 64
HGRN_SUB_BLOCK = 256
SUB_ROWS = 256
SAFE_LOG_DECAY = 80.0
VMEM_LIMIT_BYTES = 56 * 1024 * 1024


def _cparams(n_axes=1):
    return pltpu.CompilerParams(dimension_semantics=("arbitrary",) * n_axes,
                                vmem_limit_bytes=VMEM_LIMIT_BYTES)


def _const_spec(shape):
    nd = len(shape)
    return pl.BlockSpec(shape, lambda *_: (0,) * nd, pipeline_mode=pl.Buffered(1))


def _const_cols(arr, width, j):
    return pl.BlockSpec((arr.shape[0], width), lambda *_: (0, j), pipeline_mode=pl.Buffered(1))


def _sigmoid(x):
    return 1.0 / (1.0 + jnp.exp(-x))


def _layernorm(x, g, b):
    mu = jnp.mean(x, axis=-1, keepdims=True)
    xc = x - mu
    var = jnp.mean(xc * xc, axis=-1, keepdims=True)
    return xc * lax.rsqrt(var + LN_EPS) * g + b


def _dot(a, b):
    return jnp.dot(a, b, preferred_element_type=F32)


def _dot_nt(a, b):
    return lax.dot_general(a, b, (((1,), (1,)), ((), ())), preferred_element_type=F32)


def _dot_tn(a, b):
    return lax.dot_general(a, b, (((0,), (0,)), ((), ())), preferred_element_type=F32)


def _mod_kernel(cl_ref, cx_ref, w_ref, b_ref, ol_ref, ox_ref):
    w = w_ref[...]
    for c_ref, o_ref in ((cl_ref, ol_ref), (cx_ref, ox_ref)):
        cs = c_ref[...]
        s = cs * _sigmoid(cs)
        o_ref[...] = jnp.sum(s * w, axis=0, keepdims=True) + b_ref[...]


def _mod_vectors(c_lat, c_ctx, w_ada, b_ada):
    d, n = w_ada.shape
    tn = 1536
    col = lambda j: (0, j)
    return pl.pallas_call(
        _mod_kernel,
        grid=(n // tn,),
        in_specs=[_const_spec((d, 1)), _const_spec((d, 1)),
                  pl.BlockSpec((d, tn), col), pl.BlockSpec((1, tn), col)],
        out_specs=[pl.BlockSpec((1, tn), col)] * 2,
        out_shape=[jax.ShapeDtypeStruct((1, n), F32)] * 2,
        compiler_params=_cparams(),
        name="mod",
    )(c_lat.reshape(d, 1), c_ctx.reshape(d, 1), w_ada, b_ada)


def _ln_in_modulated(x_ref, er_ref, ec_ref, lng_ref, lnb_ref, mod_ref, r0, nr):
    x = x_ref[r0:r0 + nr, :]
    half = x.shape[1] // 2
    nrow = nr // GRID_W
    e0 = r0 // GRID_W
    left = jnp.concatenate(
        [jnp.broadcast_to(er_ref[e0 + r:e0 + r + 1, :], (GRID_W, half)) for r in range(nrow)], axis=0)
    right = jnp.concatenate([ec_ref[...]] * nrow, axis=0)
    xp = jnp.concatenate([x[:, :half] + left, x[:, half:] + right], axis=1)
    xl = _layernorm(xp, lng_ref[...], lnb_ref[...])
    hl = xl * (1.0 + mod_ref[1:2, :]) + mod_ref[0:1, :]
    return xl, hl


def _inproj_kernel(x_ref, er_ref, ec_ref, lng_ref, lnb_ref, mod_ref, w_ref, b_ref, dft_ref,
                   l0_ref, l1_ref, xc_ref, xs_ref, q_ref, v_ref, g_ref, *, tm):
    w4 = N_GROUPS * GROUP_DIM
    dft = dft_ref[...].astype(BF16)
    l0 = l0_ref[...]
    l1 = l1_ref[...]
    m = jnp.maximum(l0, l1)
    e0 = jnp.exp(l0 - m)
    lb = e0 / (e0 + jnp.exp(l1 - m))
    sub = min(tm, SUB_ROWS)
    for r0 in range(0, tm, sub):
        rows = slice(r0, r0 + sub)
        _, hl = _ln_in_modulated(x_ref, er_ref, ec_ref, lng_ref, lnb_ref, mod_ref, r0, sub)
        hb = hl.astype(BF16)
        fp = _dot(hb, w_ref[:, 3 * w4:5 * w4]) + b_ref[:, 3 * w4:5 * w4]
        g_ref[rows, :] = jnp.log(lb + (1.0 - lb) * _sigmoid(fp))
        qp = _dot(hb, w_ref[:, w4:2 * w4]) + b_ref[:, w4:2 * w4]
        q_ref[rows, :] = (qp * _sigmoid(qp)).astype(BF16)
        u = (_dot(hb, w_ref[:, 0:w4]) + b_ref[:, 0:w4]).astype(BF16)
        for gi in range(N_GROUPS):
            z = _dot(u[:, gi * GROUP_DIM:(gi + 1) * GROUP_DIM], dft)
            xc_ref[gi, rows, :] = z[:, :GROUP_DIM].astype(BF16)
            xs_ref[gi, rows, :] = z[:, GROUP_DIM:].astype(BF16)
        v_ref[rows, :] = (_dot(hb, w_ref[:, 2 * w4:3 * w4]) + b_ref[:, 2 * w4:3 * w4]).astype(BF16)


def _inproj(x, er, ec, lng, lnb, mod, w, b, dft, l0, l1, tm):
    t, d = x.shape
    w4 = N_GROUPS * GROUP_DIM
    nrow = tm // GRID_W
    row = lambda i: (i, 0)
    return pl.pallas_call(
        functools.partial(_inproj_kernel, tm=tm),
        grid=(t // tm,),
        in_specs=[pl.BlockSpec((tm, d), row),
                  pl.BlockSpec((nrow, d // 2), row),
                  _const_spec(ec.shape), _const_spec(lng.shape), _const_spec(lnb.shape),
                  _const_spec(mod.shape), _const_cols(w, 5 * w4, 0), _const_cols(b, 5 * w4, 0),
                  _const_spec(dft.shape), _const_spec(l0.shape), _const_spec(l1.shape)],
        out_specs=[pl.BlockSpec((N_GROUPS, tm, GROUP_DIM), lambda i: (0, i, 0))] * 2
        + [pl.BlockSpec((tm, w4), row)] * 2 + [pl.BlockSpec((tm, 2 * w4), row)],
        out_shape=[jax.ShapeDtypeStruct((N_GROUPS, t, GROUP_DIM), BF16)] * 2
        + [jax.ShapeDtypeStruct((t, w4), BF16)] * 2 + [jax.ShapeDtypeStruct((t, 2 * w4), F32)],
        compiler_params=_cparams(),
        name="inproj",
    )(x, er, ec, lng, lnb, mod, w, b, dft, l0, l1)


DFT_BATCH = 16
F32_SUBLANES = 8


def _dft1_kernel(m_ref, xc_ref, xs_ref, a_ref, sx_ref, sa_ref):
    r, nb, w = xc_ref.shape
    hs = F32_SUBLANES
    for p, x_ref in enumerate((xc_ref, xs_ref)):
        x = x_ref[...].astype(F32)
        for h in range(nb // hs):
            sx_ref[p, h] = x[:, h * hs:(h + 1) * hs, :].reshape(r * hs, w)
    cols = []
    for j in range(nb):
        cols.append(jnp.concatenate(
            [sx_ref[p, j // hs, pl.ds(j % hs, r, stride=hs), :] for p in range(2)], axis=0).astype(BF16))
    a = _dot(m_ref[...].astype(BF16), jnp.concatenate(cols, axis=1))
    for j in range(nb):
        sa_ref[j // hs, pl.ds(j % hs, 2 * r, stride=hs), :] = a[:, j * w:(j + 1) * w]
    a_ref[...] = jnp.concatenate([sa_ref[h].reshape(2, r, hs, w) for h in range(nb // hs)],
                                 axis=2).astype(BF16)


def _dft1(mat, xc, xs):
    ng, t, w = xc.shape
    r = RADIX
    nb = 2 * DFT_BATCH
    nh = nb // F32_SUBLANES
    blk = pl.BlockSpec((None, r, nb, w), lambda g, o: (g, 0, o, 0))
    return pl.pallas_call(
        _dft1_kernel,
        grid=(ng, r // nb),
        in_specs=[_const_spec(mat.shape), blk, blk],
        out_specs=pl.BlockSpec((2, r, nb, w), lambda g, o: (0, 0, o, g)),
        out_shape=jax.ShapeDtypeStruct((2, r, r, ng * w), BF16),
        scratch_shapes=[pltpu.VMEM((2, nh, r * F32_SUBLANES, w), F32),
                        pltpu.VMEM((nh, 2 * r * F32_SUBLANES, w), F32)],
        compiler_params=_cparams(2),
        name="dft1",
    )(mat, xc.reshape(ng, r, r, w), xs.reshape(ng, r, r, w))


def _dft2_kernel(cs_ref, tw_ref, a_ref, o_ref, s_ref):
    nb, r, w = a_ref.shape[1], a_ref.shape[2], a_ref.shape[3]
    hs = F32_SUBLANES
    cos_a, sin_a = cs_ref[0], cs_ref[1]
    for j in range(nb):
        cos_b, sin_b = tw_ref[j:j + 1, 0:r], tw_ref[j:j + 1, r:2 * r]
        gmat = jnp.concatenate([cos_a * cos_b - sin_a * sin_b, sin_a * cos_b + cos_a * sin_b], axis=1)
        a = jnp.concatenate([a_ref[0, j], a_ref[1, j]], axis=0)
        y = _dot(gmat.astype(BF16), a)
        for gi in range(N_GROUPS):
            s_ref[gi, j // hs, pl.ds(j % hs, r, stride=hs), :] = y[:, gi * GROUP_DIM:(gi + 1) * GROUP_DIM]
    for gi in range(N_GROUPS):
        o_ref[gi] = jnp.concatenate([s_ref[gi, h].reshape(r, hs, GROUP_DIM) for h in range(nb // hs)],
                                    axis=1).astype(BF16)


def _dft2(base, twiddle, a4):
    _, r, _, w = a4.shape
    nb = DFT_BATCH
    out = pl.pallas_call(
        _dft2_kernel,
        grid=(r // nb,),
        in_specs=[_const_spec(base.shape),
                  pl.BlockSpec((nb, 2 * r), lambda k: (k, 0)),
                  pl.BlockSpec((2, nb, r, w), lambda k: (0, k, 0, 0))],
        out_specs=pl.BlockSpec((N_GROUPS, r, None, nb, GROUP_DIM), lambda k: (0, 0, k, 0, 0)),
        out_shape=jax.ShapeDtypeStruct((N_GROUPS, r, r // nb, nb, GROUP_DIM), BF16),
        scratch_shapes=[pltpu.VMEM((N_GROUPS, nb // F32_SUBLANES, r * F32_SUBLANES, GROUP_DIM), F32)],
        compiler_params=_cparams(),
        name="dft2",
    )(base, twiddle, a4)
    return out.reshape(N_GROUPS, r * r, GROUP_DIM)


def _hgrn_tables(L, reverse):
    nlev = int(np.log2(L))
    idx = np.arange(L)
    t = idx[:, None]
    i = idx[None, :]
    blocks = [(i >= t) if reverse else (i <= t)]
    for j in range(nlev):
        h = L >> (j + 1)
        mid = (t // (2 * h)) * (2 * h) + h
        upper = t >= mid
        if reverse:
            blk = np.where(upper, (i >= mid) & (i < t), (i >= t) & (i < mid))
        else:
            blk = np.where(upper, (i >= mid) & (i <= t), (i > t) & (i < mid))
        blocks.append(blk)
    blocks.append((i < t) if reverse else (i > t))
    return np.concatenate(blocks, axis=0).astype(np.float32), nlev


def _as_column(row):
    n = row.shape[1]
    return jnp.broadcast_to(row, (n, n)).T


def _split_hi_lo(g):
    hi = g.astype(BF16)
    return hi, (g - hi.astype(F32)).astype(BF16)


def _hgrn_chunk_exact(q_ref, v_ref, g_ref, o_ref, st_ref, rows, mall, pair_masks, query_rows,
                      L, nlev, reverse):
    last = 0 if reverse else L - 1
    g = g_ref[rows, :]
    g_hi, g_lo = _split_hi_lo(g)
    ex = jnp.exp(_dot(mall, g_hi) + _dot(mall, g_lo))
    q = q_ref[rows, :].astype(F32)
    v = v_ref[rows, :]
    k = 1.0 - jnp.exp(g)
    e_cum = ex[0:L]
    qe = (q * e_cum).astype(BF16)
    ke = (k * ex[(nlev + 1) * L:(nlev + 2) * L]).astype(BF16)
    zs = [(jnp.where(query_rows[j], q, k) * ex[(j + 1) * L:(j + 2) * L]).astype(BF16)
          for j in range(nlev)]
    qk = q * k
    e_last = e_cum[last:last + 1, :]
    for hd in range(N_HEADS):
        sl = slice(hd * HEAD_DIM, (hd + 1) * HEAD_DIM)
        sc = jnp.zeros((L, L), F32)
        for j in range(nlev):
            zj = zs[j][:, sl]
            sc = jnp.where(pair_masks[j], _dot_nt(zj, zj), sc)
        st = st_ref[hd]
        vh = v[:, sl]
        o = _dot(sc.astype(BF16), vh) + _dot(qe[:, sl], st.astype(BF16))
        o = o + jnp.sum(qk[:, sl], axis=-1, keepdims=True) * vh.astype(F32)
        o_ref[rows, sl] = o
        st_ref[hd] = st * _as_column(e_last[:, sl]) + _dot_tn(ke[:, sl], vh)


def _hgrn_block_fast(q_ref, v_ref, g_ref, trib_ref, o_ref, st_ref, r0, nchunk, L, reverse):
    half = L // 2
    tb = nchunk * L
    blk = slice(r0, r0 + tb)
    chunks = []
    for c in range(nchunk):
        base = c * L
        if reverse:
            chunks.append((slice(base, base + L), slice(base + half, base + L),
                           slice(base, base + half), base + half, base))
        else:
            chunks.append((slice(base, base + L), slice(base, base + half),
                           slice(base + half, base + L), base + half - 1, base + L - 1))
    g = g_ref[blk, :]
    b = _dot(trib_ref[...], g.astype(BF16))
    q = q_ref[blk, :].astype(F32)
    v = v_ref[blk, :]
    k = 1.0 - jnp.exp(g)
    e_b = jnp.exp(b)
    qe = (q * e_b).astype(BF16)
    c2 = [b[sec] - b[edge:edge + 1, :] for (_, _, sec, edge, _) in chunks]
    q2 = jnp.concatenate([q[ch[2]] * jnp.exp(c2[c]) for c, ch in enumerate(chunks)], axis=0)
    own = []
    for c, (_, fst, _, _, _) in enumerate(chunks):
        own += [c2[c], b[fst]] if reverse else [b[fst], c2[c]]
    kh = (k * jnp.exp(-jnp.concatenate(own, axis=0))).astype(BF16)
    tail = jnp.concatenate([b[last:last + 1, :] - b[rows] for (rows, _, _, _, last) in chunks], axis=0)
    ke = (k * jnp.exp(tail)).astype(BF16)
    lhs = jnp.concatenate([qe, q2.astype(BF16)], axis=0)

    row = lax.broadcasted_iota(jnp.int32, (tb, tb), 0)
    col = lax.broadcasted_iota(jnp.int32, (tb, tb), 1)
    valid = (row // L == col // L) & ((col >= row) if reverse else (col <= row))
    if reverse:
        use_near = (row % L < half) & (col % L < half)
    else:
        use_near = (row % L >= half) & (col % L >= half)

    for hd in range(N_HEADS):
        sl = slice(hd * HEAD_DIM, (hd + 1) * HEAD_DIM)
        s_all = _dot_nt(lhs[:, sl], kh[:, sl])
        far = s_all[0:tb]
        pieces = []
        for c, (_, fst, _, _, _) in enumerate(chunks):
            near_c = s_all[tb + c * half:tb + (c + 1) * half]
            pieces += [near_c, far[fst]] if reverse else [far[fst], near_c]
        near = jnp.concatenate(pieces, axis=0)
        sc = jnp.where(valid, jnp.where(use_near, near, far), 0.0).astype(BF16)
        vh = v[:, sl]
        o_intra = _dot(sc, vh)
        upd = [_dot_tn(ke[rows, sl], vh[rows]) for (rows, _, _, _, _) in chunks]
        st = st_ref[hd]
        o_inter = [None] * nchunk
        for c in (range(nchunk - 1, -1, -1) if reverse else range(nchunk)):
            rows, _, _, _, last = chunks[c]
            o_inter[c] = _dot(qe[rows, sl], st.astype(BF16))
            st = st * _as_column(e_b[last:last + 1, sl]) + upd[c]
        st_ref[hd] = st
        o_ref[blk, sl] = o_intra + jnp.concatenate(o_inter, axis=0)


def _hgrn_block_exact(q_ref, v_ref, g_ref, mall_ref, o_ref, st_ref, nchunk, L, nlev, reverse):
    row = lax.broadcasted_iota(jnp.int32, (L, L), 0)
    col = lax.broadcasted_iota(jnp.int32, (L, L), 1)
    rowc = lax.broadcasted_iota(jnp.int32, (L, N_HEADS * HEAD_DIM), 0)
    pair_masks, query_rows = [], []
    for j in range(nlev):
        h = L >> (j + 1)
        same = (row // (2 * h)) == (col // (2 * h))
        row_up = (row // h) % 2 == 1
        col_up = (col // h) % 2 == 1
        if reverse:
            pair_masks.append(same & jnp.logical_not(row_up) & col_up)
            query_rows.append((rowc // h) % 2 == 0)
        else:
            pair_masks.append(same & row_up & jnp.logical_not(col_up))
            query_rows.append((rowc // h) % 2 == 1)
    mall = mall_ref[...]
    for c in (range(nchunk - 1, -1, -1) if reverse else range(nchunk)):
        _hgrn_chunk_exact(q_ref, v_ref, g_ref, o_ref, st_ref, slice(c * L, (c + 1) * L),
                          mall, pair_masks, query_rows, L, nlev, reverse)


def _min_leaf_log_decay(g_ref, leaf):
    g = g_ref[...]
    return jnp.min(jnp.sum(g.reshape(g.shape[0] // leaf, leaf, g.shape[1]), axis=1))


def _hgrn_kernel(qf_ref, vf_ref, gf_ref, qb_ref, vb_ref, gb_ref, mallf_ref, mallb_ref,
                 tribf_ref, tribb_ref, s0f_ref, s0b_ref, of_ref, ob_ref, sff_ref, sfb_ref,
                 stf_ref, stb_ref, *, nsub, nchunk, L, nlev):
    i = pl.program_id(0)

    @pl.when(i == 0)
    def _():
        stf_ref[...] = s0f_ref[...]
        stb_ref[...] = s0b_ref[...]

    safe = jnp.minimum(_min_leaf_log_decay(gf_ref, L // 2),
                       _min_leaf_log_decay(gb_ref, L // 2)) >= -SAFE_LOG_DECAY

    @pl.when(safe)
    def _():
        for j in range(nsub):
            _hgrn_block_fast(qf_ref, vf_ref, gf_ref, tribf_ref, of_ref, stf_ref,
                             j * nchunk * L, nchunk, L, False)
            _hgrn_block_fast(qb_ref, vb_ref, gb_ref, tribb_ref, ob_ref, stb_ref,
                             (nsub - 1 - j) * nchunk * L, nchunk, L, True)

    @pl.when(jnp.logical_not(safe))
    def _():
        nc = nsub * nchunk
        _hgrn_block_exact(qf_ref, vf_ref, gf_ref, mallf_ref, of_ref, stf_ref, nc, L, nlev, False)
        _hgrn_block_exact(qb_ref, vb_ref, gb_ref, mallb_ref, ob_ref, stb_ref, nc, L, nlev, True)

    @pl.when(i == pl.num_programs(0) - 1)
    def _():
        sff_ref[...] = stf_ref[...]
        sfb_ref[...] = stb_ref[...]


def _hgrn_scan(q, v, g, s0_f, s0_b, tb):
    t, w = q.shape
    L = HGRN_CHUNK
    nblk = t // tb
    sb = min(tb, HGRN_SUB_BLOCK)
    consts = []
    for reverse in (False, True):
        mall_np, nlev = _hgrn_tables(L, reverse)
        consts.append((jnp.asarray(mall_np, dtype=BF16),
                       jnp.asarray(np.kron(np.eye(sb // L, dtype=np.float32), mall_np[0:L]), dtype=BF16)))
    (mall_f, trib_f), (mall_b, trib_b) = consts
    fwd = lambda i: (i, 0)
    bwd = lambda i: (nblk - 1 - i, 0)
    bwd_g = lambda i: (nblk - 1 - i, 1)
    blk = lambda m: pl.BlockSpec((tb, w), m)
    return pl.pallas_call(
        functools.partial(_hgrn_kernel, nsub=tb // sb, nchunk=sb // L, L=L, nlev=nlev),
        grid=(nblk,),
        in_specs=[blk(fwd), blk(fwd), blk(fwd), blk(bwd), blk(bwd), blk(bwd_g),
                  _const_spec(mall_f.shape), _const_spec(mall_b.shape),
                  _const_spec(trib_f.shape), _const_spec(trib_b.shape),
                  _const_spec(s0_f.shape), _const_spec(s0_b.shape)],
        out_specs=[blk(fwd), blk(bwd)] + [pl.BlockSpec(s0_f.shape, lambda i: (0, 0, 0))] * 2,
        out_shape=[jax.ShapeDtypeStruct((t, w), F32)] * 2 + [jax.ShapeDtypeStruct(s0_f.shape, F32)] * 2,
        scratch_shapes=[pltpu.VMEM(s0_f.shape, F32)] * 2,
        compiler_params=_cparams(),
        name="hgrn",
    )(q, v, g, q, v, g, mall_f, mall_b, trib_f, trib_b, s0_f, s0_b)


def _merge_kernel(x_ref, er_ref, ec_ref, lng_ref, lnb_ref, mod_ref, wg_ref, bg_ref, of_ref, ob_ref,
                  ng_ref, four_ref, wfp_ref, whp_ref, wo_ref, bo_ref, pg_ref, pb_ref, o_ref, *, tm):
    w4 = N_HEADS * HEAD_DIM
    d = x_ref.shape[1]
    sub = min(tm, SUB_ROWS)
    for r0 in range(0, tm, sub):
        rows = slice(r0, r0 + sub)
        xl, hl = _ln_in_modulated(x_ref, er_ref, ec_ref, lng_ref, lnb_ref, mod_ref, r0, sub)
        hb = hl.astype(BF16)
        og = _dot(hb, wg_ref[:, 0:w4]) + bg_ref[:, 0:w4]
        o = of_ref[rows, :] + ob_ref[rows, :]
        parts = []
        for hd in range(N_HEADS):
            oh = o[:, hd * HEAD_DIM:(hd + 1) * HEAD_DIM]
            ms = jnp.mean(oh * oh, axis=-1, keepdims=True)
            parts.append(oh * lax.rsqrt(ms + RMS_EPS))
        on = jnp.concatenate(parts, axis=1) * ng_ref[...]
        oh = (on * (og * _sigmoid(og))).astype(BF16)
        g_four = _sigmoid(_dot(hb, wg_ref[:, w4:w4 + d]) + bg_ref[:, w4:w4 + d])
        four = jnp.concatenate([four_ref[gi, rows, :] for gi in range(N_GROUPS)], axis=1)
        y = g_four * _dot(four.astype(BF16), wfp_ref[...])
        g_hgrn = _sigmoid(_dot(hb, wg_ref[:, w4 + d:w4 + 2 * d]) + bg_ref[:, w4 + d:w4 + 2 * d])
        y = y + g_hgrn * _dot(oh, whp_ref[...])
        mix = _dot(y.astype(BF16), wo_ref[...]) + bo_ref[...]
        o_ref[rows, :] = _layernorm(ALPHA * xl + mod_ref[2:3, :] * mix, pg_ref[...], pb_ref[...])


def _merge(x, er, ec, lng, lnb, mod, wg, bg, o_f, o_b, ng, four, wfp, whp, wo, bo, pg, pb, tm):
    t, d = x.shape
    w4 = N_HEADS * HEAD_DIM
    nrow = tm // GRID_W
    row = lambda i: (i, 0)
    consts = [ec, lng, lnb, mod]
    consts2 = [wfp, whp, wo, bo, pg, pb]
    return pl.pallas_call(
        functools.partial(_merge_kernel, tm=tm),
        grid=(t // tm,),
        in_specs=[pl.BlockSpec((tm, d), row), pl.BlockSpec((nrow, d // 2), row)]
        + [_const_spec(a.shape) for a in consts]
        + [_const_cols(wg, w4 + 2 * d, 1), _const_cols(bg, w4 + 2 * d, 1)]
        + [pl.BlockSpec((tm, w4), row), pl.BlockSpec((tm, w4), row), _const_spec(ng.shape),
           pl.BlockSpec((N_GROUPS, tm, GROUP_DIM), lambda i: (0, i, 0))]
        + [_const_spec(a.shape) for a in consts2],
        out_specs=pl.BlockSpec((tm, d), row),
        out_shape=jax.ShapeDtypeStruct((t, d), F32),
        compiler_params=_cparams(),
        name="merge",
    )(x, er, *consts, wg, bg, o_f, o_b, ng, four, *consts2)


MLP_WEIGHT_STEPS = 8


def _mlp_kernel(x_ref, mod_ref, w1_ref, b1_ref, w2_ref, b2_ref, pg_ref, pb_ref, o_ref,
                w1s_ref, w2s_ref, *, nsplit):
    i = pl.program_id(0)
    dff = w1s_ref.shape[1]
    cw = dff // nsplit
    tm = x_ref.shape[0]

    @pl.when(i < MLP_WEIGHT_STEPS)
    def _():
        r1, r2 = w1_ref.shape[0], w2_ref.shape[0]
        w1s_ref[pl.ds(pl.multiple_of(i * r1, r1), r1), :] = w1_ref[...].astype(BF16)
        w2s_ref[pl.ds(pl.multiple_of(i * r2, r2), r2), :] = w2_ref[...].astype(BF16)

    @pl.when(i >= MLP_WEIGHT_STEPS)
    def _():
        sub = min(tm, SUB_ROWS)
        for r0 in range(0, tm, sub):
            rows = slice(r0, r0 + sub)
            x1 = x_ref[rows, :]
            hb = (x1 * (1.0 + mod_ref[4:5, :]) + mod_ref[3:4, :]).astype(BF16)
            acc = jnp.zeros(x1.shape, F32)
            for c in range(nsplit):
                cs = slice(c * cw, (c + 1) * cw)
                a = jnp.maximum(_dot(hb, w1s_ref[:, cs]) + b1_ref[:, cs], 0.0)
                acc = acc + _dot((a * a).astype(BF16), w2s_ref[cs, :])
            m = acc + b2_ref[...]
            o_ref[rows, :] = _layernorm(ALPHA * x1 + mod_ref[5:6, :] * m, pg_ref[...], pb_ref[...])


def _mlp(x1, mod, w1, b1, w2, b2, pg, pb, tm):
    t, d = x1.shape
    dff = w1.shape[1]
    nw = MLP_WEIGHT_STEPS
    row = lambda i: (jnp.maximum(i - nw, 0), 0)
    wrow = lambda i: (jnp.minimum(i, nw - 1), 0)
    cs = _const_spec
    return pl.pallas_call(
        functools.partial(_mlp_kernel, nsplit=4),
        grid=(nw + t // tm,),
        in_specs=[pl.BlockSpec((tm, d), row), cs(mod.shape),
                  pl.BlockSpec((d // nw, dff), wrow), cs(b1.shape),
                  pl.BlockSpec((dff // nw, d), wrow), cs(b2.shape), cs(pg.shape), cs(pb.shape)],
        out_specs=pl.BlockSpec((tm, d), row),
        out_shape=jax.ShapeDtypeStruct((t, d), F32),
        scratch_shapes=[pltpu.VMEM((d, dff), BF16), pltpu.VMEM((dff, d), BF16)],
        compiler_params=_cparams(),
        name="mlp",
    )(x1, mod, w1, b1, w2, b2, pg, pb)


def _pos_tables(rows, cols, dim):
    quarter = dim // 4
    omega = 1.0 / (POS_BASE ** (np.arange(quarter, dtype=np.float64) / quarter))
    r = np.arange(rows, dtype=np.float64)[:, None] * omega
    cc = np.arange(cols, dtype=np.float64)[:, None] * omega
    er = np.concatenate([np.sin(r), np.cos(r)], axis=-1)
    ec = np.concatenate([np.sin(cc), np.cos(cc)], axis=-1)
    return jnp.asarray(er.astype(np.float32)), jnp.asarray(ec.astype(np.float32))


def _dft_constants(t):
    n = RADIX
    kn = np.outer(np.arange(n), np.arange(n)).astype(np.float64)
    c = np.cos(2.0 * np.pi * kn / n)
    s = np.sin(2.0 * np.pi * kn / n)
    chan = np.concatenate([c, s], axis=1)
    stage1 = np.block([[c, -s], [-s, -c]])
    scale = 1.0 / np.sqrt(float(t) * GROUP_DIM)
    base2 = np.stack([c, s]) * scale
    beta = 2.0 * np.pi * kn / t
    twiddle = np.concatenate([np.cos(beta), np.sin(beta)], axis=1)
    as_f32 = lambda a: jnp.asarray(a.astype(np.float32))
    return as_f32(chan), as_f32(stage1), as_f32(base2), as_f32(twiddle)


def kernel(x, c, ctx, c_ctx, ln_in_g, ln_in_b, w_ada, b_ada, w_in, b_in, hgrn_lb_logits, hgrn_norm_g,
           w_four_proj, w_hgrn_proj, w_out, b_out, w_mlp1, b_mlp1, w_mlp2, b_mlp2, ln_post_g, ln_post_b):
    B, T, D = x.shape
    assert B == 1 and T == RADIX * RADIX and T % GRID_W == 0
    TC = ctx.shape[1]
    w4 = N_HEADS * HEAD_DIM
    row2 = lambda a: a.reshape(1, -1)

    mod_l, mod_c = _mod_vectors(c[0], c_ctx, w_ada[0], row2(b_ada[0]))
    mod_l = mod_l.reshape(6, D)
    mod_c = mod_c.reshape(6, D)

    er, ec = _pos_tables(T // GRID_W, GRID_W, D)
    lng, lnb = row2(ln_in_g), row2(ln_in_b)
    w_in_b = w_in[0].astype(BF16)
    b_in2 = row2(b_in[0])
    assert w_in_b.shape[1] == 2 * 5 * w4
    w_a = w_g = w_in_b
    b_a = b_g = b_in2
    l0 = hgrn_lb_logits[:, 0, :].reshape(1, 2 * w4)
    l1 = hgrn_lb_logits[:, 1, :].reshape(1, 2 * w4)
    dft_chan, dft_s1, dft_base2, dft_tw = _dft_constants(T)

    zc = jnp.zeros((TC // GRID_W, D // 2), F32)
    _, _, qc, vc, gc = _inproj(ctx[0], zc, jnp.zeros_like(ec), lng, lnb, mod_c, w_a, b_a, dft_chan,
                               l0, l1, tm=TC)
    s_zero = jnp.zeros((N_HEADS, HEAD_DIM, HEAD_DIM), F32)
    _, _, s_f, s_b = _hgrn_scan(qc, vc, gc, s_zero, s_zero, tb=TC)

    xc, xs, q, v, g = _inproj(x[0], er, ec, lng, lnb, mod_l, w_a, b_a, dft_chan, l0, l1, tm=1024)
    four = _dft2(dft_base2, dft_tw, _dft1(dft_s1, xc, xs))
    o_f, o_b, _, _ = _hgrn_scan(q, v, g, s_f, s_b, tb=512)

    x1 = _merge(x[0], er, ec, lng, lnb, mod_l, w_g, b_g, o_f, o_b, row2(hgrn_norm_g[0]), four,
                w_four_proj[0].astype(BF16), w_hgrn_proj[0].astype(BF16), w_out[0].astype(BF16),
                row2(b_out[0]), row2(ln_post_g[0, 0]), row2(ln_post_b[0, 0]), tm=1024)
    out = _mlp(x1, mod_l, w_mlp1[0], row2(b_mlp1[0]), w_mlp2[0],
               row2(b_mlp2[0]), row2(ln_post_g[0, 1]), row2(ln_post_b[0, 1]), tm=1024)
    return out[None]
```

```python
import functools

import numpy as np
import jax
import jax.numpy as jnp
from jax import lax
from jax.experimental import pallas as pl
from jax.experimental.pallas import tpu as pltpu

F32 = jnp.float32
BF16 = jnp.bfloat16

GRID_W = 64
N_GROUPS = 4
GROUP_DIM = 128
N_HEADS = 4
HEAD_DIM = 128
POS_BASE = 10000.0
LN_EPS = 1e-5
RMS_EPS = 1e-6
DEPTH = 1
ALPHA = (2.0 * DEPTH) ** 0.25

RADIX = 128
HGRN_CHUNK = 64
HGRN_SUB_BLOCK = 256
SUB_ROWS = 256
SAFE_LOG_DECAY = 80.0
VMEM_LIMIT_BYTES = 56 * 1024 * 1024


def _cparams(n_axes=1):
    return pltpu.CompilerParams(dimension_semantics=("arbitrary",) * n_axes,
                                vmem_limit_bytes=VMEM_LIMIT_BYTES)


def _const_spec(shape):
    nd = len(shape)
    return pl.BlockSpec(shape, lambda *_: (0,) * nd, pipeline_mode=pl.Buffered(1))


def _const_cols(arr, width, j):
    return pl.BlockSpec((arr.shape[0], width), lambda *_: (0, j), pipeline_mode=pl.Buffered(1))


def _sigmoid(x):
    return 1.0 / (1.0 + jnp.exp(-x))


def _layernorm(x, g, b):
    mu = jnp.mean(x, axis=-1, keepdims=True)
    xc = x - mu
    var = jnp.mean(xc * xc, axis=-1, keepdims=True)
    return xc * lax.rsqrt(var + LN_EPS) * g + b


def _dot(a, b):
    return jnp.dot(a, b, preferred_element_type=F32)


def _dot_nt(a, b):
    return lax.dot_general(a, b, (((1,), (1,)), ((), ())), preferred_element_type=F32)


def _dot_tn(a, b):
    return lax.dot_general(a, b, (((0,), (0,)), ((), ())), preferred_element_type=F32)


def _mod_kernel(cl_ref, cx_ref, w_ref, b_ref, ol_ref, ox_ref):
    w = w_ref[...]
    for c_ref, o_ref in ((cl_ref, ol_ref), (cx_ref, ox_ref)):
        cs = c_ref[...]
        s = cs * _sigmoid(cs)
        o_ref[...] = jnp.sum(s * w, axis=0, keepdims=True) + b_ref[...]


def _mod_vectors(c_lat, c_ctx, w_ada, b_ada):
    d, n = w_ada.shape
    tn = 1536
    col = lambda j: (0, j)
    return pl.pallas_call(
        _mod_kernel,
        grid=(n // tn,),
        in_specs=[_const_spec((d, 1)), _const_spec((d, 1)),
                  pl.BlockSpec((d, tn), col), pl.BlockSpec((1, tn), col)],
        out_specs=[pl.BlockSpec((1, tn), col)] * 2,
        out_shape=[jax.ShapeDtypeStruct((1, n), F32)] * 2,
        compiler_params=_cparams(),
        name="mod",
    )(c_lat.reshape(d, 1), c_ctx.reshape(d, 1), w_ada, b_ada)


def _ln_in_modulated(x_ref, er_ref, ec_ref, lng_ref, lnb_ref, mod_ref, r0, nr):
    x = x_ref[r0:r0 + nr, :]
    half = x.shape[1] // 2
    nrow = nr // GRID_W
    e0 = r0 // GRID_W
    left = jnp.concatenate(
        [jnp.broadcast_to(er_ref[e0 + r:e0 + r + 1, :], (GRID_W, half)) for r in range(nrow)], axis=0)
    right = jnp.concatenate([ec_ref[...]] * nrow, axis=0)
    xp = jnp.concatenate([x[:, :half] + left, x[:, half:] + right], axis=1)
    xl = _layernorm(xp, lng_ref[...], lnb_ref[...])
    hl = xl * (1.0 + mod_ref[1:2, :]) + mod_ref[0:1, :]
    return xl, hl


def _inproj_kernel(x_ref, er_ref, ec_ref, lng_ref, lnb_ref, mod_ref, w_ref, b_ref, dft_ref,
                   l0_ref, l1_ref, xc_ref, xs_ref, q_ref, v_ref, g_ref, *, tm):
    w4 = N_GROUPS * GROUP_DIM
    dft = dft_ref[...].astype(BF16)
    l0 = l0_ref[...]
    l1 = l1_ref[...]
    m = jnp.maximum(l0, l1)
    e0 = jnp.exp(l0 - m)
    lb = e0 / (e0 + jnp.exp(l1 - m))
    sub = min(tm, SUB_ROWS)
    for r0 in range(0, tm, sub):
        rows = slice(r0, r0 + sub)
        _, hl = _ln_in_modulated(x_ref, er_ref, ec_ref, lng_ref, lnb_ref, mod_ref, r0, sub)
        hb = hl.astype(BF16)
        fp = _dot(hb, w_ref[:, 3 * w4:5 * w4]) + b_ref[:, 3 * w4:5 * w4]
        g_ref[rows, :] = jnp.log(lb + (1.0 - lb) * _sigmoid(fp))
        qp = _dot(hb, w_ref[:, w4:2 * w4]) + b_ref[:, w4:2 * w4]
        q_ref[rows, :] = (qp * _sigmoid(qp)).astype(BF16)
        u = (_dot(hb, w_ref[:, 0:w4]) + b_ref[:, 0:w4]).astype(BF16)
        for gi in range(N_GROUPS):
            z = _dot(u[:, gi * GROUP_DIM:(gi + 1) * GROUP_DIM], dft)
            xc_ref[gi, rows, :] = z[:, :GROUP_DIM].astype(BF16)
            xs_ref[gi, rows, :] = z[:, GROUP_DIM:].astype(BF16)
        v_ref[rows, :] = (_dot(hb, w_ref[:, 2 * w4:3 * w4]) + b_ref[:, 2 * w4:3 * w4]).astype(BF16)


def _inproj(x, er, ec, lng, lnb, mod, w, b, dft, l0, l1, tm):
    t, d = x.shape
    w4 = N_GROUPS * GROUP_DIM
    nrow = tm // GRID_W
    row = lambda i: (i, 0)
    return pl.pallas_call(
        functools.partial(_inproj_kernel, tm=tm),
        grid=(t // tm,),
        in_specs=[pl.BlockSpec((tm, d), row),
                  pl.BlockSpec((nrow, d // 2), row),
                  _const_spec(ec.shape), _const_spec(lng.shape), _const_spec(lnb.shape),
                  _const_spec(mod.shape), _const_cols(w, 5 * w4, 0), _const_cols(b, 5 * w4, 0),
                  _const_spec(dft.shape), _const_spec(l0.shape), _const_spec(l1.shape)],
        out_specs=[pl.BlockSpec((N_GROUPS, tm, GROUP_DIM), lambda i: (0, i, 0))] * 2
        + [pl.BlockSpec((tm, w4), row)] * 2 + [pl.BlockSpec((tm, 2 * w4), row)],
        out_shape=[jax.ShapeDtypeStruct((N_GROUPS, t, GROUP_DIM), BF16)] * 2
        + [jax.ShapeDtypeStruct((t, w4), BF16)] * 2 + [jax.ShapeDtypeStruct((t, 2 * w4), F32)],
        compiler_params=_cparams(),
        name="inproj",
    )(x, er, ec, lng, lnb, mod, w, b, dft, l0, l1)


DFT_BATCH = 16
F32_SUBLANES = 8


def _dft1_kernel(m_ref, xc_ref, xs_ref, a_ref, sx_ref, sa_ref):
    r, nb, w = xc_ref.shape
    hs = F32_SUBLANES
    for p, x_ref in enumerate((xc_ref, xs_ref)):
        x = x_ref[...].astype(F32)
        for h in range(nb // hs):
            sx_ref[p, h] = x[:, h * hs:(h + 1) * hs, :].reshape(r * hs, w)
    cols = []
    for j in range(nb):
        cols.append(jnp.concatenate(
            [sx_ref[p, j // hs, pl.ds(j % hs, r, stride=hs), :] for p in range(2)], axis=0).astype(BF16))
    a = _dot(m_ref[...].astype(BF16), jnp.concatenate(cols, axis=1))
    for j in range(nb):
        sa_ref[j // hs, pl.ds(j % hs, 2 * r, stride=hs), :] = a[:, j * w:(j + 1) * w]
    a_ref[...] = jnp.concatenate([sa_ref[h].reshape(2, r, hs, w) for h in range(nb // hs)],
                                 axis=2).astype(BF16)


def _dft1(mat, xc, xs):
    ng, t, w = xc.shape
    r = RADIX
    nb = 2 * DFT_BATCH
    nh = nb // F32_SUBLANES
    blk = pl.BlockSpec((None, r, nb, w), lambda g, o: (g, 0, o, 0))
    return pl.pallas_call(
        _dft1_kernel,
        grid=(ng, r // nb),
        in_specs=[_const_spec(mat.shape), blk, blk],
        out_specs=pl.BlockSpec((2, r, nb, w), lambda g, o: (0, 0, o, g)),
        out_shape=jax.ShapeDtypeStruct((2, r, r, ng * w), BF16),
        scratch_shapes=[pltpu.VMEM((2, nh, r * F32_SUBLANES, w), F32),
                        pltpu.VMEM((nh, 2 * r * F32_SUBLANES, w), F32)],
        compiler_params=_cparams(2),
        name="dft1",
    )(mat, xc.reshape(ng, r, r, w), xs.reshape(ng, r, r, w))


def _dft2_kernel(cs_ref, tw_ref, a_ref, o_ref, s_ref):
    nb, r, w = a_ref.shape[1], a_ref.shape[2], a_ref.shape[3]
    hs = F32_SUBLANES
    cos_a, sin_a = cs_ref[0], cs_ref[1]
    for j in range(nb):
        cos_b, sin_b = tw_ref[j:j + 1, 0:r], tw_ref[j:j + 1, r:2 * r]
        gmat = jnp.concatenate([cos_a * cos_b - sin_a * sin_b, sin_a * cos_b + cos_a * sin_b], axis=1)
        a = jnp.concatenate([a_ref[0, j], a_ref[1, j]], axis=0)
        y = _dot(gmat.astype(BF16), a)
        for gi in range(N_GROUPS):
            s_ref[gi, j // hs, pl.ds(j % hs, r, stride=hs), :] = y[:, gi * GROUP_DIM:(gi + 1) * GROUP_DIM]
    for gi in range(N_GROUPS):
        o_ref[gi] = jnp.concatenate([s_ref[gi, h].reshape(r, hs, GROUP_DIM) for h in range(nb // hs)],
                                    axis=1).astype(BF16)


def _dft2(base, twiddle, a4):
    _, r, _, w = a4.shape
    nb = DFT_BATCH
    out = pl.pallas_call(
        _dft2_kernel,
        grid=(r // nb,),
        in_specs=[_const_spec(base.shape),
                  pl.BlockSpec((nb, 2 * r), lambda k: (k, 0)),
                  pl.BlockSpec((2, nb, r, w), lambda k: (0, k, 0, 0))],
        out_specs=pl.BlockSpec((N_GROUPS, r, None, nb, GROUP_DIM), lambda k: (0, 0, k, 0, 0)),
        out_shape=jax.ShapeDtypeStruct((N_GROUPS, r, r // nb, nb, GROUP_DIM), BF16),
        scratch_shapes=[pltpu.VMEM((N_GROUPS, nb // F32_SUBLANES, r * F32_SUBLANES, GROUP_DIM), F32)],
        compiler_params=_cparams(),
        name="dft2",
    )(base, twiddle, a4)
    return out.reshape(N_GROUPS, r * r, GROUP_DIM)


def _hgrn_tables(L, reverse):
    nlev = int(np.log2(L))
    idx = np.arange(L)
    t = idx[:, None]
    i = idx[None, :]
    blocks = [(i >= t) if reverse else (i <= t)]
    for j in range(nlev):
        h = L >> (j + 1)
        mid = (t // (2 * h)) * (2 * h) + h
        upper = t >= mid
        if reverse:
            blk = np.where(upper, (i >= mid) & (i < t), (i >= t) & (i < mid))
        else:
            blk = np.where(upper, (i >= mid) & (i <= t), (i > t) & (i < mid))
        blocks.append(blk)
    blocks.append((i < t) if reverse else (i > t))
    return np.concatenate(blocks, axis=0).astype(np.float32), nlev


def _as_column(row):
    n = row.shape[1]
    return jnp.broadcast_to(row, (n, n)).T


def _split_hi_lo(g):
    hi = g.astype(BF16)
    return hi, (g - hi.astype(F32)).astype(BF16)


def _hgrn_chunk_exact(q_ref, v_ref, g_ref, o_ref, st_ref, rows, mall, pair_masks, query_rows,
                      L, nlev, reverse):
    last = 0 if reverse else L - 1
    g = g_ref[rows, :]
    g_hi, g_lo = _split_hi_lo(g)
    ex = jnp.exp(_dot(mall, g_hi) + _dot(mall, g_lo))
    q = q_ref[rows, :].astype(F32)
    v = v_ref[rows, :]
    k = 1.0 - jnp.exp(g)
    e_cum = ex[0:L]
    qe = (q * e_cum).astype(BF16)
    ke = (k * ex[(nlev + 1) * L:(nlev + 2) * L]).astype(BF16)
    zs = [(jnp.where(query_rows[j], q, k) * ex[(j + 1) * L:(j + 2) * L]).astype(BF16)
          for j in range(nlev)]
    qk = q * k
    e_last = e_cum[last:last + 1, :]
    for hd in range(N_HEADS):
        sl = slice(hd * HEAD_DIM, (hd + 1) * HEAD_DIM)
        sc = jnp.zeros((L, L), F32)
        for j in range(nlev):
            zj = zs[j][:, sl]
            sc = jnp.where(pair_masks[j], _dot_nt(zj, zj), sc)
        st = st_ref[hd]
        vh = v[:, sl]
        o = _dot(sc.astype(BF16), vh) + _dot(qe[:, sl], st.astype(BF16))
        o = o + jnp.sum(qk[:, sl], axis=-1, keepdims=True) * vh.astype(F32)
        o_ref[rows, sl] = o
        st_ref[hd] = st * _as_column(e_last[:, sl]) + _dot_tn(ke[:, sl], vh)


def _hgrn_block_fast(q_ref, v_ref, g_ref, trib_ref, o_ref, st_ref, r0, nchunk, L, reverse):
    half = L // 2
    tb = nchunk * L
    blk = slice(r0, r0 + tb)
    chunks = []
    for c in range(nchunk):
        base = c * L
        if reverse:
            chunks.append((slice(base, base + L), slice(base + half, base + L),
                           slice(base, base + half), base + half, base))
        else:
            chunks.append((slice(base, base + L), slice(base, base + half),
                           slice(base + half, base + L), base + half - 1, base + L - 1))
    g = g_ref[blk, :]
    b = _dot(trib_ref[...], g.astype(BF16))
    q = q_ref[blk, :].astype(F32)
    v = v_ref[blk, :]
    k = 1.0 - jnp.exp(g)
    e_b = jnp.exp(b)
    qe = (q * e_b).astype(BF16)
    c2 = [b[sec] - b[edge:edge + 1, :] for (_, _, sec, edge, _) in chunks]
    q2 = jnp.concatenate([q[ch[2]] * jnp.exp(c2[c]) for c, ch in enumerate(chunks)], axis=0)
    own = []
    for c, (_, fst, _, _, _) in enumerate(chunks):
        own += [c2[c], b[fst]] if reverse else [b[fst], c2[c]]
    kh = (k * jnp.exp(-jnp.concatenate(own, axis=0))).astype(BF16)
    tail = jnp.concatenate([b[last:last + 1, :] - b[rows] for (rows, _, _, _, last) in chunks], axis=0)
    ke = (k * jnp.exp(tail)).astype(BF16)
    lhs = jnp.concatenate([qe, q2.astype(BF16)], axis=0)

    row = lax.broadcasted_iota(jnp.int32, (tb, tb), 0)
    col = lax.broadcasted_iota(jnp.int32, (tb, tb), 1)
    valid = (row // L == col // L) & ((col >= row) if reverse else (col <= row))
    if reverse:
        use_near = (row % L < half) & (col % L < half)
    else:
        use_near = (row % L >= half) & (col % L >= half)

    for hd in range(N_HEADS):
        sl = slice(hd * HEAD_DIM, (hd + 1) * HEAD_DIM)
        s_all = _dot_nt(lhs[:, sl], kh[:, sl])
        far = s_all[0:tb]
        pieces = []
        for c, (_, fst, _, _, _) in enumerate(chunks):
            near_c = s_all[tb + c * half:tb + (c + 1) * half]
            pieces += [near_c, far[fst]] if reverse else [far[fst], near_c]
        near = jnp.concatenate(pieces, axis=0)
        sc = jnp.where(valid, jnp.where(use_near, near, far), 0.0).astype(BF16)
        vh = v[:, sl]
        o_intra = _dot(sc, vh)
        upd = [_dot_tn(ke[rows, sl], vh[rows]) for (rows, _, _, _, _) in chunks]
        st = st_ref[hd]
        o_inter = [None] * nchunk
        for c in (range(nchunk - 1, -1, -1) if reverse else range(nchunk)):
            rows, _, _, _, last = chunks[c]
            o_inter[c] = _dot(qe[rows, sl], st.astype(BF16))
            st = st * _as_column(e_b[last:last + 1, sl]) + upd[c]
        st_ref[hd] = st
        o_ref[blk, sl] = o_intra + jnp.concatenate(o_inter, axis=0)


def _hgrn_block_exact(q_ref, v_ref, g_ref, mall_ref, o_ref, st_ref, nchunk, L, nlev, reverse):
    row = lax.broadcasted_iota(jnp.int32, (L, L), 0)
    col = lax.broadcasted_iota(jnp.int32, (L, L), 1)
    rowc = lax.broadcasted_iota(jnp.int32, (L, N_HEADS * HEAD_DIM), 0)
    pair_masks, query_rows = [], []
    for j in range(nlev):
        h = L >> (j + 1)
        same = (row // (2 * h)) == (col // (2 * h))
        row_up = (row // h) % 2 == 1
        col_up = (col // h) % 2 == 1
        if reverse:
            pair_masks.append(same & jnp.logical_not(row_up) & col_up)
            query_rows.append((rowc // h) % 2 == 0)
        else:
            pair_masks.append(same & row_up & jnp.logical_not(col_up))
            query_rows.append((rowc // h) % 2 == 1)
    mall = mall_ref[...]
    for c in (range(nchunk - 1, -1, -1) if reverse else range(nchunk)):
        _hgrn_chunk_exact(q_ref, v_ref, g_ref, o_ref, st_ref, slice(c * L, (c + 1) * L),
                          mall, pair_masks, query_rows, L, nlev, reverse)


def _min_leaf_log_decay(g_ref, leaf):
    g = g_ref[...]
    return jnp.min(jnp.sum(g.reshape(g.shape[0] // leaf, leaf, g.shape[1]), axis=1))


def _hgrn_kernel(qf_ref, vf_ref, gf_ref, qb_ref, vb_ref, gb_ref, mallf_ref, mallb_ref,
                 tribf_ref, tribb_ref, s0f_ref, s0b_ref, of_ref, ob_ref, sff_ref, sfb_ref,
                 stf_ref, stb_ref, *, nsub, nchunk, L, nlev):
    i = pl.program_id(0)

    @pl.when(i == 0)
    def _():
        stf_ref[...] = s0f_ref[...]
        stb_ref[...] = s0b_ref[...]

    safe = jnp.minimum(_min_leaf_log_decay(gf_ref, L // 2),
                       _min_leaf_log_decay(gb_ref, L // 2)) >= -SAFE_LOG_DECAY

    @pl.when(safe)
    def _():
        for j in range(nsub):
            _hgrn_block_fast(qf_ref, vf_ref, gf_ref, tribf_ref, of_ref, stf_ref,
                             j * nchunk * L, nchunk, L, False)
            _hgrn_block_fast(qb_ref, vb_ref, gb_ref, tribb_ref, ob_ref, stb_ref,
                             (nsub - 1 - j) * nchunk * L, nchunk, L, True)

    @pl.when(jnp.logical_not(safe))
    def _():
        nc = nsub * nchunk
        _hgrn_block_exact(qf_ref, vf_ref, gf_ref, mallf_ref, of_ref, stf_ref, nc, L, nlev, False)
        _hgrn_block_exact(qb_ref, vb_ref, gb_ref, mallb_ref, ob_ref, stb_ref, nc, L, nlev, True)

    @pl.when(i == pl.num_programs(0) - 1)
    def _():
        sff_ref[...] = stf_ref[...]
        sfb_ref[...] = stb_ref[...]


def _hgrn_scan(q, v, g, s0_f, s0_b, tb):
    t, w = q.shape
    L = HGRN_CHUNK
    nblk = t // tb
    sb = min(tb, HGRN_SUB_BLOCK)
    consts = []
    for reverse in (False, True):
        mall_np, nlev = _hgrn_tables(L, reverse)
        consts.append((jnp.asarray(mall_np, dtype=BF16),
                       jnp.asarray(np.kron(np.eye(sb // L, dtype=np.float32), mall_np[0:L]), dtype=BF16)))
    (mall_f, trib_f), (mall_b, trib_b) = consts
    fwd = lambda i: (i, 0)
    bwd = lambda i: (nblk - 1 - i, 0)
    bwd_g = lambda i: (nblk - 1 - i, 1)
    blk = lambda m: pl.BlockSpec((tb, w), m)
    return pl.pallas_call(
        functools.partial(_hgrn_kernel, nsub=tb // sb, nchunk=sb // L, L=L, nlev=nlev),
        grid=(nblk,),
        in_specs=[blk(fwd), blk(fwd), blk(fwd), blk(bwd), blk(bwd), blk(bwd_g),
                  _const_spec(mall_f.shape), _const_spec(mall_b.shape),
                  _const_spec(trib_f.shape), _const_spec(trib_b.shape),
                  _const_spec(s0_f.shape), _const_spec(s0_b.shape)],
        out_specs=[blk(fwd), blk(bwd)] + [pl.BlockSpec(s0_f.shape, lambda i: (0, 0, 0))] * 2,
        out_shape=[jax.ShapeDtypeStruct((t, w), F32)] * 2 + [jax.ShapeDtypeStruct(s0_f.shape, F32)] * 2,
        scratch_shapes=[pltpu.VMEM(s0_f.shape, F32)] * 2,
        compiler_params=_cparams(),
        name="hgrn",
    )(q, v, g, q, v, g, mall_f, mall_b, trib_f, trib_b, s0_f, s0_b)


def _merge_kernel(x_ref, er_ref, ec_ref, lng_ref, lnb_ref, mod_ref, wg_ref, bg_ref, of_ref, ob_ref,
                  ng_ref, four_ref, wfp_ref, whp_ref, wo_ref, bo_ref, o_ref, *, tm):
    w4 = N_HEADS * HEAD_DIM
    d = x_ref.shape[1]
    sub = min(tm, SUB_ROWS)
    for r0 in range(0, tm, sub):
        rows = slice(r0, r0 + sub)
        xl, hl = _ln_in_modulated(x_ref, er_ref, ec_ref, lng_ref, lnb_ref, mod_ref, r0, sub)
        hb = hl.astype(BF16)
        og = _dot(hb, wg_ref[:, 0:w4]) + bg_ref[:, 0:w4]
        o = of_ref[rows, :] + ob_ref[rows, :]
        parts = []
        for hd in range(N_HEADS):
            oh = o[:, hd * HEAD_DIM:(hd + 1) * HEAD_DIM]
            ms = jnp.mean(oh * oh, axis=-1, keepdims=True)
            parts.append(oh * lax.rsqrt(ms + RMS_EPS))
        on = jnp.concatenate(parts, axis=1) * ng_ref[...]
        oh = (on * (og * _sigmoid(og))).astype(BF16)
        g_four = _sigmoid(_dot(hb, wg_ref[:, w4:w4 + d]) + bg_ref[:, w4:w4 + d])
        four = jnp.concatenate([four_ref[gi, rows, :] for gi in range(N_GROUPS)], axis=1)
        y = g_four * _dot(four.astype(BF16), wfp_ref[...])
        g_hgrn = _sigmoid(_dot(hb, wg_ref[:, w4 + d:w4 + 2 * d]) + bg_ref[:, w4 + d:w4 + 2 * d])
        y = y + g_hgrn * _dot(oh, whp_ref[...])
        mix = _dot(y.astype(BF16), wo_ref[...]) + bo_ref[...]
        o_ref[rows, :] = ALPHA * xl + mod_ref[2:3, :] * mix


def _merge(x, er, ec, lng, lnb, mod, wg, bg, o_f, o_b, ng, four, wfp, whp, wo, bo, tm):
    t, d = x.shape
    w4 = N_HEADS * HEAD_DIM
    nrow = tm // GRID_W
    row = lambda i: (i, 0)
    consts = [ec, lng, lnb, mod]
    consts2 = [wfp, whp, wo, bo]
    return pl.pallas_call(
        functools.partial(_merge_kernel, tm=tm),
        grid=(t // tm,),
        in_specs=[pl.BlockSpec((tm, d), row), pl.BlockSpec((nrow, d // 2), row)]
        + [_const_spec(a.shape) for a in consts]
        + [_const_cols(wg, w4 + 2 * d, 1), _const_cols(bg, w4 + 2 * d, 1)]
        + [pl.BlockSpec((tm, w4), row), pl.BlockSpec((tm, w4), row), _const_spec(ng.shape),
           pl.BlockSpec((N_GROUPS, tm, GROUP_DIM), lambda i: (0, i, 0))]
        + [_const_spec(a.shape) for a in consts2],
        out_specs=pl.BlockSpec((tm, d), row),
        out_shape=jax.ShapeDtypeStruct((t, d), F32),
        compiler_params=_cparams(),
        name="merge",
    )(x, er, *consts, wg, bg, o_f, o_b, ng, four, *consts2)


MLP_WEIGHT_STEPS = 8


def _mlp_kernel(x_ref, mod_ref, ag_ref, ab_ref, w1_ref, b1_ref, w2_ref, b2_ref, pg_ref, pb_ref, o_ref,
                w1s_ref, w2s_ref, *, nsplit):
    i = pl.program_id(0)
    dff = w1s_ref.shape[1]
    cw = dff // nsplit
    tm = x_ref.shape[0]

    @pl.when(i < MLP_WEIGHT_STEPS)
    def _():
        r1, r2 = w1_ref.shape[0], w2_ref.shape[0]
        w1s_ref[pl.ds(pl.multiple_of(i * r1, r1), r1), :] = w1_ref[...].astype(BF16)
        w2s_ref[pl.ds(pl.multiple_of(i * r2, r2), r2), :] = w2_ref[...].astype(BF16)

    @pl.when(i >= MLP_WEIGHT_STEPS)
    def _():
        sub = min(tm, SUB_ROWS)
        for r0 in range(0, tm, sub):
            rows = slice(r0, r0 + sub)
            x1 = _layernorm(x_ref[rows, :], ag_ref[...], ab_ref[...])
            hb = (x1 * (1.0 + mod_ref[4:5, :]) + mod_ref[3:4, :]).astype(BF16)
            acc = jnp.zeros(x1.shape, F32)
            for c in range(nsplit):
                cs = slice(c * cw, (c + 1) * cw)
                a = jnp.maximum(_dot(hb, w1s_ref[:, cs]) + b1_ref[:, cs], 0.0)
                acc = acc + _dot((a * a).astype(BF16), w2s_ref[cs, :])
            m = acc + b2_ref[...]
            o_ref[rows, :] = _layernorm(ALPHA * x1 + mod_ref[5:6, :] * m, pg_ref[...], pb_ref[...])


def _mlp(z1, mod, ag, ab, w1, b1, w2, b2, pg, pb, tm):
    t, d = z1.shape
    dff = w1.shape[1]
    nw = MLP_WEIGHT_STEPS
    row = lambda i: (jnp.maximum(i - nw, 0), 0)
    wrow = lambda i: (jnp.minimum(i, nw - 1), 0)
    cs = _const_spec
    return pl.pallas_call(
        functools.partial(_mlp_kernel, nsplit=4),
        grid=(nw + t // tm,),
        in_specs=[pl.BlockSpec((tm, d), row), cs(mod.shape), cs(ag.shape), cs(ab.shape),
                  pl.BlockSpec((d // nw, dff), wrow), cs(b1.shape),
                  pl.BlockSpec((dff // nw, d), wrow), cs(b2.shape), cs(pg.shape), cs(pb.shape)],
        out_specs=pl.BlockSpec((tm, d), row),
        out_shape=jax.ShapeDtypeStruct((t, d), F32),
        scratch_shapes=[pltpu.VMEM((d, dff), BF16), pltpu.VMEM((dff, d), BF16)],
        compiler_params=_cparams(),
        name="mlp",
    )(z1, mod, ag, ab, w1, b1, w2, b2, pg, pb)


def _pos_tables(rows, cols, dim):
    quarter = dim // 4
    omega = 1.0 / (POS_BASE ** (np.arange(quarter, dtype=np.float64) / quarter))
    r = np.arange(rows, dtype=np.float64)[:, None] * omega
    cc = np.arange(cols, dtype=np.float64)[:, None] * omega
    er = np.concatenate([np.sin(r), np.cos(r)], axis=-1)
    ec = np.concatenate([np.sin(cc), np.cos(cc)], axis=-1)
    return jnp.asarray(er.astype(np.float32)), jnp.asarray(ec.astype(np.float32))


def _dft_constants(t):
    n = RADIX
    kn = np.outer(np.arange(n), np.arange(n)).astype(np.float64)
    c = np.cos(2.0 * np.pi * kn / n)
    s = np.sin(2.0 * np.pi * kn / n)
    chan = np.concatenate([c, s], axis=1)
    stage1 = np.block([[c, -s], [-s, -c]])
    scale = 1.0 / np.sqrt(float(t) * GROUP_DIM)
    base2 = np.stack([c, s]) * scale
    beta = 2.0 * np.pi * kn / t
    twiddle = np.concatenate([np.cos(beta), np.sin(beta)], axis=1)
    as_f32 = lambda a: jnp.asarray(a.astype(np.float32))
    return as_f32(chan), as_f32(stage1), as_f32(base2), as_f32(twiddle)


def kernel(x, c, ctx, c_ctx, ln_in_g, ln_in_b, w_ada, b_ada, w_in, b_in, hgrn_lb_logits, hgrn_norm_g,
           w_four_proj, w_hgrn_proj, w_out, b_out, w_mlp1, b_mlp1, w_mlp2, b_mlp2, ln_post_g, ln_post_b):
    B, T, D = x.shape
    assert B == 1 and T == RADIX * RADIX and T % GRID_W == 0
    TC = ctx.shape[1]
    w4 = N_HEADS * HEAD_DIM
    row2 = lambda a: a.reshape(1, -1)

    mod_l, mod_c = _mod_vectors(c[0], c_ctx, w_ada[0], row2(b_ada[0]))
    mod_l = mod_l.reshape(6, D)
    mod_c = mod_c.reshape(6, D)

    er, ec = _pos_tables(T // GRID_W, GRID_W, D)
    lng, lnb = row2(ln_in_g), row2(ln_in_b)
    w_in_b = w_in[0].astype(BF16)
    b_in2 = row2(b_in[0])
    assert w_in_b.shape[1] == 2 * 5 * w4
    w_a = w_g = w_in_b
    b_a = b_g = b_in2
    l0 = hgrn_lb_logits[:, 0, :].reshape(1, 2 * w4)
    l1 = hgrn_lb_logits[:, 1, :].reshape(1, 2 * w4)
    dft_chan, dft_s1, dft_base2, dft_tw = _dft_constants(T)

    zc = jnp.zeros((TC // GRID_W, D // 2), F32)
    _, _, qc, vc, gc = _inproj(ctx[0], zc, jnp.zeros_like(ec), lng, lnb, mod_c, w_a, b_a, dft_chan,
                               l0, l1, tm=TC)
    s_zero = jnp.zeros((N_HEADS, HEAD_DIM, HEAD_DIM), F32)
    _, _, s_f, s_b = _hgrn_scan(qc, vc, gc, s_zero, s_zero, tb=TC)

    xc, xs, q, v, g = _inproj(x[0], er, ec, lng, lnb, mod_l, w_a, b_a, dft_chan, l0, l1, tm=1024)
    four = _dft2(dft_base2, dft_tw, _dft1(dft_s1, xc, xs))
    o_f, o_b, _, _ = _hgrn_scan(q, v, g, s_f, s_b, tb=512)

    z1 = _merge(x[0], er, ec, lng, lnb, mod_l, w_g, b_g, o_f, o_b, row2(hgrn_norm_g[0]), four,
                w_four_proj[0].astype(BF16), w_hgrn_proj[0].astype(BF16), w_out[0].astype(BF16),
                row2(b_out[0]), tm=1024)
    out = _mlp(z1, mod_l, row2(ln_post_g[0, 0]), row2(ln_post_b[0, 0]), w_mlp1[0], row2(b_mlp1[0]),
               w_mlp2[0], row2(b_mlp2[0]), row2(ln_post_g[0, 1]), row2(ln_post_b[0, 1]), tm=1024)
    return out[None]
```

```python
import functools

import numpy as np
import jax
import jax.numpy as jnp
from jax import lax
from jax.experimental import pallas as pl
from jax.experimental.pallas import tpu as pltpu

F32 = jnp.float32
BF16 = jnp.bfloat16

GRID_W = 64
N_GROUPS = 4
GROUP_DIM = 128
N_HEADS = 4
HEAD_DIM = 128
POS_BASE = 10000.0
LN_EPS = 1e-5
RMS_EPS = 1e-6
DEPTH = 1
ALPHA = (2.0 * DEPTH) ** 0.25

RADIX = 128
HGRN_CHUNK = 64
HGRN_SUB_BLOCK = 256
SUB_ROWS = 256
SAFE_LOG_DECAY = 80.0
VMEM_LIMIT_BYTES = 56 * 1024 * 1024


def _cparams(n_axes=1):
    return pltpu.CompilerParams(dimension_semantics=("arbitrary",) * n_axes,
                                vmem_limit_bytes=VMEM_LIMIT_BYTES)


def _const_spec(shape):
    nd = len(shape)
    return pl.BlockSpec(shape, lambda *_: (0,) * nd, pipeline_mode=pl.Buffered(1))


def _const_cols(arr, width, j):
    return pl.BlockSpec((arr.shape[0], width), lambda *_: (0, j), pipeline_mode=pl.Buffered(1))


def _sigmoid(x):
    return 1.0 / (1.0 + jnp.exp(-x))


def _layernorm(x, g, b):
    mu = jnp.mean(x, axis=-1, keepdims=True)
    xc = x - mu
    var = jnp.mean(xc * xc, axis=-1, keepdims=True)
    return xc * lax.rsqrt(var + LN_EPS) * g + b


def _dot(a, b):
    return jnp.dot(a, b, preferred_element_type=F32)


def _dot_nt(a, b):
    return lax.dot_general(a, b, (((1,), (1,)), ((), ())), preferred_element_type=F32)


def _dot_tn(a, b):
    return lax.dot_general(a, b, (((0,), (0,)), ((), ())), preferred_element_type=F32)


def _narrow_specs(arrays, nsteps):
    chunk = lambda a: pl.BlockSpec((a.shape[0] // nsteps, a.shape[1]), lambda i, *_: (i, 0))
    return ([chunk(a) for a in arrays], [chunk(a) for a in arrays],
            [jax.ShapeDtypeStruct(a.shape, BF16) for a in arrays])


def _narrow(src_refs, dst_refs):
    for s_ref, d_ref in zip(src_refs, dst_refs):
        d_ref[...] = s_ref[...].astype(BF16)


def _mod_kernel(cl_ref, cx_ref, w_ref, b_ref, win_ref, ol_ref, ox_ref, winb_ref):
    w = w_ref[...]
    for c_ref, o_ref in ((cl_ref, ol_ref), (cx_ref, ox_ref)):
        cs = c_ref[...]
        s = cs * _sigmoid(cs)
        o_ref[...] = jnp.sum(s * w, axis=0, keepdims=True) + b_ref[...]
    _narrow([win_ref], [winb_ref])


def _mod_vectors(c_lat, c_ctx, w_ada, b_ada, w_in):
    d, n = w_ada.shape
    tn = 1536
    col = lambda j: (0, j)
    n_in, n_out, n_shape = _narrow_specs([w_in], n // tn)
    return pl.pallas_call(
        _mod_kernel,
        grid=(n // tn,),
        in_specs=[_const_spec((d, 1)), _const_spec((d, 1)),
                  pl.BlockSpec((d, tn), col), pl.BlockSpec((1, tn), col)] + n_in,
        out_specs=[pl.BlockSpec((1, tn), col)] * 2 + n_out,
        out_shape=[jax.ShapeDtypeStruct((1, n), F32)] * 2 + n_shape,
        compiler_params=_cparams(),
        name="mod",
    )(c_lat.reshape(d, 1), c_ctx.reshape(d, 1), w_ada, b_ada, w_in)


def _ln_in_modulated(x_ref, er_ref, ec_ref, lng_ref, lnb_ref, mod_ref, r0, nr):
    x = x_ref[r0:r0 + nr, :]
    half = x.shape[1] // 2
    nrow = nr // GRID_W
    e0 = r0 // GRID_W
    left = jnp.concatenate(
        [jnp.broadcast_to(er_ref[e0 + r:e0 + r + 1, :], (GRID_W, half)) for r in range(nrow)], axis=0)
    right = jnp.concatenate([ec_ref[...]] * nrow, axis=0)
    xp = jnp.concatenate([x[:, :half] + left, x[:, half:] + right], axis=1)
    xl = _layernorm(xp, lng_ref[...], lnb_ref[...])
    hl = xl * (1.0 + mod_ref[1:2, :]) + mod_ref[0:1, :]
    return xl, hl


def _inproj_kernel(x_ref, er_ref, ec_ref, lng_ref, lnb_ref, mod_ref, w_ref, b_ref, dft_ref,
                   l0_ref, l1_ref, xc_ref, xs_ref, q_ref, v_ref, g_ref, *, tm):
    w4 = N_GROUPS * GROUP_DIM
    dft = dft_ref[...].astype(BF16)
    l0 = l0_ref[...]
    l1 = l1_ref[...]
    m = jnp.maximum(l0, l1)
    e0 = jnp.exp(l0 - m)
    lb = e0 / (e0 + jnp.exp(l1 - m))
    sub = min(tm, SUB_ROWS)
    for r0 in range(0, tm, sub):
        rows = slice(r0, r0 + sub)
        _, hl = _ln_in_modulated(x_ref, er_ref, ec_ref, lng_ref, lnb_ref, mod_ref, r0, sub)
        hb = hl.astype(BF16)
        fp = _dot(hb, w_ref[:, 3 * w4:5 * w4]) + b_ref[:, 3 * w4:5 * w4]
        g_ref[rows, :] = jnp.log(lb + (1.0 - lb) * _sigmoid(fp))
        qp = _dot(hb, w_ref[:, w4:2 * w4]) + b_ref[:, w4:2 * w4]
        q_ref[rows, :] = (qp * _sigmoid(qp)).astype(BF16)
        u = (_dot(hb, w_ref[:, 0:w4]) + b_ref[:, 0:w4]).astype(BF16)
        for gi in range(N_GROUPS):
            z = _dot(u[:, gi * GROUP_DIM:(gi + 1) * GROUP_DIM], dft)
            xc_ref[gi, rows, :] = z[:, :GROUP_DIM].astype(BF16)
            xs_ref[gi, rows, :] = z[:, GROUP_DIM:].astype(BF16)
        v_ref[rows, :] = (_dot(hb, w_ref[:, 2 * w4:3 * w4]) + b_ref[:, 2 * w4:3 * w4]).astype(BF16)


def _inproj(x, er, ec, lng, lnb, mod, w, b, dft, l0, l1, tm):
    t, d = x.shape
    w4 = N_GROUPS * GROUP_DIM
    nrow = tm // GRID_W
    row = lambda i: (i, 0)
    return pl.pallas_call(
        functools.partial(_inproj_kernel, tm=tm),
        grid=(t // tm,),
        in_specs=[pl.BlockSpec((tm, d), row),
                  pl.BlockSpec((nrow, d // 2), row),
                  _const_spec(ec.shape), _const_spec(lng.shape), _const_spec(lnb.shape),
                  _const_spec(mod.shape), _const_cols(w, 5 * w4, 0), _const_cols(b, 5 * w4, 0),
                  _const_spec(dft.shape), _const_spec(l0.shape), _const_spec(l1.shape)],
        out_specs=[pl.BlockSpec((N_GROUPS, tm, GROUP_DIM), lambda i: (0, i, 0))] * 2
        + [pl.BlockSpec((tm, w4), row)] * 2 + [pl.BlockSpec((tm, 2 * w4), row)],
        out_shape=[jax.ShapeDtypeStruct((N_GROUPS, t, GROUP_DIM), BF16)] * 2
        + [jax.ShapeDtypeStruct((t, w4), BF16)] * 2 + [jax.ShapeDtypeStruct((t, 2 * w4), F32)],
        compiler_params=_cparams(),
        name="inproj",
    )(x, er, ec, lng, lnb, mod, w, b, dft, l0, l1)


DFT_BATCH = 16
F32_SUBLANES = 8


def _dft1_kernel(m_ref, xc_ref, xs_ref, a_ref, sx_ref, sa_ref):
    r, nb, w = xc_ref.shape
    hs = F32_SUBLANES
    for p, x_ref in enumerate((xc_ref, xs_ref)):
        x = x_ref[...].astype(F32)
        for h in range(nb // hs):
            sx_ref[p, h] = x[:, h * hs:(h + 1) * hs, :].reshape(r * hs, w)
    cols = []
    for j in range(nb):
        cols.append(jnp.concatenate(
            [sx_ref[p, j // hs, pl.ds(j % hs, r, stride=hs), :] for p in range(2)], axis=0).astype(BF16))
    a = _dot(m_ref[...].astype(BF16), jnp.concatenate(cols, axis=1))
    for j in range(nb):
        sa_ref[j // hs, pl.ds(j % hs, 2 * r, stride=hs), :] = a[:, j * w:(j + 1) * w]
    a_ref[...] = jnp.concatenate([sa_ref[h].reshape(2, r, hs, w) for h in range(nb // hs)],
                                 axis=2).astype(BF16)


def _dft1(mat, xc, xs):
    ng, t, w = xc.shape
    r = RADIX
    nb = 2 * DFT_BATCH
    nh = nb // F32_SUBLANES
    blk = pl.BlockSpec((None, r, nb, w), lambda g, o: (g, 0, o, 0))
    return pl.pallas_call(
        _dft1_kernel,
        grid=(ng, r // nb),
        in_specs=[_const_spec(mat.shape), blk, blk],
        out_specs=pl.BlockSpec((2, r, nb, w), lambda g, o: (0, 0, o, g)),
        out_shape=jax.ShapeDtypeStruct((2, r, r, ng * w), BF16),
        scratch_shapes=[pltpu.VMEM((2, nh, r * F32_SUBLANES, w), F32),
                        pltpu.VMEM((nh, 2 * r * F32_SUBLANES, w), F32)],
        compiler_params=_cparams(2),
        name="dft1",
    )(mat, xc.reshape(ng, r, r, w), xs.reshape(ng, r, r, w))


def _dft2_kernel(cs_ref, tw_ref, a_ref, o_ref, s_ref):
    nb, r, w = a_ref.shape[1], a_ref.shape[2], a_ref.shape[3]
    hs = F32_SUBLANES
    cos_a, sin_a = cs_ref[0], cs_ref[1]
    for j in range(nb):
        cos_b, sin_b = tw_ref[j:j + 1, 0:r], tw_ref[j:j + 1, r:2 * r]
        gmat = jnp.concatenate([cos_a * cos_b - sin_a * sin_b, sin_a * cos_b + cos_a * sin_b], axis=1)
        a = jnp.concatenate([a_ref[0, j], a_ref[1, j]], axis=0)
        y = _dot(gmat.astype(BF16), a)
        for gi in range(N_GROUPS):
            s_ref[gi, j // hs, pl.ds(j % hs, r, stride=hs), :] = y[:, gi * GROUP_DIM:(gi + 1) * GROUP_DIM]
    for gi in range(N_GROUPS):
        o_ref[gi] = jnp.concatenate([s_ref[gi, h].reshape(r, hs, GROUP_DIM) for h in range(nb // hs)],
                                    axis=1).astype(BF16)


def _dft2(base, twiddle, a4):
    _, r, _, w = a4.shape
    nb = DFT_BATCH
    out = pl.pallas_call(
        _dft2_kernel,
        grid=(r // nb,),
        in_specs=[_const_spec(base.shape),
                  pl.BlockSpec((nb, 2 * r), lambda k: (k, 0)),
                  pl.BlockSpec((2, nb, r, w), lambda k: (0, k, 0, 0))],
        out_specs=pl.BlockSpec((N_GROUPS, r, None, nb, GROUP_DIM), lambda k: (0, 0, k, 0, 0)),
        out_shape=jax.ShapeDtypeStruct((N_GROUPS, r, r // nb, nb, GROUP_DIM), BF16),
        scratch_shapes=[pltpu.VMEM((N_GROUPS, nb // F32_SUBLANES, r * F32_SUBLANES, GROUP_DIM), F32)],
        compiler_params=_cparams(),
        name="dft2",
    )(base, twiddle, a4)
    return out.reshape(N_GROUPS, r * r, GROUP_DIM)


def _hgrn_tables(L, reverse):
    nlev = int(np.log2(L))
    idx = np.arange(L)
    t = idx[:, None]
    i = idx[None, :]
    blocks = [(i >= t) if reverse else (i <= t)]
    for j in range(nlev):
        h = L >> (j + 1)
        mid = (t // (2 * h)) * (2 * h) + h
        upper = t >= mid
        if reverse:
            blk = np.where(upper, (i >= mid) & (i < t), (i >= t) & (i < mid))
        else:
            blk = np.where(upper, (i >= mid) & (i <= t), (i > t) & (i < mid))
        blocks.append(blk)
    blocks.append((i < t) if reverse else (i > t))
    return np.concatenate(blocks, axis=0).astype(np.float32), nlev


def _as_column(row):
    n = row.shape[1]
    return jnp.broadcast_to(row, (n, n)).T


def _split_hi_lo(g):
    hi = g.astype(BF16)
    return hi, (g - hi.astype(F32)).astype(BF16)


def _hgrn_chunk_exact(q_ref, v_ref, g_ref, o_ref, st_ref, rows, mall, pair_masks, query_rows,
                      L, nlev, reverse):
    last = 0 if reverse else L - 1
    g = g_ref[rows, :]
    g_hi, g_lo = _split_hi_lo(g)
    ex = jnp.exp(_dot(mall, g_hi) + _dot(mall, g_lo))
    q = q_ref[rows, :].astype(F32)
    v = v_ref[rows, :]
    k = 1.0 - jnp.exp(g)
    e_cum = ex[0:L]
    qe = (q * e_cum).astype(BF16)
    ke = (k * ex[(nlev + 1) * L:(nlev + 2) * L]).astype(BF16)
    zs = [(jnp.where(query_rows[j], q, k) * ex[(j + 1) * L:(j + 2) * L]).astype(BF16)
          for j in range(nlev)]
    qk = q * k
    e_last = e_cum[last:last + 1, :]
    for hd in range(N_HEADS):
        sl = slice(hd * HEAD_DIM, (hd + 1) * HEAD_DIM)
        sc = jnp.zeros((L, L), F32)
        for j in range(nlev):
            zj = zs[j][:, sl]
            sc = jnp.where(pair_masks[j], _dot_nt(zj, zj), sc)
        st = st_ref[hd]
        vh = v[:, sl]
        o = _dot(sc.astype(BF16), vh) + _dot(qe[:, sl], st.astype(BF16))
        o = o + jnp.sum(qk[:, sl], axis=-1, keepdims=True) * vh.astype(F32)
        o_ref[rows, sl] = o
        st_ref[hd] = st * _as_column(e_last[:, sl]) + _dot_tn(ke[:, sl], vh)


def _hgrn_block_fast(q_ref, v_ref, g_ref, trib_ref, o_ref, st_ref, r0, nchunk, L, reverse):
    half = L // 2
    tb = nchunk * L
    blk = slice(r0, r0 + tb)
    chunks = []
    for c in range(nchunk):
        base = c * L
        if reverse:
            chunks.append((slice(base, base + L), slice(base + half, base + L),
                           slice(base, base + half), base + half, base))
        else:
            chunks.append((slice(base, base + L), slice(base, base + half),
                           slice(base + half, base + L), base + half - 1, base + L - 1))
    g = g_ref[blk, :]
    b = _dot(trib_ref[...], g.astype(BF16))
    q = q_ref[blk, :].astype(F32)
    v = v_ref[blk, :]
    k = 1.0 - jnp.exp(g)
    e_b = jnp.exp(b)
    qe = (q * e_b).astype(BF16)
    c2 = [b[sec] - b[edge:edge + 1, :] for (_, _, sec, edge, _) in chunks]
    q2 = jnp.concatenate([q[ch[2]] * jnp.exp(c2[c]) for c, ch in enumerate(chunks)], axis=0)
    own = []
    for c, (_, fst, _, _, _) in enumerate(chunks):
        own += [c2[c], b[fst]] if reverse else [b[fst], c2[c]]
    kh = (k * jnp.exp(-jnp.concatenate(own, axis=0))).astype(BF16)
    tail = jnp.concatenate([b[last:last + 1, :] - b[rows] for (rows, _, _, _, last) in chunks], axis=0)
    ke = (k * jnp.exp(tail)).astype(BF16)
    lhs = jnp.concatenate([qe, q2.astype(BF16)], axis=0)

    row = lax.broadcasted_iota(jnp.int32, (tb, tb), 0)
    col = lax.broadcasted_iota(jnp.int32, (tb, tb), 1)
    valid = (row // L == col // L) & ((col >= row) if reverse else (col <= row))
    if reverse:
        use_near = (row % L < half) & (col % L < half)
    else:
        use_near = (row % L >= half) & (col % L >= half)

    for hd in range(N_HEADS):
        sl = slice(hd * HEAD_DIM, (hd + 1) * HEAD_DIM)
        s_all = _dot_nt(lhs[:, sl], kh[:, sl])
        far = s_all[0:tb]
        pieces = []
        for c, (_, fst, _, _, _) in enumerate(chunks):
            near_c = s_all[tb + c * half:tb + (c + 1) * half]
            pieces += [near_c, far[fst]] if reverse else [far[fst], near_c]
        near = jnp.concatenate(pieces, axis=0)
        sc = jnp.where(valid, jnp.where(use_near, near, far), 0.0).astype(BF16)
        vh = v[:, sl]
        o_intra = _dot(sc, vh)
        upd = [_dot_tn(ke[rows, sl], vh[rows]) for (rows, _, _, _, _) in chunks]
        st = st_ref[hd]
        o_inter = [None] * nchunk
        for c in (range(nchunk - 1, -1, -1) if reverse else range(nchunk)):
            rows, _, _, _, last = chunks[c]
            o_inter[c] = _dot(qe[rows, sl], st.astype(BF16))
            st = st * _as_column(e_b[last:last + 1, sl]) + upd[c]
        st_ref[hd] = st
        o_ref[blk, sl] = o_intra + jnp.concatenate(o_inter, axis=0)


def _hgrn_block_exact(q_ref, v_ref, g_ref, mall_ref, o_ref, st_ref, nchunk, L, nlev, reverse):
    row = lax.broadcasted_iota(jnp.int32, (L, L), 0)
    col = lax.broadcasted_iota(jnp.int32, (L, L), 1)
    rowc = lax.broadcasted_iota(jnp.int32, (L, N_HEADS * HEAD_DIM), 0)
    pair_masks, query_rows = [], []
    for j in range(nlev):
        h = L >> (j + 1)
        same = (row // (2 * h)) == (col // (2 * h))
        row_up = (row // h) % 2 == 1
        col_up = (col // h) % 2 == 1
        if reverse:
            pair_masks.append(same & jnp.logical_not(row_up) & col_up)
            query_rows.append((rowc // h) % 2 == 0)
        else:
            pair_masks.append(same & row_up & jnp.logical_not(col_up))
            query_rows.append((rowc // h) % 2 == 1)
    mall = mall_ref[...]
    for c in (range(nchunk - 1, -1, -1) if reverse else range(nchunk)):
        _hgrn_chunk_exact(q_ref, v_ref, g_ref, o_ref, st_ref, slice(c * L, (c + 1) * L),
                          mall, pair_masks, query_rows, L, nlev, reverse)


def _min_leaf_log_decay(g_ref, leaf):
    g = g_ref[...]
    return jnp.min(jnp.sum(g.reshape(g.shape[0] // leaf, leaf, g.shape[1]), axis=1))


def _hgrn_kernel(*refs, nsub, nchunk, L, nlev, n_narrow):
    (qf_ref, vf_ref, gf_ref, qb_ref, vb_ref, gb_ref, mallf_ref, mallb_ref,
     tribf_ref, tribb_ref, s0f_ref, s0b_ref) = refs[:12]
    narrow_in = refs[12:12 + n_narrow]
    of_ref, ob_ref, sff_ref, sfb_ref = refs[12 + n_narrow:16 + n_narrow]
    narrow_out = refs[16 + n_narrow:16 + 2 * n_narrow]
    stf_ref, stb_ref = refs[16 + 2 * n_narrow:]
    _narrow(narrow_in, narrow_out)
    i = pl.program_id(0)

    @pl.when(i == 0)
    def _():
        stf_ref[...] = s0f_ref[...]
        stb_ref[...] = s0b_ref[...]

    safe = jnp.minimum(_min_leaf_log_decay(gf_ref, L // 2),
                       _min_leaf_log_decay(gb_ref, L // 2)) >= -SAFE_LOG_DECAY

    @pl.when(safe)
    def _():
        for j in range(nsub):
            _hgrn_block_fast(qf_ref, vf_ref, gf_ref, tribf_ref, of_ref, stf_ref,
                             j * nchunk * L, nchunk, L, False)
            _hgrn_block_fast(qb_ref, vb_ref, gb_ref, tribb_ref, ob_ref, stb_ref,
                             (nsub - 1 - j) * nchunk * L, nchunk, L, True)

    @pl.when(jnp.logical_not(safe))
    def _():
        nc = nsub * nchunk
        _hgrn_block_exact(qf_ref, vf_ref, gf_ref, mallf_ref, of_ref, stf_ref, nc, L, nlev, False)
        _hgrn_block_exact(qb_ref, vb_ref, gb_ref, mallb_ref, ob_ref, stb_ref, nc, L, nlev, True)

    @pl.when(i == pl.num_programs(0) - 1)
    def _():
        sff_ref[...] = stf_ref[...]
        sfb_ref[...] = stb_ref[...]


def _hgrn_scan(q, v, g, s0_f, s0_b, tb, narrow=()):
    t, w = q.shape
    L = HGRN_CHUNK
    nblk = t // tb
    sb = min(tb, HGRN_SUB_BLOCK)
    consts = []
    for reverse in (False, True):
        mall_np, nlev = _hgrn_tables(L, reverse)
        consts.append((jnp.asarray(mall_np, dtype=BF16),
                       jnp.asarray(np.kron(np.eye(sb // L, dtype=np.float32), mall_np[0:L]), dtype=BF16)))
    (mall_f, trib_f), (mall_b, trib_b) = consts
    fwd = lambda i: (i, 0)
    bwd = lambda i: (nblk - 1 - i, 0)
    bwd_g = lambda i: (nblk - 1 - i, 1)
    blk = lambda m: pl.BlockSpec((tb, w), m)
    n_in, n_out, n_shape = _narrow_specs(list(narrow), nblk)
    return pl.pallas_call(
        functools.partial(_hgrn_kernel, nsub=tb // sb, nchunk=sb // L, L=L, nlev=nlev,
                          n_narrow=len(narrow)),
        grid=(nblk,),
        in_specs=[blk(fwd), blk(fwd), blk(fwd), blk(bwd), blk(bwd), blk(bwd_g),
                  _const_spec(mall_f.shape), _const_spec(mall_b.shape),
                  _const_spec(trib_f.shape), _const_spec(trib_b.shape),
                  _const_spec(s0_f.shape), _const_spec(s0_b.shape)] + n_in,
        out_specs=[blk(fwd), blk(bwd)] + [pl.BlockSpec(s0_f.shape, lambda i: (0, 0, 0))] * 2 + n_out,
        out_shape=[jax.ShapeDtypeStruct((t, w), F32)] * 2
        + [jax.ShapeDtypeStruct(s0_f.shape, F32)] * 2 + n_shape,
        scratch_shapes=[pltpu.VMEM(s0_f.shape, F32)] * 2,
        compiler_params=_cparams(),
        name="hgrn",
    )(q, v, g, q, v, g, mall_f, mall_b, trib_f, trib_b, s0_f, s0_b, *narrow)


def _merge_kernel(x_ref, er_ref, ec_ref, lng_ref, lnb_ref, mod_ref, wg_ref, bg_ref, of_ref, ob_ref,
                  ng_ref, four_ref, wfp_ref, whp_ref, wo_ref, bo_ref, w1_ref, w2_ref,
                  o_ref, w1b_ref, w2b_ref, *, tm):
    _narrow([w1_ref, w2_ref], [w1b_ref, w2b_ref])
    w4 = N_HEADS * HEAD_DIM
    d = x_ref.shape[1]
    sub = min(tm, SUB_ROWS)
    for r0 in range(0, tm, sub):
        rows = slice(r0, r0 + sub)
        xl, hl = _ln_in_modulated(x_ref, er_ref, ec_ref, lng_ref, lnb_ref, mod_ref, r0, sub)
        hb = hl.astype(BF16)
        og = _dot(hb, wg_ref[:, 0:w4]) + bg_ref[:, 0:w4]
        o = of_ref[rows, :] + ob_ref[rows, :]
        parts = []
        for hd in range(N_HEADS):
            oh = o[:, hd * HEAD_DIM:(hd + 1) * HEAD_DIM]
            ms = jnp.mean(oh * oh, axis=-1, keepdims=True)
            parts.append(oh * lax.rsqrt(ms + RMS_EPS))
        on = jnp.concatenate(parts, axis=1) * ng_ref[...]
        oh = (on * (og * _sigmoid(og))).astype(BF16)
        g_four = _sigmoid(_dot(hb, wg_ref[:, w4:w4 + d]) + bg_ref[:, w4:w4 + d])
        four = jnp.concatenate([four_ref[gi, rows, :] for gi in range(N_GROUPS)], axis=1)
        y = g_four * _dot(four.astype(BF16), wfp_ref[...])
        g_hgrn = _sigmoid(_dot(hb, wg_ref[:, w4 + d:w4 + 2 * d]) + bg_ref[:, w4 + d:w4 + 2 * d])
        y = y + g_hgrn * _dot(oh, whp_ref[...])
        mix = _dot(y.astype(BF16), wo_ref[...]) + bo_ref[...]
        o_ref[rows, :] = ALPHA * xl + mod_ref[2:3, :] * mix


def _merge(x, er, ec, lng, lnb, mod, wg, bg, o_f, o_b, ng, four, wfp, whp, wo, bo, w1, w2, tm):
    t, d = x.shape
    w4 = N_HEADS * HEAD_DIM
    nrow = tm // GRID_W
    row = lambda i: (i, 0)
    consts = [ec, lng, lnb, mod]
    consts2 = [wfp, whp, wo, bo]
    n_in, n_out, n_shape = _narrow_specs([w1, w2], t // tm)
    return pl.pallas_call(
        functools.partial(_merge_kernel, tm=tm),
        grid=(t // tm,),
        in_specs=[pl.BlockSpec((tm, d), row), pl.BlockSpec((nrow, d // 2), row)]
        + [_const_spec(a.shape) for a in consts]
        + [_const_cols(wg, w4 + 2 * d, 1), _const_cols(bg, w4 + 2 * d, 1)]
        + [pl.BlockSpec((tm, w4), row), pl.BlockSpec((tm, w4), row), _const_spec(ng.shape),
           pl.BlockSpec((N_GROUPS, tm, GROUP_DIM), lambda i: (0, i, 0))]
        + [_const_spec(a.shape) for a in consts2] + n_in,
        out_specs=[pl.BlockSpec((tm, d), row)] + n_out,
        out_shape=[jax.ShapeDtypeStruct((t, d), F32)] + n_shape,
        compiler_params=_cparams(),
        name="merge",
    )(x, er, *consts, wg, bg, o_f, o_b, ng, four, *consts2, w1, w2)


def _mlp_kernel(x_ref, mod_ref, ag_ref, ab_ref, w1_ref, b1_ref, w2_ref, b2_ref, pg_ref, pb_ref, o_ref,
                *, nsplit):
    dff = w1_ref.shape[1]
    cw = dff // nsplit
    tm = x_ref.shape[0]
    sub = min(tm, SUB_ROWS)
    for r0 in range(0, tm, sub):
        rows = slice(r0, r0 + sub)
        x1 = _layernorm(x_ref[rows, :], ag_ref[...], ab_ref[...])
        hb = (x1 * (1.0 + mod_ref[4:5, :]) + mod_ref[3:4, :]).astype(BF16)
        acc = jnp.zeros(x1.shape, F32)
        for c in range(nsplit):
            cs = slice(c * cw, (c + 1) * cw)
            a = jnp.maximum(_dot(hb, w1_ref[:, cs]) + b1_ref[:, cs], 0.0)
            acc = acc + _dot((a * a).astype(BF16), w2_ref[cs, :])
        m = acc + b2_ref[...]
        o_ref[rows, :] = _layernorm(ALPHA * x1 + mod_ref[5:6, :] * m, pg_ref[...], pb_ref[...])


def _mlp(z1, mod, ag, ab, w1, b1, w2, b2, pg, pb, tm):
    t, d = z1.shape
    row = lambda i: (i, 0)
    consts = [mod, ag, ab, w1, b1, w2, b2, pg, pb]
    return pl.pallas_call(
        functools.partial(_mlp_kernel, nsplit=4),
        grid=(t // tm,),
        in_specs=[pl.BlockSpec((tm, d), row)] + [_const_spec(a.shape) for a in consts],
        out_specs=pl.BlockSpec((tm, d), row),
        out_shape=jax.ShapeDtypeStruct((t, d), F32),
        compiler_params=_cparams(),
        name="mlp",
    )(z1, *consts)


def _pos_tables(rows, cols, dim):
    quarter = dim // 4
    omega = 1.0 / (POS_BASE ** (np.arange(quarter, dtype=np.float64) / quarter))
    r = np.arange(rows, dtype=np.float64)[:, None] * omega
    cc = np.arange(cols, dtype=np.float64)[:, None] * omega
    er = np.concatenate([np.sin(r), np.cos(r)], axis=-1)
    ec = np.concatenate([np.sin(cc), np.cos(cc)], axis=-1)
    return jnp.asarray(er.astype(np.float32)), jnp.asarray(ec.astype(np.float32))


def _dft_constants(t):
    n = RADIX
    kn = np.outer(np.arange(n), np.arange(n)).astype(np.float64)
    c = np.cos(2.0 * np.pi * kn / n)
    s = np.sin(2.0 * np.pi * kn / n)
    chan = np.concatenate([c, s], axis=1)
    stage1 = np.block([[c, -s], [-s, -c]])
    scale = 1.0 / np.sqrt(float(t) * GROUP_DIM)
    base2 = np.stack([c, s]) * scale
    beta = 2.0 * np.pi * kn / t
    twiddle = np.concatenate([np.cos(beta), np.sin(beta)], axis=1)
    as_f32 = lambda a: jnp.asarray(a.astype(np.float32))
    return as_f32(chan), as_f32(stage1), as_f32(base2), as_f32(twiddle)


def kernel(x, c, ctx, c_ctx, ln_in_g, ln_in_b, w_ada, b_ada, w_in, b_in, hgrn_lb_logits, hgrn_norm_g,
           w_four_proj, w_hgrn_proj, w_out, b_out, w_mlp1, b_mlp1, w_mlp2, b_mlp2, ln_post_g, ln_post_b):
    B, T, D = x.shape
    assert B == 1 and T == RADIX * RADIX and T % GRID_W == 0
    TC = ctx.shape[1]
    w4 = N_HEADS * HEAD_DIM
    row2 = lambda a: a.reshape(1, -1)

    mod_l, mod_c, w_in_b = _mod_vectors(c[0], c_ctx, w_ada[0], row2(b_ada[0]), w_in[0])
    mod_l = mod_l.reshape(6, D)
    mod_c = mod_c.reshape(6, D)

    er, ec = _pos_tables(T // GRID_W, GRID_W, D)
    lng, lnb = row2(ln_in_g), row2(ln_in_b)
    b_in2 = row2(b_in[0])
    assert w_in_b.shape[1] == 2 * 5 * w4
    w_a = w_g = w_in_b
    b_a = b_g = b_in2
    l0 = hgrn_lb_logits[:, 0, :].reshape(1, 2 * w4)
    l1 = hgrn_lb_logits[:, 1, :].reshape(1, 2 * w4)
    dft_chan, dft_s1, dft_base2, dft_tw = _dft_constants(T)

    zc = jnp.zeros((TC // GRID_W, D // 2), F32)
    _, _, qc, vc, gc = _inproj(ctx[0], zc, jnp.zeros_like(ec), lng, lnb, mod_c, w_a, b_a, dft_chan,
                               l0, l1, tm=TC)
    s_zero = jnp.zeros((N_HEADS, HEAD_DIM, HEAD_DIM), F32)
    _, _, s_f, s_b = _hgrn_scan(qc, vc, gc, s_zero, s_zero, tb=TC)

    xc, xs, q, v, g = _inproj(x[0], er, ec, lng, lnb, mod_l, w_a, b_a, dft_chan, l0, l1, tm=1024)
    four = _dft2(dft_base2, dft_tw, _dft1(dft_s1, xc, xs))
    o_f, o_b, _, _, w_fp_b, w_hp_b, w_out_b = _hgrn_scan(
        q, v, g, s_f, s_b, tb=512, narrow=(w_four_proj[0], w_hgrn_proj[0], w_out[0]))

    z1, w1_b, w2_b = _merge(x[0], er, ec, lng, lnb, mod_l, w_g, b_g, o_f, o_b, row2(hgrn_norm_g[0]),
                            four, w_fp_b, w_hp_b, w_out_b, row2(b_out[0]), w_mlp1[0], w_mlp2[0], tm=1024)
    out = _mlp(z1, mod_l, row2(ln_post_g[0, 0]), row2(ln_post_b[0, 0]), w1_b, row2(b_mlp1[0]),
               w2_b, row2(b_mlp2[0]), row2(ln_post_g[0, 1]), row2(ln_post_b[0, 1]), tm=1024)
    return out[None]
```

```python
import functools

import numpy as np
import jax
import jax.numpy as jnp
from jax import lax
from jax.experimental import pallas as pl
from jax.experimental.pallas import tpu as pltpu

F32 = jnp.float32
BF16 = jnp.bfloat16

GRID_W = 64
N_GROUPS = 4
GROUP_DIM = 128
N_HEADS = 4
HEAD_DIM = 128
POS_BASE = 10000.0
LN_EPS = 1e-5
RMS_EPS = 1e-6
DEPTH = 1
ALPHA = (2.0 * DEPTH) ** 0.25

RADIX = 128
HGRN_CHUNK = 64
HGRN_SUB_BLOCK = 256
SUB_ROWS = 256
SAFE_LOG_DECAY = 80.0
VMEM_LIMIT_BYTES = 56 * 1024 * 1024


def _cparams(n_axes=1):
    return pltpu.CompilerParams(dimension_semantics=("arbitrary",) * n_axes,
                                vmem_limit_bytes=VMEM_LIMIT_BYTES)


def _const_spec(shape):
    nd = len(shape)
    return pl.BlockSpec(shape, lambda *_: (0,) * nd, pipeline_mode=pl.Buffered(1))


def _const_cols(arr, width, j):
    return pl.BlockSpec((arr.shape[0], width), lambda *_: (0, j), pipeline_mode=pl.Buffered(1))


def _sigmoid(x):
    return 1.0 / (1.0 + jnp.exp(-x))


def _layernorm(x, g, b):
    mu = jnp.mean(x, axis=-1, keepdims=True)
    xc = x - mu
    var = jnp.mean(xc * xc, axis=-1, keepdims=True)
    return xc * lax.rsqrt(var + LN_EPS) * g + b


def _dot(a, b):
    return jnp.dot(a, b, preferred_element_type=F32)


def _dot_nt(a, b):
    return lax.dot_general(a, b, (((1,), (1,)), ((), ())), preferred_element_type=F32)


def _dot_tn(a, b):
    return lax.dot_general(a, b, (((0,), (0,)), ((), ())), preferred_element_type=F32)


def _narrow_specs(arrays, nsteps):
    chunk = lambda a: pl.BlockSpec((a.shape[0] // nsteps, a.shape[1]), lambda i, *_: (i, 0))
    return ([chunk(a) for a in arrays], [chunk(a) for a in arrays],
            [jax.ShapeDtypeStruct(a.shape, BF16) for a in arrays])


def _narrow(src_refs, dst_refs):
    for s_ref, d_ref in zip(src_refs, dst_refs):
        d_ref[...] = s_ref[...].astype(BF16)


def _mod_kernel(cl_ref, cx_ref, w_ref, b_ref, win_ref, ol_ref, ox_ref, winb_ref):
    w = w_ref[...]
    for c_ref, o_ref in ((cl_ref, ol_ref), (cx_ref, ox_ref)):
        cs = c_ref[...]
        s = cs * _sigmoid(cs)
        o_ref[...] = jnp.sum(s * w, axis=0, keepdims=True) + b_ref[...]
    _narrow([win_ref], [winb_ref])


def _mod_vectors(c_lat, c_ctx, w_ada, b_ada, w_in):
    d, n = w_ada.shape
    tn = 1536
    col = lambda j: (0, j)
    n_in, n_out, n_shape = _narrow_specs([w_in], n // tn)
    return pl.pallas_call(
        _mod_kernel,
        grid=(n // tn,),
        in_specs=[_const_spec((d, 1)), _const_spec((d, 1)),
                  pl.BlockSpec((d, tn), col), pl.BlockSpec((1, tn), col)] + n_in,
        out_specs=[pl.BlockSpec((1, tn), col)] * 2 + n_out,
        out_shape=[jax.ShapeDtypeStruct((1, n), F32)] * 2 + n_shape,
        compiler_params=_cparams(),
        name="mod",
    )(c_lat.reshape(d, 1), c_ctx.reshape(d, 1), w_ada, b_ada, w_in)


def _ln_in_modulated(x_ref, er_ref, ec_ref, lng_ref, lnb_ref, mod_ref, r0, nr):
    x = x_ref[r0:r0 + nr, :]
    half = x.shape[1] // 2
    nrow = nr // GRID_W
    e0 = r0 // GRID_W
    left = jnp.concatenate(
        [jnp.broadcast_to(er_ref[e0 + r:e0 + r + 1, :], (GRID_W, half)) for r in range(nrow)], axis=0)
    right = jnp.concatenate([ec_ref[...]] * nrow, axis=0)
    xp = jnp.concatenate([x[:, :half] + left, x[:, half:] + right], axis=1)
    xl = _layernorm(xp, lng_ref[...], lnb_ref[...])
    hl = xl * (1.0 + mod_ref[1:2, :]) + mod_ref[0:1, :]
    return xl, hl


def _inproj_kernel(x_ref, er_ref, ec_ref, lng_ref, lnb_ref, mod_ref, w_ref, b_ref,
                   l0_ref, l1_ref, u_ref, q_ref, v_ref, g_ref, *, tm):
    w4 = N_GROUPS * GROUP_DIM
    l0 = l0_ref[...]
    l1 = l1_ref[...]
    m = jnp.maximum(l0, l1)
    e0 = jnp.exp(l0 - m)
    lb = e0 / (e0 + jnp.exp(l1 - m))
    sub = min(tm, SUB_ROWS)
    for r0 in range(0, tm, sub):
        rows = slice(r0, r0 + sub)
        _, hl = _ln_in_modulated(x_ref, er_ref, ec_ref, lng_ref, lnb_ref, mod_ref, r0, sub)
        hb = hl.astype(BF16)
        fp = _dot(hb, w_ref[:, 3 * w4:5 * w4]) + b_ref[:, 3 * w4:5 * w4]
        g_ref[rows, :] = jnp.log(lb + (1.0 - lb) * _sigmoid(fp))
        qp = _dot(hb, w_ref[:, w4:2 * w4]) + b_ref[:, w4:2 * w4]
        q_ref[rows, :] = (qp * _sigmoid(qp)).astype(BF16)
        u = (_dot(hb, w_ref[:, 0:w4]) + b_ref[:, 0:w4]).astype(BF16)
        for gi in range(N_GROUPS):
            u_ref[gi, rows, :] = u[:, gi * GROUP_DIM:(gi + 1) * GROUP_DIM]
        v_ref[rows, :] = (_dot(hb, w_ref[:, 2 * w4:3 * w4]) + b_ref[:, 2 * w4:3 * w4]).astype(BF16)


def _inproj(x, er, ec, lng, lnb, mod, w, b, l0, l1, tm):
    t, d = x.shape
    w4 = N_GROUPS * GROUP_DIM
    nrow = tm // GRID_W
    row = lambda i: (i, 0)
    return pl.pallas_call(
        functools.partial(_inproj_kernel, tm=tm),
        grid=(t // tm,),
        in_specs=[pl.BlockSpec((tm, d), row),
                  pl.BlockSpec((nrow, d // 2), row),
                  _const_spec(ec.shape), _const_spec(lng.shape), _const_spec(lnb.shape),
                  _const_spec(mod.shape), _const_cols(w, 5 * w4, 0), _const_cols(b, 5 * w4, 0),
                  _const_spec(l0.shape), _const_spec(l1.shape)],
        out_specs=[pl.BlockSpec((N_GROUPS, tm, GROUP_DIM), lambda i: (0, i, 0))]
        + [pl.BlockSpec((tm, w4), row)] * 2 + [pl.BlockSpec((tm, 2 * w4), row)],
        out_shape=[jax.ShapeDtypeStruct((N_GROUPS, t, GROUP_DIM), BF16)]
        + [jax.ShapeDtypeStruct((t, w4), BF16)] * 2 + [jax.ShapeDtypeStruct((t, 2 * w4), F32)],
        compiler_params=_cparams(),
        name="inproj",
    )(x, er, ec, lng, lnb, mod, w, b, l0, l1)


DFT_BATCH = 16
F32_SUBLANES = 8


def _dft1_kernel(cs_ref, m_ref, u_ref, a_ref, su_ref, sa_ref):
    r, nb, w = u_ref.shape
    hs = F32_SUBLANES
    u = u_ref[...].astype(F32)
    for h in range(nb // hs):
        su_ref[h] = u[:, h * hs:(h + 1) * hs, :].reshape(r * hs, w)
    cols = [su_ref[j // hs, pl.ds(j % hs, r, stride=hs), :].astype(BF16) for j in range(nb)]
    pq = _dot(cs_ref[...].astype(BF16), jnp.concatenate(cols, axis=1))
    for j in range(nb):
        sa_ref[j // hs, pl.ds(j % hs, 2 * r, stride=hs), :] = pq[:, j * w:(j + 1) * w]
    mat = m_ref[...].astype(BF16)
    parts = []
    for h in range(nb // hs):
        rows_pq = jnp.concatenate([sa_ref[h, 0:r * hs, :], sa_ref[h, r * hs:2 * r * hs, :]], axis=1)
        parts.append(_dot(rows_pq.astype(BF16), mat))
    a_ref[0] = jnp.concatenate([p[:, :w].reshape(r, hs, w) for p in parts], axis=1).astype(BF16)
    a_ref[1] = jnp.concatenate([p[:, w:].reshape(r, hs, w) for p in parts], axis=1).astype(BF16)


def _dft1(cs, mat, u):
    ng, t, w = u.shape
    r = RADIX
    nb = 2 * DFT_BATCH
    nh = nb // F32_SUBLANES
    return pl.pallas_call(
        _dft1_kernel,
        grid=(ng, r // nb),
        in_specs=[_const_spec(cs.shape), _const_spec(mat.shape),
                  pl.BlockSpec((None, r, nb, w), lambda g, o: (g, 0, o, 0))],
        out_specs=pl.BlockSpec((2, r, nb, w), lambda g, o: (0, 0, o, g)),
        out_shape=jax.ShapeDtypeStruct((2, r, r, ng * w), BF16),
        scratch_shapes=[pltpu.VMEM((nh, r * F32_SUBLANES, w), F32),
                        pltpu.VMEM((nh, 2 * r * F32_SUBLANES, w), F32)],
        compiler_params=_cparams(2),
        name="dft1",
    )(cs, mat, u.reshape(ng, r, r, w))


def _dft2_kernel(cs_ref, tw_ref, a_ref, o_ref, s_ref):
    nb, r, w = a_ref.shape[1], a_ref.shape[2], a_ref.shape[3]
    hs = F32_SUBLANES
    cos_a, sin_a = cs_ref[0], cs_ref[1]
    for j in range(nb):
        cos_b, sin_b = tw_ref[j:j + 1, 0:r], tw_ref[j:j + 1, r:2 * r]
        gmat = jnp.concatenate([cos_a * cos_b - sin_a * sin_b, sin_a * cos_b + cos_a * sin_b], axis=1)
        a = jnp.concatenate([a_ref[0, j], a_ref[1, j]], axis=0)
        y = _dot(gmat.astype(BF16), a)
        for gi in range(N_GROUPS):
            s_ref[gi, j // hs, pl.ds(j % hs, r, stride=hs), :] = y[:, gi * GROUP_DIM:(gi + 1) * GROUP_DIM]
    for gi in range(N_GROUPS):
        o_ref[gi] = jnp.concatenate([s_ref[gi, h].reshape(r, hs, GROUP_DIM) for h in range(nb // hs)],
                                    axis=1).astype(BF16)


def _dft2(base, twiddle, a4):
    _, r, _, w = a4.shape
    nb = DFT_BATCH
    out = pl.pallas_call(
        _dft2_kernel,
        grid=(r // nb,),
        in_specs=[_const_spec(base.shape),
                  pl.BlockSpec((nb, 2 * r), lambda k: (k, 0)),
                  pl.BlockSpec((2, nb, r, w), lambda k: (0, k, 0, 0))],
        out_specs=pl.BlockSpec((N_GROUPS, r, None, nb, GROUP_DIM), lambda k: (0, 0, k, 0, 0)),
        out_shape=jax.ShapeDtypeStruct((N_GROUPS, r, r // nb, nb, GROUP_DIM), BF16),
        scratch_shapes=[pltpu.VMEM((N_GROUPS, nb // F32_SUBLANES, r * F32_SUBLANES, GROUP_DIM), F32)],
        compiler_params=_cparams(),
        name="dft2",
    )(base, twiddle, a4)
    return out.reshape(N_GROUPS, r * r, GROUP_DIM)


def _hgrn_tables(L, reverse):
    nlev = int(np.log2(L))
    idx = np.arange(L)
    t = idx[:, None]
    i = idx[None, :]
    blocks = [(i >= t) if reverse else (i <= t)]
    for j in range(nlev):
        h = L >> (j + 1)
        mid = (t // (2 * h)) * (2 * h) + h
        upper = t >= mid
        if reverse:
            blk = np.where(upper, (i >= mid) & (i < t), (i >= t) & (i < mid))
        else:
            blk = np.where(upper, (i >= mid) & (i <= t), (i > t) & (i < mid))
        blocks.append(blk)
    blocks.append((i < t) if reverse else (i > t))
    return np.concatenate(blocks, axis=0).astype(np.float32), nlev


def _as_column(row):
    n = row.shape[1]
    return jnp.broadcast_to(row, (n, n)).T


def _split_hi_lo(g):
    hi = g.astype(BF16)
    return hi, (g - hi.astype(F32)).astype(BF16)


def _hgrn_chunk_exact(q_ref, v_ref, g_ref, o_ref, st_ref, rows, mall, pair_masks, query_rows,
                      L, nlev, reverse):
    last = 0 if reverse else L - 1
    g = g_ref[rows, :]
    g_hi, g_lo = _split_hi_lo(g)
    ex = jnp.exp(_dot(mall, g_hi) + _dot(mall, g_lo))
    q = q_ref[rows, :].astype(F32)
    v = v_ref[rows, :]
    k = 1.0 - jnp.exp(g)
    e_cum = ex[0:L]
    qe = (q * e_cum).astype(BF16)
    ke = (k * ex[(nlev + 1) * L:(nlev + 2) * L]).astype(BF16)
    zs = [(jnp.where(query_rows[j], q, k) * ex[(j + 1) * L:(j + 2) * L]).astype(BF16)
          for j in range(nlev)]
    qk = q * k
    e_last = e_cum[last:last + 1, :]
    for hd in range(N_HEADS):
        sl = slice(hd * HEAD_DIM, (hd + 1) * HEAD_DIM)
        sc = jnp.zeros((L, L), F32)
        for j in range(nlev):
            zj = zs[j][:, sl]
            sc = jnp.where(pair_masks[j], _dot_nt(zj, zj), sc)
        st = st_ref[hd]
        vh = v[:, sl]
        o = _dot(sc.astype(BF16), vh) + _dot(qe[:, sl], st.astype(BF16))
        o = o + jnp.sum(qk[:, sl], axis=-1, keepdims=True) * vh.astype(F32)
        o_ref[rows, sl] = o
        st_ref[hd] = st * _as_column(e_last[:, sl]) + _dot_tn(ke[:, sl], vh)


def _hgrn_block_fast(q_ref, v_ref, g_ref, trib_ref, o_ref, st_ref, r0, nchunk, L, reverse):
    half = L // 2
    tb = nchunk * L
    blk = slice(r0, r0 + tb)
    chunks = []
    for c in range(nchunk):
        base = c * L
        if reverse:
            chunks.append((slice(base, base + L), slice(base + half, base + L),
                           slice(base, base + half), base + half, base))
        else:
            chunks.append((slice(base, base + L), slice(base, base + half),
                           slice(base + half, base + L), base + half - 1, base + L - 1))
    g = g_ref[blk, :]
    b = _dot(trib_ref[...], g.astype(BF16))
    q = q_ref[blk, :].astype(F32)
    v = v_ref[blk, :]
    k = 1.0 - jnp.exp(g)
    e_b = jnp.exp(b)
    qe = (q * e_b).astype(BF16)
    c2 = [b[sec] - b[edge:edge + 1, :] for (_, _, sec, edge, _) in chunks]
    q2 = jnp.concatenate([q[ch[2]] * jnp.exp(c2[c]) for c, ch in enumerate(chunks)], axis=0)
    own = []
    for c, (_, fst, _, _, _) in enumerate(chunks):
        own += [c2[c], b[fst]] if reverse else [b[fst], c2[c]]
    kh = (k * jnp.exp(-jnp.concatenate(own, axis=0))).astype(BF16)
    tail = jnp.concatenate([b[last:last + 1, :] - b[rows] for (rows, _, _, _, last) in chunks], axis=0)
    ke = (k * jnp.exp(tail)).astype(BF16)
    lhs = jnp.concatenate([qe, q2.astype(BF16)], axis=0)

    row = lax.broadcasted_iota(jnp.int32, (tb, tb), 0)
    col = lax.broadcasted_iota(jnp.int32, (tb, tb), 1)
    valid = (row // L == col // L) & ((col >= row) if reverse else (col <= row))
    if reverse:
        use_near = (row % L < half) & (col % L < half)
    else:
        use_near = (row % L >= half) & (col % L >= half)

    for hd in range(N_HEADS):
        sl = slice(hd * HEAD_DIM, (hd + 1) * HEAD_DIM)
        s_all = _dot_nt(lhs[:, sl], kh[:, sl])
        far = s_all[0:tb]
        pieces = []
        for c, (_, fst, _, _, _) in enumerate(chunks):
            near_c = s_all[tb + c * half:tb + (c + 1) * half]
            pieces += [near_c, far[fst]] if reverse else [far[fst], near_c]
        near = jnp.concatenate(pieces, axis=0)
        sc = jnp.where(valid, jnp.where(use_near, near, far), 0.0).astype(BF16)
        vh = v[:, sl]
        o_intra = _dot(sc, vh)
        upd = [_dot_tn(ke[rows, sl], vh[rows]) for (rows, _, _, _, _) in chunks]
        st = st_ref[hd]
        o_inter = [None] * nchunk
        for c in (range(nchunk - 1, -1, -1) if reverse else range(nchunk)):
            rows, _, _, _, last = chunks[c]
            o_inter[c] = _dot(qe[rows, sl], st.astype(BF16))
            st = st * _as_column(e_b[last:last + 1, sl]) + upd[c]
        st_ref[hd] = st
        o_ref[blk, sl] = o_intra + jnp.concatenate(o_inter, axis=0)


def _hgrn_block_exact(q_ref, v_ref, g_ref, mall_ref, o_ref, st_ref, nchunk, L, nlev, reverse):
    row = lax.broadcasted_iota(jnp.int32, (L, L), 0)
    col = lax.broadcasted_iota(jnp.int32, (L, L), 1)
    rowc = lax.broadcasted_iota(jnp.int32, (L, N_HEADS * HEAD_DIM), 0)
    pair_masks, query_rows = [], []
    for j in range(nlev):
        h = L >> (j + 1)
        same = (row // (2 * h)) == (col // (2 * h))
        row_up = (row // h) % 2 == 1
        col_up = (col // h) % 2 == 1
        if reverse:
            pair_masks.append(same & jnp.logical_not(row_up) & col_up)
            query_rows.append((rowc // h) % 2 == 0)
        else:
            pair_masks.append(same & row_up & jnp.logical_not(col_up))
            query_rows.append((rowc // h) % 2 == 1)
    mall = mall_ref[...]
    for c in (range(nchunk - 1, -1, -1) if reverse else range(nchunk)):
        _hgrn_chunk_exact(q_ref, v_ref, g_ref, o_ref, st_ref, slice(c * L, (c + 1) * L),
                          mall, pair_masks, query_rows, L, nlev, reverse)


def _min_leaf_log_decay(g_ref, leaf):
    g = g_ref[...]
    return jnp.min(jnp.sum(g.reshape(g.shape[0] // leaf, leaf, g.shape[1]), axis=1))


def _hgrn_kernel(*refs, nsub, nchunk, L, nlev, n_narrow):
    (qf_ref, vf_ref, gf_ref, qb_ref, vb_ref, gb_ref, mallf_ref, mallb_ref,
     tribf_ref, tribb_ref, s0f_ref, s0b_ref) = refs[:12]
    narrow_in = refs[12:12 + n_narrow]
    of_ref, ob_ref, sff_ref, sfb_ref = refs[12 + n_narrow:16 + n_narrow]
    narrow_out = refs[16 + n_narrow:16 + 2 * n_narrow]
    stf_ref, stb_ref = refs[16 + 2 * n_narrow:]
    _narrow(narrow_in, narrow_out)
    i = pl.program_id(0)

    @pl.when(i == 0)
    def _():
        stf_ref[...] = s0f_ref[...]
        stb_ref[...] = s0b_ref[...]

    safe = jnp.minimum(_min_leaf_log_decay(gf_ref, L // 2),
                       _min_leaf_log_decay(gb_ref, L // 2)) >= -SAFE_LOG_DECAY

    @pl.when(safe)
    def _():
        for j in range(nsub):
            _hgrn_block_fast(qf_ref, vf_ref, gf_ref, tribf_ref, of_ref, stf_ref,
                             j * nchunk * L, nchunk, L, False)
            _hgrn_block_fast(qb_ref, vb_ref, gb_ref, tribb_ref, ob_ref, stb_ref,
                             (nsub - 1 - j) * nchunk * L, nchunk, L, True)

    @pl.when(jnp.logical_not(safe))
    def _():
        nc = nsub * nchunk
        _hgrn_block_exact(qf_ref, vf_ref, gf_ref, mallf_ref, of_ref, stf_ref, nc, L, nlev, False)
        _hgrn_block_exact(qb_ref, vb_ref, gb_ref, mallb_ref, ob_ref, stb_ref, nc, L, nlev, True)

    @pl.when(i == pl.num_programs(0) - 1)
    def _():
        sff_ref[...] = stf_ref[...]
        sfb_ref[...] = stb_ref[...]


def _hgrn_scan(q, v, g, s0_f, s0_b, tb, narrow=()):
    t, w = q.shape
    L = HGRN_CHUNK
    nblk = t // tb
    sb = min(tb, HGRN_SUB_BLOCK)
    consts = []
    for reverse in (False, True):
        mall_np, nlev = _hgrn_tables(L, reverse)
        consts.append((jnp.asarray(mall_np, dtype=BF16),
                       jnp.asarray(np.kron(np.eye(sb // L, dtype=np.float32), mall_np[0:L]), dtype=BF16)))
    (mall_f, trib_f), (mall_b, trib_b) = consts
    fwd = lambda i: (i, 0)
    bwd = lambda i: (nblk - 1 - i, 0)
    bwd_g = lambda i: (nblk - 1 - i, 1)
    blk = lambda m: pl.BlockSpec((tb, w), m)
    n_in, n_out, n_shape = _narrow_specs(list(narrow), nblk)
    return pl.pallas_call(
        functools.partial(_hgrn_kernel, nsub=tb // sb, nchunk=sb // L, L=L, nlev=nlev,
                          n_narrow=len(narrow)),
        grid=(nblk,),
        in_specs=[blk(fwd), blk(fwd), blk(fwd), blk(bwd), blk(bwd), blk(bwd_g),
                  _const_spec(mall_f.shape), _const_spec(mall_b.shape),
                  _const_spec(trib_f.shape), _const_spec(trib_b.shape),
                  _const_spec(s0_f.shape), _const_spec(s0_b.shape)] + n_in,
        out_specs=[blk(fwd), blk(bwd)] + [pl.BlockSpec(s0_f.shape, lambda i: (0, 0, 0))] * 2 + n_out,
        out_shape=[jax.ShapeDtypeStruct((t, w), F32)] * 2
        + [jax.ShapeDtypeStruct(s0_f.shape, F32)] * 2 + n_shape,
        scratch_shapes=[pltpu.VMEM(s0_f.shape, F32)] * 2,
        compiler_params=_cparams(),
        name="hgrn",
    )(q, v, g, q, v, g, mall_f, mall_b, trib_f, trib_b, s0_f, s0_b, *narrow)


def _merge_kernel(x_ref, er_ref, ec_ref, lng_ref, lnb_ref, mod_ref, wg_ref, bg_ref, of_ref, ob_ref,
                  ng_ref, four_ref, wfp_ref, whp_ref, wo_ref, bo_ref, w1_ref, w2_ref,
                  o_ref, w1b_ref, w2b_ref, *, tm):
    _narrow([w1_ref, w2_ref], [w1b_ref, w2b_ref])
    w4 = N_HEADS * HEAD_DIM
    d = x_ref.shape[1]
    sub = min(tm, SUB_ROWS)
    for r0 in range(0, tm, sub):
        rows = slice(r0, r0 + sub)
        xl, hl = _ln_in_modulated(x_ref, er_ref, ec_ref, lng_ref, lnb_ref, mod_ref, r0, sub)
        hb = hl.astype(BF16)
        og = _dot(hb, wg_ref[:, 0:w4]) + bg_ref[:, 0:w4]
        o = of_ref[rows, :] + ob_ref[rows, :]
        parts = []
        for hd in range(N_HEADS):
            oh = o[:, hd * HEAD_DIM:(hd + 1) * HEAD_DIM]
            ms = jnp.mean(oh * oh, axis=-1, keepdims=True)
            parts.append(oh * lax.rsqrt(ms + RMS_EPS))
        on = jnp.concatenate(parts, axis=1) * ng_ref[...]
        oh = (on * (og * _sigmoid(og))).astype(BF16)
        g_four = _sigmoid(_dot(hb, wg_ref[:, w4:w4 + d]) + bg_ref[:, w4:w4 + d])
        four = jnp.concatenate([four_ref[gi, rows, :] for gi in range(N_GROUPS)], axis=1)
        y = g_four * _dot(four.astype(BF16), wfp_ref[...])
        g_hgrn = _sigmoid(_dot(hb, wg_ref[:, w4 + d:w4 + 2 * d]) + bg_ref[:, w4 + d:w4 + 2 * d])
        y = y + g_hgrn * _dot(oh, whp_ref[...])
        mix = _dot(y.astype(BF16), wo_ref[...]) + bo_ref[...]
        o_ref[rows, :] = ALPHA * xl + mod_ref[2:3, :] * mix


def _merge(x, er, ec, lng, lnb, mod, wg, bg, o_f, o_b, ng, four, wfp, whp, wo, bo, w1, w2, tm):
    t, d = x.shape
    w4 = N_HEADS * HEAD_DIM
    nrow = tm // GRID_W
    row = lambda i: (i, 0)
    consts = [ec, lng, lnb, mod]
    consts2 = [wfp, whp, wo, bo]
    n_in, n_out, n_shape = _narrow_specs([w1, w2], t // tm)
    return pl.pallas_call(
        functools.partial(_merge_kernel, tm=tm),
        grid=(t // tm,),
        in_specs=[pl.BlockSpec((tm, d), row), pl.BlockSpec((nrow, d // 2), row)]
        + [_const_spec(a.shape) for a in consts]
        + [_const_cols(wg, w4 + 2 * d, 1), _const_cols(bg, w4 + 2 * d, 1)]
        + [pl.BlockSpec((tm, w4), row), pl.BlockSpec((tm, w4), row), _const_spec(ng.shape),
           pl.BlockSpec((N_GROUPS, tm, GROUP_DIM), lambda i: (0, i, 0))]
        + [_const_spec(a.shape) for a in consts2] + n_in,
        out_specs=[pl.BlockSpec((tm, d), row)] + n_out,
        out_shape=[jax.ShapeDtypeStruct((t, d), F32)] + n_shape,
        compiler_params=_cparams(),
        name="merge",
    )(x, er, *consts, wg, bg, o_f, o_b, ng, four, *consts2, w1, w2)


def _mlp_kernel(x_ref, mod_ref, ag_ref, ab_ref, w1_ref, b1_ref, w2_ref, b2_ref, pg_ref, pb_ref, o_ref,
                *, nsplit):
    dff = w1_ref.shape[1]
    cw = dff // nsplit
    tm = x_ref.shape[0]
    sub = min(tm, SUB_ROWS)
    for r0 in range(0, tm, sub):
        rows = slice(r0, r0 + sub)
        x1 = _layernorm(x_ref[rows, :], ag_ref[...], ab_ref[...])
        hb = (x1 * (1.0 + mod_ref[4:5, :]) + mod_ref[3:4, :]).astype(BF16)
        acc = jnp.zeros(x1.shape, F32)
        for c in range(nsplit):
            cs = slice(c * cw, (c + 1) * cw)
            a = jnp.maximum(_dot(hb, w1_ref[:, cs]) + b1_ref[:, cs], 0.0)
            acc = acc + _dot((a * a).astype(BF16), w2_ref[cs, :])
        m = acc + b2_ref[...]
        o_ref[rows, :] = _layernorm(ALPHA * x1 + mod_ref[5:6, :] * m, pg_ref[...], pb_ref[...])


def _mlp(z1, mod, ag, ab, w1, b1, w2, b2, pg, pb, tm):
    t, d = z1.shape
    row = lambda i: (i, 0)
    consts = [mod, ag, ab, w1, b1, w2, b2, pg, pb]
    return pl.pallas_call(
        functools.partial(_mlp_kernel, nsplit=4),
        grid=(t // tm,),
        in_specs=[pl.BlockSpec((tm, d), row)] + [_const_spec(a.shape) for a in consts],
        out_specs=pl.BlockSpec((tm, d), row),
        out_shape=jax.ShapeDtypeStruct((t, d), F32),
        compiler_params=_cparams(),
        name="mlp",
    )(z1, *consts)


def _pos_tables(rows, cols, dim):
    quarter = dim // 4
    omega = 1.0 / (POS_BASE ** (np.arange(quarter, dtype=np.float64) / quarter))
    r = np.arange(rows, dtype=np.float64)[:, None] * omega
    cc = np.arange(cols, dtype=np.float64)[:, None] * omega
    er = np.concatenate([np.sin(r), np.cos(r)], axis=-1)
    ec = np.concatenate([np.sin(cc), np.cos(cc)], axis=-1)
    return jnp.asarray(er.astype(np.float32)), jnp.asarray(ec.astype(np.float32))


def _dft_constants(t):
    n = RADIX
    kn = np.outer(np.arange(n), np.arange(n)).astype(np.float64)
    c = np.cos(2.0 * np.pi * kn / n)
    s = np.sin(2.0 * np.pi * kn / n)
    chan = np.concatenate([c, s], axis=0)
    stage1 = np.block([[c, -s], [-s, -c]])
    scale = 1.0 / np.sqrt(float(t) * GROUP_DIM)
    base2 = np.stack([c, s]) * scale
    beta = 2.0 * np.pi * kn / t
    twiddle = np.concatenate([np.cos(beta), np.sin(beta)], axis=1)
    as_f32 = lambda a: jnp.asarray(a.astype(np.float32))
    return as_f32(chan), as_f32(stage1), as_f32(base2), as_f32(twiddle)


def kernel(x, c, ctx, c_ctx, ln_in_g, ln_in_b, w_ada, b_ada, w_in, b_in, hgrn_lb_logits, hgrn_norm_g,
           w_four_proj, w_hgrn_proj, w_out, b_out, w_mlp1, b_mlp1, w_mlp2, b_mlp2, ln_post_g, ln_post_b):
    B, T, D = x.shape
    assert B == 1 and T == RADIX * RADIX and T % GRID_W == 0
    TC = ctx.shape[1]
    w4 = N_HEADS * HEAD_DIM
    row2 = lambda a: a.reshape(1, -1)

    mod_l, mod_c, w_in_b = _mod_vectors(c[0], c_ctx, w_ada[0], row2(b_ada[0]), w_in[0])
    mod_l = mod_l.reshape(6, D)
    mod_c = mod_c.reshape(6, D)

    er, ec = _pos_tables(T // GRID_W, GRID_W, D)
    lng, lnb = row2(ln_in_g), row2(ln_in_b)
    b_in2 = row2(b_in[0])
    assert w_in_b.shape[1] == 2 * 5 * w4
    w_a = w_g = w_in_b
    b_a = b_g = b_in2
    l0 = hgrn_lb_logits[:, 0, :].reshape(1, 2 * w4)
    l1 = hgrn_lb_logits[:, 1, :].reshape(1, 2 * w4)
    dft_chan, dft_s1, dft_base2, dft_tw = _dft_constants(T)

    zc = jnp.zeros((TC // GRID_W, D // 2), F32)
    _, qc, vc, gc = _inproj(ctx[0], zc, jnp.zeros_like(ec), lng, lnb, mod_c, w_a, b_a, l0, l1, tm=TC)
    s_zero = jnp.zeros((N_HEADS, HEAD_DIM, HEAD_DIM), F32)
    _, _, s_f, s_b = _hgrn_scan(qc, vc, gc, s_zero, s_zero, tb=TC)

    u, q, v, g = _inproj(x[0], er, ec, lng, lnb, mod_l, w_a, b_a, l0, l1, tm=1024)
    four = _dft2(dft_base2, dft_tw, _dft1(dft_chan, dft_s1, u))
    o_f, o_b, _, _, w_fp_b, w_hp_b, w_out_b = _hgrn_scan(
        q, v, g, s_f, s_b, tb=512, narrow=(w_four_proj[0], w_hgrn_proj[0], w_out[0]))

    z1, w1_b, w2_b = _merge(x[0], er, ec, lng, lnb, mod_l, w_g, b_g, o_f, o_b, row2(hgrn_norm_g[0]),
                            four, w_fp_b, w_hp_b, w_out_b, row2(b_out[0]), w_mlp1[0], w_mlp2[0], tm=1024)
    out = _mlp(z1, mod_l, row2(ln_post_g[0, 0]), row2(ln_post_b[0, 0]), w1_b, row2(b_mlp1[0]),
               w2_b, row2(b_mlp2[0]), row2(ln_post_g[0, 1]), row2(ln_post_b[0, 1]), tm=1024)
    return out[None]
```

```python
import functools

import numpy as np
import jax
import jax.numpy as jnp
from jax import lax
from jax.experimental import pallas as pl
from jax.experimental.pallas import tpu as pltpu

F32 = jnp.float32
BF16 = jnp.bfloat16

GRID_W = 64
N_GROUPS = 4
GROUP_DIM = 128
N_HEADS = 4
HEAD_DIM = 128
POS_BASE = 10000.0
LN_EPS = 1e-5
RMS_EPS = 1e-6
DEPTH = 1
ALPHA = (2.0 * DEPTH) ** 0.25

RADIX = 128
HGRN_CHUNK = 64
HGRN_SUB_BLOCK = 256
HGRN_BLOCK = 512
ROW_TILE = 1024
SUB_ROWS = 256
MOD_COL_TILE = 1536
SAFE_LOG_DECAY = 80.0
VMEM_LIMIT_BYTES = 56 * 1024 * 1024


def _cparams(n_axes=1):
    return pltpu.CompilerParams(dimension_semantics=("arbitrary",) * n_axes,
                                vmem_limit_bytes=VMEM_LIMIT_BYTES)


def _const_spec(shape):
    nd = len(shape)
    return pl.BlockSpec(shape, lambda *_: (0,) * nd, pipeline_mode=pl.Buffered(1))


def _const_cols(arr, width, j):
    return pl.BlockSpec((arr.shape[0], width), lambda *_: (0, j), pipeline_mode=pl.Buffered(1))


def _sigmoid(x):
    return 1.0 / (1.0 + jnp.exp(-x))


def _layernorm(x, g, b):
    mu = jnp.mean(x, axis=-1, keepdims=True)
    xc = x - mu
    var = jnp.mean(xc * xc, axis=-1, keepdims=True)
    return xc * lax.rsqrt(var + LN_EPS) * g + b


def _dot(a, b):
    return jnp.dot(a, b, preferred_element_type=F32)


def _dot_nt(a, b):
    return lax.dot_general(a, b, (((1,), (1,)), ((), ())), preferred_element_type=F32)


def _dot_tn(a, b):
    return lax.dot_general(a, b, (((0,), (0,)), ((), ())), preferred_element_type=F32)


def _narrow_specs(arrays, nsteps):
    chunk = lambda a: pl.BlockSpec((a.shape[0] // nsteps, a.shape[1]), lambda i, *_: (i, 0))
    return ([chunk(a) for a in arrays], [chunk(a) for a in arrays],
            [jax.ShapeDtypeStruct(a.shape, BF16) for a in arrays])


def _narrow(src_refs, dst_refs):
    for s_ref, d_ref in zip(src_refs, dst_refs):
        d_ref[...] = s_ref[...].astype(BF16)


def _mod_kernel(cl_ref, cx_ref, w_ref, b_ref, win_ref, ol_ref, ox_ref, winb_ref):
    w = w_ref[...]
    for c_ref, o_ref in ((cl_ref, ol_ref), (cx_ref, ox_ref)):
        cs = c_ref[...]
        s = cs * _sigmoid(cs)
        o_ref[...] = jnp.sum(s * w, axis=0, keepdims=True) + b_ref[...]
    _narrow([win_ref], [winb_ref])


def _mod_vectors(c_lat, c_ctx, w_ada, b_ada, w_in):
    d, n = w_ada.shape
    tn = MOD_COL_TILE
    col = lambda j: (0, j)
    n_in, n_out, n_shape = _narrow_specs([w_in], n // tn)
    return pl.pallas_call(
        _mod_kernel,
        grid=(n // tn,),
        in_specs=[_const_spec((d, 1)), _const_spec((d, 1)),
                  pl.BlockSpec((d, tn), col), pl.BlockSpec((1, tn), col)] + n_in,
        out_specs=[pl.BlockSpec((1, tn), col)] * 2 + n_out,
        out_shape=[jax.ShapeDtypeStruct((1, n), F32)] * 2 + n_shape,
        compiler_params=_cparams(),
        name="mod",
    )(c_lat.reshape(d, 1), c_ctx.reshape(d, 1), w_ada, b_ada, w_in)


def _ln_in_modulated(x_ref, er_ref, ec_ref, lng_ref, lnb_ref, mod_ref, r0, nr):
    x = x_ref[r0:r0 + nr, :]
    half = x.shape[1] // 2
    nrow = nr // GRID_W
    e0 = r0 // GRID_W
    left = jnp.concatenate(
        [jnp.broadcast_to(er_ref[e0 + r:e0 + r + 1, :], (GRID_W, half)) for r in range(nrow)], axis=0)
    right = jnp.concatenate([ec_ref[...]] * nrow, axis=0)
    xp = jnp.concatenate([x[:, :half] + left, x[:, half:] + right], axis=1)
    xl = _layernorm(xp, lng_ref[...], lnb_ref[...])
    hl = xl * (1.0 + mod_ref[1:2, :]) + mod_ref[0:1, :]
    return xl, hl


def _inproj_kernel(x_ref, er_ref, ec_ref, lng_ref, lnb_ref, mod_ref, w_ref, b_ref,
                   l0_ref, l1_ref, u_ref, q_ref, v_ref, g_ref, *, tm):
    w4 = N_GROUPS * GROUP_DIM
    l0 = l0_ref[...]
    l1 = l1_ref[...]
    m = jnp.maximum(l0, l1)
    e0 = jnp.exp(l0 - m)
    lb = e0 / (e0 + jnp.exp(l1 - m))
    sub = min(tm, SUB_ROWS)
    for r0 in range(0, tm, sub):
        rows = slice(r0, r0 + sub)
        _, hl = _ln_in_modulated(x_ref, er_ref, ec_ref, lng_ref, lnb_ref, mod_ref, r0, sub)
        hb = hl.astype(BF16)
        fp = _dot(hb, w_ref[:, 3 * w4:5 * w4]) + b_ref[:, 3 * w4:5 * w4]
        g_ref[rows, :] = jnp.log(lb + (1.0 - lb) * _sigmoid(fp))
        qp = _dot(hb, w_ref[:, w4:2 * w4]) + b_ref[:, w4:2 * w4]
        q_ref[rows, :] = (qp * _sigmoid(qp)).astype(BF16)
        u = (_dot(hb, w_ref[:, 0:w4]) + b_ref[:, 0:w4]).astype(BF16)
        for gi in range(N_GROUPS):
            u_ref[gi, rows, :] = u[:, gi * GROUP_DIM:(gi + 1) * GROUP_DIM]
        v_ref[rows, :] = (_dot(hb, w_ref[:, 2 * w4:3 * w4]) + b_ref[:, 2 * w4:3 * w4]).astype(BF16)


def _inproj(x, er, ec, lng, lnb, mod, w, b, l0, l1, tm):
    t, d = x.shape
    w4 = N_GROUPS * GROUP_DIM
    nrow = tm // GRID_W
    row = lambda i: (i, 0)
    return pl.pallas_call(
        functools.partial(_inproj_kernel, tm=tm),
        grid=(t // tm,),
        in_specs=[pl.BlockSpec((tm, d), row),
                  pl.BlockSpec((nrow, d // 2), row),
                  _const_spec(ec.shape), _const_spec(lng.shape), _const_spec(lnb.shape),
                  _const_spec(mod.shape), _const_cols(w, 5 * w4, 0), _const_cols(b, 5 * w4, 0),
                  _const_spec(l0.shape), _const_spec(l1.shape)],
        out_specs=[pl.BlockSpec((N_GROUPS, tm, GROUP_DIM), lambda i: (0, i, 0))]
        + [pl.BlockSpec((tm, w4), row)] * 2 + [pl.BlockSpec((tm, 2 * w4), row)],
        out_shape=[jax.ShapeDtypeStruct((N_GROUPS, t, GROUP_DIM), BF16)]
        + [jax.ShapeDtypeStruct((t, w4), BF16)] * 2 + [jax.ShapeDtypeStruct((t, 2 * w4), F32)],
        compiler_params=_cparams(),
        name="inproj",
    )(x, er, ec, lng, lnb, mod, w, b, l0, l1)


DFT_BATCH = 16
F32_SUBLANES = 8


def _dft1_kernel(cs_ref, m_ref, u_ref, a_ref, su_ref, sa_ref):
    r, nb, w = u_ref.shape
    hs = F32_SUBLANES
    u = u_ref[...].astype(F32)
    for h in range(nb // hs):
        su_ref[h] = u[:, h * hs:(h + 1) * hs, :].reshape(r * hs, w)
    cols = [su_ref[j // hs, pl.ds(j % hs, r, stride=hs), :].astype(BF16) for j in range(nb)]
    pq = _dot(cs_ref[...].astype(BF16), jnp.concatenate(cols, axis=1))
    for j in range(nb):
        sa_ref[j // hs, pl.ds(j % hs, 2 * r, stride=hs), :] = pq[:, j * w:(j + 1) * w]
    mat = m_ref[...].astype(BF16)
    parts = []
    for h in range(nb // hs):
        rows_pq = jnp.concatenate([sa_ref[h, 0:r * hs, :], sa_ref[h, r * hs:2 * r * hs, :]], axis=1)
        parts.append(_dot(rows_pq.astype(BF16), mat))
    a_ref[0] = jnp.concatenate([p[:, :w].reshape(r, hs, w) for p in parts], axis=1).astype(BF16)
    a_ref[1] = jnp.concatenate([p[:, w:].reshape(r, hs, w) for p in parts], axis=1).astype(BF16)


def _dft1(cs, mat, u):
    ng, t, w = u.shape
    r = RADIX
    nb = 2 * DFT_BATCH
    nh = nb // F32_SUBLANES
    return pl.pallas_call(
        _dft1_kernel,
        grid=(ng, r // nb),
        in_specs=[_const_spec(cs.shape), _const_spec(mat.shape),
                  pl.BlockSpec((None, r, nb, w), lambda g, o: (g, 0, o, 0))],
        out_specs=pl.BlockSpec((2, r, nb, w), lambda g, o: (0, 0, o, g)),
        out_shape=jax.ShapeDtypeStruct((2, r, r, ng * w), BF16),
        scratch_shapes=[pltpu.VMEM((nh, r * F32_SUBLANES, w), F32),
                        pltpu.VMEM((nh, 2 * r * F32_SUBLANES, w), F32)],
        compiler_params=_cparams(2),
        name="dft1",
    )(cs, mat, u.reshape(ng, r, r, w))


def _dft2_kernel(cs_ref, tw_ref, a_ref, o_ref, s_ref):
    nb, r, w = a_ref.shape[1], a_ref.shape[2], a_ref.shape[3]
    hs = F32_SUBLANES
    cos_a, sin_a = cs_ref[0], cs_ref[1]
    for j in range(nb):
        cos_b, sin_b = tw_ref[j:j + 1, 0:r], tw_ref[j:j + 1, r:2 * r]
        gmat = jnp.concatenate([cos_a * cos_b - sin_a * sin_b, sin_a * cos_b + cos_a * sin_b], axis=1)
        a = jnp.concatenate([a_ref[0, j], a_ref[1, j]], axis=0)
        y = _dot(gmat.astype(BF16), a)
        for gi in range(N_GROUPS):
            s_ref[gi, j // hs, pl.ds(j % hs, r, stride=hs), :] = y[:, gi * GROUP_DIM:(gi + 1) * GROUP_DIM]
    for gi in range(N_GROUPS):
        o_ref[gi] = jnp.concatenate([s_ref[gi, h].reshape(r, hs, GROUP_DIM) for h in range(nb // hs)],
                                    axis=1).astype(BF16)


def _dft2(base, twiddle, a4):
    _, r, _, w = a4.shape
    nb = DFT_BATCH
    out = pl.pallas_call(
        _dft2_kernel,
        grid=(r // nb,),
        in_specs=[_const_spec(base.shape),
                  pl.BlockSpec((nb, 2 * r), lambda k: (k, 0)),
                  pl.BlockSpec((2, nb, r, w), lambda k: (0, k, 0, 0))],
        out_specs=pl.BlockSpec((N_GROUPS, r, None, nb, GROUP_DIM), lambda k: (0, 0, k, 0, 0)),
        out_shape=jax.ShapeDtypeStruct((N_GROUPS, r, r // nb, nb, GROUP_DIM), BF16),
        scratch_shapes=[pltpu.VMEM((N_GROUPS, nb // F32_SUBLANES, r * F32_SUBLANES, GROUP_DIM), F32)],
        compiler_params=_cparams(),
        name="dft2",
    )(base, twiddle, a4)
    return out.reshape(N_GROUPS, r * r, GROUP_DIM)


def _hgrn_tables(L, reverse):
    nlev = int(np.log2(L))
    idx = np.arange(L)
    t = idx[:, None]
    i = idx[None, :]
    blocks = [(i >= t) if reverse else (i <= t)]
    for j in range(nlev):
        h = L >> (j + 1)
        mid = (t // (2 * h)) * (2 * h) + h
        upper = t >= mid
        if reverse:
            blk = np.where(upper, (i >= mid) & (i < t), (i >= t) & (i < mid))
        else:
            blk = np.where(upper, (i >= mid) & (i <= t), (i > t) & (i < mid))
        blocks.append(blk)
    blocks.append((i < t) if reverse else (i > t))
    return np.concatenate(blocks, axis=0).astype(np.float32), nlev


def _as_column(row):
    n = row.shape[1]
    return jnp.broadcast_to(row, (n, n)).T


def _split_hi_lo(g):
    hi = g.astype(BF16)
    return hi, (g - hi.astype(F32)).astype(BF16)


def _hgrn_chunk_exact(q_ref, v_ref, g_ref, o_ref, st_ref, rows, mall, pair_masks, query_rows,
                      L, nlev, reverse):
    last = 0 if reverse else L - 1
    g = g_ref[rows, :]
    g_hi, g_lo = _split_hi_lo(g)
    ex = jnp.exp(_dot(mall, g_hi) + _dot(mall, g_lo))
    q = q_ref[rows, :].astype(F32)
    v = v_ref[rows, :]
    k = 1.0 - jnp.exp(g)
    e_cum = ex[0:L]
    qe = (q * e_cum).astype(BF16)
    ke = (k * ex[(nlev + 1) * L:(nlev + 2) * L]).astype(BF16)
    zs = [(jnp.where(query_rows[j], q, k) * ex[(j + 1) * L:(j + 2) * L]).astype(BF16)
          for j in range(nlev)]
    qk = q * k
    e_last = e_cum[last:last + 1, :]
    for hd in range(N_HEADS):
        sl = slice(hd * HEAD_DIM, (hd + 1) * HEAD_DIM)
        sc = jnp.zeros((L, L), F32)
        for j in range(nlev):
            zj = zs[j][:, sl]
            sc = jnp.where(pair_masks[j], _dot_nt(zj, zj), sc)
        st = st_ref[hd]
        vh = v[:, sl]
        o = _dot(sc.astype(BF16), vh) + _dot(qe[:, sl], st.astype(BF16))
        o = o + jnp.sum(qk[:, sl], axis=-1, keepdims=True) * vh.astype(F32)
        o_ref[rows, sl] = o
        st_ref[hd] = st * _as_column(e_last[:, sl]) + _dot_tn(ke[:, sl], vh)


def _hgrn_block_fast(q_ref, v_ref, g_ref, trib_ref, o_ref, st_ref, r0, nchunk, L, reverse):
    half = L // 2
    tb = nchunk * L
    blk = slice(r0, r0 + tb)
    chunks = []
    for c in range(nchunk):
        base = c * L
        if reverse:
            chunks.append((slice(base, base + L), slice(base + half, base + L),
                           slice(base, base + half), base + half, base))
        else:
            chunks.append((slice(base, base + L), slice(base, base + half),
                           slice(base + half, base + L), base + half - 1, base + L - 1))
    g = g_ref[blk, :]
    b = _dot(trib_ref[...], g.astype(BF16))
    q = q_ref[blk, :].astype(F32)
    v = v_ref[blk, :]
    k = 1.0 - jnp.exp(g)
    e_b = jnp.exp(b)
    qe = (q * e_b).astype(BF16)
    c2 = [b[sec] - b[edge:edge + 1, :] for (_, _, sec, edge, _) in chunks]
    q2 = jnp.concatenate([q[ch[2]] * jnp.exp(c2[c]) for c, ch in enumerate(chunks)], axis=0)
    own = []
    for c, (_, fst, _, _, _) in enumerate(chunks):
        own += [c2[c], b[fst]] if reverse else [b[fst], c2[c]]
    kh = (k * jnp.exp(-jnp.concatenate(own, axis=0))).astype(BF16)
    tail = jnp.concatenate([b[last:last + 1, :] - b[rows] for (rows, _, _, _, last) in chunks], axis=0)
    ke = (k * jnp.exp(tail)).astype(BF16)
    lhs = jnp.concatenate([qe, q2.astype(BF16)], axis=0)

    row = lax.broadcasted_iota(jnp.int32, (tb, tb), 0)
    col = lax.broadcasted_iota(jnp.int32, (tb, tb), 1)
    valid = (row // L == col // L) & ((col >= row) if reverse else (col <= row))
    if reverse:
        use_near = (row % L < half) & (col % L < half)
    else:
        use_near = (row % L >= half) & (col % L >= half)

    for hd in range(N_HEADS):
        sl = slice(hd * HEAD_DIM, (hd + 1) * HEAD_DIM)
        s_all = _dot_nt(lhs[:, sl], kh[:, sl])
        far = s_all[0:tb]
        pieces = []
        for c, (_, fst, _, _, _) in enumerate(chunks):
            near_c = s_all[tb + c * half:tb + (c + 1) * half]
            pieces += [near_c, far[fst]] if reverse else [far[fst], near_c]
        near = jnp.concatenate(pieces, axis=0)
        sc = jnp.where(valid, jnp.where(use_near, near, far), 0.0).astype(BF16)
        vh = v[:, sl]
        o_intra = _dot(sc, vh)
        upd = [_dot_tn(ke[rows, sl], vh[rows]) for (rows, _, _, _, _) in chunks]
        st = st_ref[hd]
        o_inter = [None] * nchunk
        for c in (range(nchunk - 1, -1, -1) if reverse else range(nchunk)):
            rows, _, _, _, last = chunks[c]
            o_inter[c] = _dot(qe[rows, sl], st.astype(BF16))
            st = st * _as_column(e_b[last:last + 1, sl]) + upd[c]
        st_ref[hd] = st
        o_ref[blk, sl] = o_intra + jnp.concatenate(o_inter, axis=0)


def _hgrn_block_exact(q_ref, v_ref, g_ref, mall_ref, o_ref, st_ref, nchunk, L, nlev, reverse):
    row = lax.broadcasted_iota(jnp.int32, (L, L), 0)
    col = lax.broadcasted_iota(jnp.int32, (L, L), 1)
    rowc = lax.broadcasted_iota(jnp.int32, (L, N_HEADS * HEAD_DIM), 0)
    pair_masks, query_rows = [], []
    for j in range(nlev):
        h = L >> (j + 1)
        same = (row // (2 * h)) == (col // (2 * h))
        row_up = (row // h) % 2 == 1
        col_up = (col // h) % 2 == 1
        if reverse:
            pair_masks.append(same & jnp.logical_not(row_up) & col_up)
            query_rows.append((rowc // h) % 2 == 0)
        else:
            pair_masks.append(same & row_up & jnp.logical_not(col_up))
            query_rows.append((rowc // h) % 2 == 1)
    mall = mall_ref[...]
    for c in (range(nchunk - 1, -1, -1) if reverse else range(nchunk)):
        _hgrn_chunk_exact(q_ref, v_ref, g_ref, o_ref, st_ref, slice(c * L, (c + 1) * L),
                          mall, pair_masks, query_rows, L, nlev, reverse)


def _min_leaf_log_decay(g_ref, leaf):
    g = g_ref[...]
    return jnp.min(jnp.sum(g.reshape(g.shape[0] // leaf, leaf, g.shape[1]), axis=1))


def _hgrn_kernel(*refs, nsub, nchunk, L, nlev, n_narrow):
    (qf_ref, vf_ref, gf_ref, qb_ref, vb_ref, gb_ref, mallf_ref, mallb_ref,
     tribf_ref, tribb_ref, s0f_ref, s0b_ref) = refs[:12]
    narrow_in = refs[12:12 + n_narrow]
    of_ref, ob_ref, sff_ref, sfb_ref = refs[12 + n_narrow:16 + n_narrow]
    narrow_out = refs[16 + n_narrow:16 + 2 * n_narrow]
    stf_ref, stb_ref = refs[16 + 2 * n_narrow:]
    _narrow(narrow_in, narrow_out)
    i = pl.program_id(0)

    @pl.when(i == 0)
    def _():
        stf_ref[...] = s0f_ref[...]
        stb_ref[...] = s0b_ref[...]

    safe = jnp.minimum(_min_leaf_log_decay(gf_ref, L // 2),
                       _min_leaf_log_decay(gb_ref, L // 2)) >= -SAFE_LOG_DECAY

    @pl.when(safe)
    def _():
        for j in range(nsub):
            _hgrn_block_fast(qf_ref, vf_ref, gf_ref, tribf_ref, of_ref, stf_ref,
                             j * nchunk * L, nchunk, L, False)
            _hgrn_block_fast(qb_ref, vb_ref, gb_ref, tribb_ref, ob_ref, stb_ref,
                             (nsub - 1 - j) * nchunk * L, nchunk, L, True)

    @pl.when(jnp.logical_not(safe))
    def _():
        nc = nsub * nchunk
        _hgrn_block_exact(qf_ref, vf_ref, gf_ref, mallf_ref, of_ref, stf_ref, nc, L, nlev, False)
        _hgrn_block_exact(qb_ref, vb_ref, gb_ref, mallb_ref, ob_ref, stb_ref, nc, L, nlev, True)

    @pl.when(i == pl.num_programs(0) - 1)
    def _():
        sff_ref[...] = stf_ref[...]
        sfb_ref[...] = stb_ref[...]


def _hgrn_scan(q, v, g, s0_f, s0_b, tb, narrow=()):
    t, w = q.shape
    L = HGRN_CHUNK
    nblk = t // tb
    sb = min(tb, HGRN_SUB_BLOCK)
    consts = []
    for reverse in (False, True):
        mall_np, nlev = _hgrn_tables(L, reverse)
        consts.append((jnp.asarray(mall_np, dtype=BF16),
                       jnp.asarray(np.kron(np.eye(sb // L, dtype=np.float32), mall_np[0:L]), dtype=BF16)))
    (mall_f, trib_f), (mall_b, trib_b) = consts
    fwd = lambda i: (i, 0)
    bwd = lambda i: (nblk - 1 - i, 0)
    bwd_g = lambda i: (nblk - 1 - i, 1)
    blk = lambda m: pl.BlockSpec((tb, w), m)
    n_in, n_out, n_shape = _narrow_specs(list(narrow), nblk)
    return pl.pallas_call(
        functools.partial(_hgrn_kernel, nsub=tb // sb, nchunk=sb // L, L=L, nlev=nlev,
                          n_narrow=len(narrow)),
        grid=(nblk,),
        in_specs=[blk(fwd), blk(fwd), blk(fwd), blk(bwd), blk(bwd), blk(bwd_g),
                  _const_spec(mall_f.shape), _const_spec(mall_b.shape),
                  _const_spec(trib_f.shape), _const_spec(trib_b.shape),
                  _const_spec(s0_f.shape), _const_spec(s0_b.shape)] + n_in,
        out_specs=[blk(fwd), blk(bwd)] + [pl.BlockSpec(s0_f.shape, lambda i: (0, 0, 0))] * 2 + n_out,
        out_shape=[jax.ShapeDtypeStruct((t, w), F32)] * 2
        + [jax.ShapeDtypeStruct(s0_f.shape, F32)] * 2 + n_shape,
        scratch_shapes=[pltpu.VMEM(s0_f.shape, F32)] * 2,
        compiler_params=_cparams(),
        name="hgrn",
    )(q, v, g, q, v, g, mall_f, mall_b, trib_f, trib_b, s0_f, s0_b, *narrow)


def _merge_kernel(x_ref, er_ref, ec_ref, lng_ref, lnb_ref, mod_ref, wg_ref, bg_ref, of_ref, ob_ref,
                  ng_ref, four_ref, wfp_ref, whp_ref, wo_ref, bo_ref, w1_ref, w2_ref,
                  o_ref, w1b_ref, w2b_ref, *, tm):
    _narrow([w1_ref, w2_ref], [w1b_ref, w2b_ref])
    w4 = N_HEADS * HEAD_DIM
    d = x_ref.shape[1]
    sub = min(tm, SUB_ROWS)
    for r0 in range(0, tm, sub):
        rows = slice(r0, r0 + sub)
        xl, hl = _ln_in_modulated(x_ref, er_ref, ec_ref, lng_ref, lnb_ref, mod_ref, r0, sub)
        hb = hl.astype(BF16)
        og = _dot(hb, wg_ref[:, 0:w4]) + bg_ref[:, 0:w4]
        o = of_ref[rows, :] + ob_ref[rows, :]
        parts = []
        for hd in range(N_HEADS):
            oh = o[:, hd * HEAD_DIM:(hd + 1) * HEAD_DIM]
            ms = jnp.mean(oh * oh, axis=-1, keepdims=True)
            parts.append(oh * lax.rsqrt(ms + RMS_EPS))
        on = jnp.concatenate(parts, axis=1) * ng_ref[...]
        oh = (on * (og * _sigmoid(og))).astype(BF16)
        g_four = _sigmoid(_dot(hb, wg_ref[:, w4:w4 + d]) + bg_ref[:, w4:w4 + d])
        four = jnp.concatenate([four_ref[gi, rows, :] for gi in range(N_GROUPS)], axis=1)
        y = g_four * _dot(four.astype(BF16), wfp_ref[...])
        g_hgrn = _sigmoid(_dot(hb, wg_ref[:, w4 + d:w4 + 2 * d]) + bg_ref[:, w4 + d:w4 + 2 * d])
        y = y + g_hgrn * _dot(oh, whp_ref[...])
        mix = _dot(y.astype(BF16), wo_ref[...]) + bo_ref[...]
        o_ref[rows, :] = ALPHA * xl + mod_ref[2:3, :] * mix


def _merge(x, er, ec, lng, lnb, mod, wg, bg, o_f, o_b, ng, four, wfp, whp, wo, bo, w1, w2, tm):
    t, d = x.shape
    w4 = N_HEADS * HEAD_DIM
    nrow = tm // GRID_W
    row = lambda i: (i, 0)
    consts = [ec, lng, lnb, mod]
    consts2 = [wfp, whp, wo, bo]
    n_in, n_out, n_shape = _narrow_specs([w1, w2], t // tm)
    return pl.pallas_call(
        functools.partial(_merge_kernel, tm=tm),
        grid=(t // tm,),
        in_specs=[pl.BlockSpec((tm, d), row), pl.BlockSpec((nrow, d // 2), row)]
        + [_const_spec(a.shape) for a in consts]
        + [_const_cols(wg, w4 + 2 * d, 1), _const_cols(bg, w4 + 2 * d, 1)]
        + [pl.BlockSpec((tm, w4), row), pl.BlockSpec((tm, w4), row), _const_spec(ng.shape),
           pl.BlockSpec((N_GROUPS, tm, GROUP_DIM), lambda i: (0, i, 0))]
        + [_const_spec(a.shape) for a in consts2] + n_in,
        out_specs=[pl.BlockSpec((tm, d), row)] + n_out,
        out_shape=[jax.ShapeDtypeStruct((t, d), F32)] + n_shape,
        compiler_params=_cparams(),
        name="merge",
    )(x, er, *consts, wg, bg, o_f, o_b, ng, four, *consts2, w1, w2)


def _mlp_kernel(x_ref, mod_ref, ag_ref, ab_ref, w1_ref, b1_ref, w2_ref, b2_ref, pg_ref, pb_ref, o_ref,
                *, nsplit):
    dff = w1_ref.shape[1]
    cw = dff // nsplit
    tm = x_ref.shape[0]
    sub = min(tm, SUB_ROWS)
    for r0 in range(0, tm, sub):
        rows = slice(r0, r0 + sub)
        x1 = _layernorm(x_ref[rows, :], ag_ref[...], ab_ref[...])
        hb = (x1 * (1.0 + mod_ref[4:5, :]) + mod_ref[3:4, :]).astype(BF16)
        acc = jnp.zeros(x1.shape, F32)
        for c in range(nsplit):
            cs = slice(c * cw, (c + 1) * cw)
            a = jnp.maximum(_dot(hb, w1_ref[:, cs]) + b1_ref[:, cs], 0.0)
            acc = acc + _dot((a * a).astype(BF16), w2_ref[cs, :])
        m = acc + b2_ref[...]
        o_ref[rows, :] = _layernorm(ALPHA * x1 + mod_ref[5:6, :] * m, pg_ref[...], pb_ref[...])


def _mlp(z1, mod, ag, ab, w1, b1, w2, b2, pg, pb, tm):
    t, d = z1.shape
    row = lambda i: (i, 0)
    consts = [mod, ag, ab, w1, b1, w2, b2, pg, pb]
    return pl.pallas_call(
        functools.partial(_mlp_kernel, nsplit=4),
        grid=(t // tm,),
        in_specs=[pl.BlockSpec((tm, d), row)] + [_const_spec(a.shape) for a in consts],
        out_specs=pl.BlockSpec((tm, d), row),
        out_shape=jax.ShapeDtypeStruct((t, d), F32),
        compiler_params=_cparams(),
        name="mlp",
    )(z1, *consts)


def _pos_tables(rows, cols, dim):
    quarter = dim // 4
    omega = 1.0 / (POS_BASE ** (np.arange(quarter, dtype=np.float64) / quarter))
    r = np.arange(rows, dtype=np.float64)[:, None] * omega
    cc = np.arange(cols, dtype=np.float64)[:, None] * omega
    er = np.concatenate([np.sin(r), np.cos(r)], axis=-1)
    ec = np.concatenate([np.sin(cc), np.cos(cc)], axis=-1)
    return jnp.asarray(er.astype(np.float32)), jnp.asarray(ec.astype(np.float32))


def _dft_constants(t):
    n = RADIX
    kn = np.outer(np.arange(n), np.arange(n)).astype(np.float64)
    c = np.cos(2.0 * np.pi * kn / n)
    s = np.sin(2.0 * np.pi * kn / n)
    chan = np.concatenate([c, s], axis=0)
    stage1 = np.block([[c, -s], [-s, -c]])
    scale = 1.0 / np.sqrt(float(t) * GROUP_DIM)
    base2 = np.stack([c, s]) * scale
    beta = 2.0 * np.pi * kn / t
    twiddle = np.concatenate([np.cos(beta), np.sin(beta)], axis=1)
    as_f32 = lambda a: jnp.asarray(a.astype(np.float32))
    return as_f32(chan), as_f32(stage1), as_f32(base2), as_f32(twiddle)


def kernel(x, c, ctx, c_ctx, ln_in_g, ln_in_b, w_ada, b_ada, w_in, b_in, hgrn_lb_logits, hgrn_norm_g,
           w_four_proj, w_hgrn_proj, w_out, b_out, w_mlp1, b_mlp1, w_mlp2, b_mlp2, ln_post_g, ln_post_b):
    B, T, D = x.shape
    assert B == 1 and T == RADIX * RADIX and T % GRID_W == 0
    TC = ctx.shape[1]
    w4 = N_HEADS * HEAD_DIM
    row2 = lambda a: a.reshape(1, -1)

    mod_l, mod_c, w_in_b = _mod_vectors(c[0], c_ctx, w_ada[0], row2(b_ada[0]), w_in[0])
    mod_l = mod_l.reshape(6, D)
    mod_c = mod_c.reshape(6, D)

    er, ec = _pos_tables(T // GRID_W, GRID_W, D)
    lng, lnb = row2(ln_in_g), row2(ln_in_b)
    b_in2 = row2(b_in[0])
    assert w_in_b.shape[1] == 2 * 5 * w4
    w_a = w_g = w_in_b
    b_a = b_g = b_in2
    l0 = hgrn_lb_logits[:, 0, :].reshape(1, 2 * w4)
    l1 = hgrn_lb_logits[:, 1, :].reshape(1, 2 * w4)
    dft_chan, dft_s1, dft_base2, dft_tw = _dft_constants(T)

    zc = jnp.zeros((TC // GRID_W, D // 2), F32)
    _, qc, vc, gc = _inproj(ctx[0], zc, jnp.zeros_like(ec), lng, lnb, mod_c, w_a, b_a, l0, l1, tm=TC)
    s_zero = jnp.zeros((N_HEADS, HEAD_DIM, HEAD_DIM), F32)
    _, _, s_f, s_b = _hgrn_scan(qc, vc, gc, s_zero, s_zero, tb=TC)

    u, q, v, g = _inproj(x[0], er, ec, lng, lnb, mod_l, w_a, b_a, l0, l1, tm=ROW_TILE)
    four = _dft2(dft_base2, dft_tw, _dft1(dft_chan, dft_s1, u))
    o_f, o_b, _, _, w_fp_b, w_hp_b, w_out_b = _hgrn_scan(
        q, v, g, s_f, s_b, tb=HGRN_BLOCK, narrow=(w_four_proj[0], w_hgrn_proj[0], w_out[0]))

    z1, w1_b, w2_b = _merge(x[0], er, ec, lng, lnb, mod_l, w_g, b_g, o_f, o_b, row2(hgrn_norm_g[0]),
                            four, w_fp_b, w_hp_b, w_out_b, row2(b_out[0]), w_mlp1[0], w_mlp2[0],
                            tm=ROW_TILE)
    out = _mlp(z1, mod_l, row2(ln_post_g[0, 0]), row2(ln_post_b[0, 0]), w1_b, row2(b_mlp1[0]),
               w2_b, row2(b_mlp2[0]), row2(ln_post_g[0, 1]), row2(ln_post_b[0, 1]), tm=ROW_TILE)
    return out[None]
```

```python
import functools

import numpy as np
import jax
import jax.numpy as jnp
from jax import lax
from jax.experimental import pallas as pl
from jax.experimental.pallas import tpu as pltpu

F32 = jnp.float32
BF16 = jnp.bfloat16

GRID_W = 64
N_GROUPS = 4
GROUP_DIM = 128
N_HEADS = 4
HEAD_DIM = 128
POS_BASE = 10000.0
LN_EPS = 1e-5
RMS_EPS = 1e-6
DEPTH = 1
ALPHA = (2.0 * DEPTH) ** 0.25

RADIX = 128
HGRN_CHUNK = 64
HGRN_SUB_BLOCK = 256
HGRN_BLOCK = 1024
ROW_TILE = 1024
SUB_ROWS = 256
MOD_COL_TILE = 1536
SAFE_LOG_DECAY = 80.0
VMEM_LIMIT_BYTES = 56 * 1024 * 1024


def _cparams(n_axes=1):
    return pltpu.CompilerParams(dimension_semantics=("arbitrary",) * n_axes,
                                vmem_limit_bytes=VMEM_LIMIT_BYTES)


def _const_spec(shape):
    nd = len(shape)
    return pl.BlockSpec(shape, lambda *_: (0,) * nd, pipeline_mode=pl.Buffered(1))


def _const_cols(arr, width, j):
    return pl.BlockSpec((arr.shape[0], width), lambda *_: (0, j), pipeline_mode=pl.Buffered(1))


def _sigmoid(x):
    return 1.0 / (1.0 + jnp.exp(-x))


def _layernorm(x, g, b):
    mu = jnp.mean(x, axis=-1, keepdims=True)
    xc = x - mu
    var = jnp.mean(xc * xc, axis=-1, keepdims=True)
    return xc * lax.rsqrt(var + LN_EPS) * g + b


def _dot(a, b):
    return jnp.dot(a, b, preferred_element_type=F32)


def _dot_nt(a, b):
    return lax.dot_general(a, b, (((1,), (1,)), ((), ())), preferred_element_type=F32)


def _dot_tn(a, b):
    return lax.dot_general(a, b, (((0,), (0,)), ((), ())), preferred_element_type=F32)


def _narrow_specs(arrays, nsteps):
    chunk = lambda a: pl.BlockSpec((a.shape[0] // nsteps, a.shape[1]), lambda i, *_: (i, 0))
    return ([chunk(a) for a in arrays], [chunk(a) for a in arrays],
            [jax.ShapeDtypeStruct(a.shape, BF16) for a in arrays])


def _narrow(src_refs, dst_refs):
    for s_ref, d_ref in zip(src_refs, dst_refs):
        d_ref[...] = s_ref[...].astype(BF16)


def _mod_kernel(cl_ref, cx_ref, w_ref, b_ref, win_ref, ol_ref, ox_ref, winb_ref):
    w = w_ref[...]
    for c_ref, o_ref in ((cl_ref, ol_ref), (cx_ref, ox_ref)):
        cs = c_ref[...]
        s = cs * _sigmoid(cs)
        o_ref[...] = jnp.sum(s * w, axis=0, keepdims=True) + b_ref[...]
    _narrow([win_ref], [winb_ref])


def _mod_vectors(c_lat, c_ctx, w_ada, b_ada, w_in):
    d, n = w_ada.shape
    tn = MOD_COL_TILE
    col = lambda j: (0, j)
    n_in, n_out, n_shape = _narrow_specs([w_in], n // tn)
    return pl.pallas_call(
        _mod_kernel,
        grid=(n // tn,),
        in_specs=[_const_spec((d, 1)), _const_spec((d, 1)),
                  pl.BlockSpec((d, tn), col), pl.BlockSpec((1, tn), col)] + n_in,
        out_specs=[pl.BlockSpec((1, tn), col)] * 2 + n_out,
        out_shape=[jax.ShapeDtypeStruct((1, n), F32)] * 2 + n_shape,
        compiler_params=_cparams(),
        name="mod",
    )(c_lat.reshape(d, 1), c_ctx.reshape(d, 1), w_ada, b_ada, w_in)


def _ln_in_modulated(x_ref, er_ref, ec_ref, lng_ref, lnb_ref, mod_ref, r0, nr):
    x = x_ref[r0:r0 + nr, :]
    half = x.shape[1] // 2
    nrow = nr // GRID_W
    e0 = r0 // GRID_W
    left = jnp.concatenate(
        [jnp.broadcast_to(er_ref[e0 + r:e0 + r + 1, :], (GRID_W, half)) for r in range(nrow)], axis=0)
    right = jnp.concatenate([ec_ref[...]] * nrow, axis=0)
    xp = jnp.concatenate([x[:, :half] + left, x[:, half:] + right], axis=1)
    xl = _layernorm(xp, lng_ref[...], lnb_ref[...])
    hl = xl * (1.0 + mod_ref[1:2, :]) + mod_ref[0:1, :]
    return xl, hl


def _inproj_kernel(x_ref, er_ref, ec_ref, lng_ref, lnb_ref, mod_ref, w_ref, b_ref,
                   l0_ref, l1_ref, u_ref, q_ref, v_ref, g_ref, *, tm):
    w4 = N_GROUPS * GROUP_DIM
    l0 = l0_ref[...]
    l1 = l1_ref[...]
    m = jnp.maximum(l0, l1)
    e0 = jnp.exp(l0 - m)
    lb = e0 / (e0 + jnp.exp(l1 - m))
    sub = min(tm, SUB_ROWS)
    for r0 in range(0, tm, sub):
        rows = slice(r0, r0 + sub)
        _, hl = _ln_in_modulated(x_ref, er_ref, ec_ref, lng_ref, lnb_ref, mod_ref, r0, sub)
        hb = hl.astype(BF16)
        fp = _dot(hb, w_ref[:, 3 * w4:5 * w4]) + b_ref[:, 3 * w4:5 * w4]
        g_ref[rows, :] = jnp.log(lb + (1.0 - lb) * _sigmoid(fp))
        qp = _dot(hb, w_ref[:, w4:2 * w4]) + b_ref[:, w4:2 * w4]
        q_ref[rows, :] = (qp * _sigmoid(qp)).astype(BF16)
        u = (_dot(hb, w_ref[:, 0:w4]) + b_ref[:, 0:w4]).astype(BF16)
        for gi in range(N_GROUPS):
            u_ref[gi, rows, :] = u[:, gi * GROUP_DIM:(gi + 1) * GROUP_DIM]
        v_ref[rows, :] = (_dot(hb, w_ref[:, 2 * w4:3 * w4]) + b_ref[:, 2 * w4:3 * w4]).astype(BF16)


def _inproj(x, er, ec, lng, lnb, mod, w, b, l0, l1, tm):
    t, d = x.shape
    w4 = N_GROUPS * GROUP_DIM
    nrow = tm // GRID_W
    row = lambda i: (i, 0)
    return pl.pallas_call(
        functools.partial(_inproj_kernel, tm=tm),
        grid=(t // tm,),
        in_specs=[pl.BlockSpec((tm, d), row),
                  pl.BlockSpec((nrow, d // 2), row),
                  _const_spec(ec.shape), _const_spec(lng.shape), _const_spec(lnb.shape),
                  _const_spec(mod.shape), _const_cols(w, 5 * w4, 0), _const_cols(b, 5 * w4, 0),
                  _const_spec(l0.shape), _const_spec(l1.shape)],
        out_specs=[pl.BlockSpec((N_GROUPS, tm, GROUP_DIM), lambda i: (0, i, 0))]
        + [pl.BlockSpec((tm, w4), row)] * 2 + [pl.BlockSpec((tm, 2 * w4), row)],
        out_shape=[jax.ShapeDtypeStruct((N_GROUPS, t, GROUP_DIM), BF16)]
        + [jax.ShapeDtypeStruct((t, w4), BF16)] * 2 + [jax.ShapeDtypeStruct((t, 2 * w4), F32)],
        compiler_params=_cparams(),
        name="inproj",
    )(x, er, ec, lng, lnb, mod, w, b, l0, l1)


DFT_BATCH = 16
F32_SUBLANES = 8


def _dft1_kernel(cs_ref, m_ref, u_ref, a_ref, su_ref, sa_ref):
    r, nb, w = u_ref.shape
    hs = F32_SUBLANES
    u = u_ref[...].astype(F32)
    for h in range(nb // hs):
        su_ref[h] = u[:, h * hs:(h + 1) * hs, :].reshape(r * hs, w)
    cols = [su_ref[j // hs, pl.ds(j % hs, r, stride=hs), :].astype(BF16) for j in range(nb)]
    pq = _dot(cs_ref[...].astype(BF16), jnp.concatenate(cols, axis=1))
    for j in range(nb):
        sa_ref[j // hs, pl.ds(j % hs, 2 * r, stride=hs), :] = pq[:, j * w:(j + 1) * w]
    mat = m_ref[...].astype(BF16)
    parts = []
    for h in range(nb // hs):
        rows_pq = jnp.concatenate([sa_ref[h, 0:r * hs, :], sa_ref[h, r * hs:2 * r * hs, :]], axis=1)
        parts.append(_dot(rows_pq.astype(BF16), mat))
    a_ref[0] = jnp.concatenate([p[:, :w].reshape(r, hs, w) for p in parts], axis=1).astype(BF16)
    a_ref[1] = jnp.concatenate([p[:, w:].reshape(r, hs, w) for p in parts], axis=1).astype(BF16)


def _dft1(cs, mat, u):
    ng, t, w = u.shape
    r = RADIX
    nb = 2 * DFT_BATCH
    nh = nb // F32_SUBLANES
    return pl.pallas_call(
        _dft1_kernel,
        grid=(ng, r // nb),
        in_specs=[_const_spec(cs.shape), _const_spec(mat.shape),
                  pl.BlockSpec((None, r, nb, w), lambda g, o: (g, 0, o, 0))],
        out_specs=pl.BlockSpec((2, r, nb, w), lambda g, o: (0, 0, o, g)),
        out_shape=jax.ShapeDtypeStruct((2, r, r, ng * w), BF16),
        scratch_shapes=[pltpu.VMEM((nh, r * F32_SUBLANES, w), F32),
                        pltpu.VMEM((nh, 2 * r * F32_SUBLANES, w), F32)],
        compiler_params=_cparams(2),
        name="dft1",
    )(cs, mat, u.reshape(ng, r, r, w))


def _dft2_kernel(cs_ref, tw_ref, a_ref, o_ref, s_ref):
    nb, r, w = a_ref.shape[1], a_ref.shape[2], a_ref.shape[3]
    hs = F32_SUBLANES
    cos_a, sin_a = cs_ref[0], cs_ref[1]
    for j in range(nb):
        cos_b, sin_b = tw_ref[j:j + 1, 0:r], tw_ref[j:j + 1, r:2 * r]
        gmat = jnp.concatenate([cos_a * cos_b - sin_a * sin_b, sin_a * cos_b + cos_a * sin_b], axis=1)
        a = jnp.concatenate([a_ref[0, j], a_ref[1, j]], axis=0)
        y = _dot(gmat.astype(BF16), a)
        for gi in range(N_GROUPS):
            s_ref[gi, j // hs, pl.ds(j % hs, r, stride=hs), :] = y[:, gi * GROUP_DIM:(gi + 1) * GROUP_DIM]
    for gi in range(N_GROUPS):
        o_ref[gi] = jnp.concatenate([s_ref[gi, h].reshape(r, hs, GROUP_DIM) for h in range(nb // hs)],
                                    axis=1).astype(BF16)


def _dft2(base, twiddle, a4):
    _, r, _, w = a4.shape
    nb = DFT_BATCH
    out = pl.pallas_call(
        _dft2_kernel,
        grid=(r // nb,),
        in_specs=[_const_spec(base.shape),
                  pl.BlockSpec((nb, 2 * r), lambda k: (k, 0)),
                  pl.BlockSpec((2, nb, r, w), lambda k: (0, k, 0, 0))],
        out_specs=pl.BlockSpec((N_GROUPS, r, None, nb, GROUP_DIM), lambda k: (0, 0, k, 0, 0)),
        out_shape=jax.ShapeDtypeStruct((N_GROUPS, r, r // nb, nb, GROUP_DIM), BF16),
        scratch_shapes=[pltpu.VMEM((N_GROUPS, nb // F32_SUBLANES, r * F32_SUBLANES, GROUP_DIM), F32)],
        compiler_params=_cparams(),
        name="dft2",
    )(base, twiddle, a4)
    return out.reshape(N_GROUPS, r * r, GROUP_DIM)


def _hgrn_tables(L, reverse):
    nlev = int(np.log2(L))
    idx = np.arange(L)
    t = idx[:, None]
    i = idx[None, :]
    blocks = [(i >= t) if reverse else (i <= t)]
    for j in range(nlev):
        h = L >> (j + 1)
        mid = (t // (2 * h)) * (2 * h) + h
        upper = t >= mid
        if reverse:
            blk = np.where(upper, (i >= mid) & (i < t), (i >= t) & (i < mid))
        else:
            blk = np.where(upper, (i >= mid) & (i <= t), (i > t) & (i < mid))
        blocks.append(blk)
    blocks.append((i < t) if reverse else (i > t))
    return np.concatenate(blocks, axis=0).astype(np.float32), nlev


def _as_column(row):
    n = row.shape[1]
    return jnp.broadcast_to(row, (n, n)).T


def _split_hi_lo(g):
    hi = g.astype(BF16)
    return hi, (g - hi.astype(F32)).astype(BF16)


def _hgrn_chunk_exact(q_ref, v_ref, g_ref, o_ref, st_ref, rows, mall, pair_masks, query_rows,
                      L, nlev, reverse):
    last = 0 if reverse else L - 1
    g = g_ref[rows, :]
    g_hi, g_lo = _split_hi_lo(g)
    ex = jnp.exp(_dot(mall, g_hi) + _dot(mall, g_lo))
    q = q_ref[rows, :].astype(F32)
    v = v_ref[rows, :]
    k = 1.0 - jnp.exp(g)
    e_cum = ex[0:L]
    qe = (q * e_cum).astype(BF16)
    ke = (k * ex[(nlev + 1) * L:(nlev + 2) * L]).astype(BF16)
    zs = [(jnp.where(query_rows[j], q, k) * ex[(j + 1) * L:(j + 2) * L]).astype(BF16)
          for j in range(nlev)]
    qk = q * k
    e_last = e_cum[last:last + 1, :]
    for hd in range(N_HEADS):
        sl = slice(hd * HEAD_DIM, (hd + 1) * HEAD_DIM)
        sc = jnp.zeros((L, L), F32)
        for j in range(nlev):
            zj = zs[j][:, sl]
            sc = jnp.where(pair_masks[j], _dot_nt(zj, zj), sc)
        st = st_ref[hd]
        vh = v[:, sl]
        o = _dot(sc.astype(BF16), vh) + _dot(qe[:, sl], st.astype(BF16))
        o = o + jnp.sum(qk[:, sl], axis=-1, keepdims=True) * vh.astype(F32)
        o_ref[rows, sl] = o
        st_ref[hd] = st * _as_column(e_last[:, sl]) + _dot_tn(ke[:, sl], vh)


def _hgrn_block_fast(q_ref, v_ref, g_ref, trib_ref, o_ref, st_ref, r0, nchunk, L, reverse):
    half = L // 2
    tb = nchunk * L
    blk = slice(r0, r0 + tb)
    chunks = []
    for c in range(nchunk):
        base = c * L
        if reverse:
            chunks.append((slice(base, base + L), slice(base + half, base + L),
                           slice(base, base + half), base + half, base))
        else:
            chunks.append((slice(base, base + L), slice(base, base + half),
                           slice(base + half, base + L), base + half - 1, base + L - 1))
    g = g_ref[blk, :]
    b = _dot(trib_ref[...], g.astype(BF16))
    q = q_ref[blk, :].astype(F32)
    v = v_ref[blk, :]
    k = 1.0 - jnp.exp(g)
    e_b = jnp.exp(b)
    qe = (q * e_b).astype(BF16)
    c2 = [b[sec] - b[edge:edge + 1, :] for (_, _, sec, edge, _) in chunks]
    q2 = jnp.concatenate([q[ch[2]] * jnp.exp(c2[c]) for c, ch in enumerate(chunks)], axis=0)
    own = []
    for c, (_, fst, _, _, _) in enumerate(chunks):
        own += [c2[c], b[fst]] if reverse else [b[fst], c2[c]]
    kh = (k * jnp.exp(-jnp.concatenate(own, axis=0))).astype(BF16)
    tail = jnp.concatenate([b[last:last + 1, :] - b[rows] for (rows, _, _, _, last) in chunks], axis=0)
    ke = (k * jnp.exp(tail)).astype(BF16)
    lhs = jnp.concatenate([qe, q2.astype(BF16)], axis=0)

    row = lax.broadcasted_iota(jnp.int32, (tb, tb), 0)
    col = lax.broadcasted_iota(jnp.int32, (tb, tb), 1)
    valid = (row // L == col // L) & ((col >= row) if reverse else (col <= row))
    if reverse:
        use_near = (row % L < half) & (col % L < half)
    else:
        use_near = (row % L >= half) & (col % L >= half)

    for hd in range(N_HEADS):
        sl = slice(hd * HEAD_DIM, (hd + 1) * HEAD_DIM)
        s_all = _dot_nt(lhs[:, sl], kh[:, sl])
        far = s_all[0:tb]
        pieces = []
        for c, (_, fst, _, _, _) in enumerate(chunks):
            near_c = s_all[tb + c * half:tb + (c + 1) * half]
            pieces += [near_c, far[fst]] if reverse else [far[fst], near_c]
        near = jnp.concatenate(pieces, axis=0)
        sc = jnp.where(valid, jnp.where(use_near, near, far), 0.0).astype(BF16)
        vh = v[:, sl]
        o_intra = _dot(sc, vh)
        upd = [_dot_tn(ke[rows, sl], vh[rows]) for (rows, _, _, _, _) in chunks]
        st = st_ref[hd]
        o_inter = [None] * nchunk
        for c in (range(nchunk - 1, -1, -1) if reverse else range(nchunk)):
            rows, _, _, _, last = chunks[c]
            o_inter[c] = _dot(qe[rows, sl], st.astype(BF16))
            st = st * _as_column(e_b[last:last + 1, sl]) + upd[c]
        st_ref[hd] = st
        o_ref[blk, sl] = o_intra + jnp.concatenate(o_inter, axis=0)


def _hgrn_block_exact(q_ref, v_ref, g_ref, mall_ref, o_ref, st_ref, nchunk, L, nlev, reverse):
    row = lax.broadcasted_iota(jnp.int32, (L, L), 0)
    col = lax.broadcasted_iota(jnp.int32, (L, L), 1)
    rowc = lax.broadcasted_iota(jnp.int32, (L, N_HEADS * HEAD_DIM), 0)
    pair_masks, query_rows = [], []
    for j in range(nlev):
        h = L >> (j + 1)
        same = (row // (2 * h)) == (col // (2 * h))
        row_up = (row // h) % 2 == 1
        col_up = (col // h) % 2 == 1
        if reverse:
            pair_masks.append(same & jnp.logical_not(row_up) & col_up)
            query_rows.append((rowc // h) % 2 == 0)
        else:
            pair_masks.append(same & row_up & jnp.logical_not(col_up))
            query_rows.append((rowc // h) % 2 == 1)
    mall = mall_ref[...]
    for c in (range(nchunk - 1, -1, -1) if reverse else range(nchunk)):
        _hgrn_chunk_exact(q_ref, v_ref, g_ref, o_ref, st_ref, slice(c * L, (c + 1) * L),
                          mall, pair_masks, query_rows, L, nlev, reverse)


def _min_leaf_log_decay(g_ref, leaf):
    g = g_ref[...]
    return jnp.min(jnp.sum(g.reshape(g.shape[0] // leaf, leaf, g.shape[1]), axis=1))


def _hgrn_kernel(*refs, nsub, nchunk, L, nlev, n_narrow):
    (qf_ref, vf_ref, gf_ref, qb_ref, vb_ref, gb_ref, mallf_ref, mallb_ref,
     tribf_ref, tribb_ref, s0f_ref, s0b_ref) = refs[:12]
    narrow_in = refs[12:12 + n_narrow]
    of_ref, ob_ref, sff_ref, sfb_ref = refs[12 + n_narrow:16 + n_narrow]
    narrow_out = refs[16 + n_narrow:16 + 2 * n_narrow]
    stf_ref, stb_ref = refs[16 + 2 * n_narrow:]
    _narrow(narrow_in, narrow_out)
    i = pl.program_id(0)

    @pl.when(i == 0)
    def _():
        stf_ref[...] = s0f_ref[...]
        stb_ref[...] = s0b_ref[...]

    safe = jnp.minimum(_min_leaf_log_decay(gf_ref, L // 2),
                       _min_leaf_log_decay(gb_ref, L // 2)) >= -SAFE_LOG_DECAY

    @pl.when(safe)
    def _():
        for j in range(nsub):
            _hgrn_block_fast(qf_ref, vf_ref, gf_ref, tribf_ref, of_ref, stf_ref,
                             j * nchunk * L, nchunk, L, False)
            _hgrn_block_fast(qb_ref, vb_ref, gb_ref, tribb_ref, ob_ref, stb_ref,
                             (nsub - 1 - j) * nchunk * L, nchunk, L, True)

    @pl.when(jnp.logical_not(safe))
    def _():
        nc = nsub * nchunk
        _hgrn_block_exact(qf_ref, vf_ref, gf_ref, mallf_ref, of_ref, stf_ref, nc, L, nlev, False)
        _hgrn_block_exact(qb_ref, vb_ref, gb_ref, mallb_ref, ob_ref, stb_ref, nc, L, nlev, True)

    @pl.when(i == pl.num_programs(0) - 1)
    def _():
        sff_ref[...] = stf_ref[...]
        sfb_ref[...] = stb_ref[...]


def _hgrn_scan(q, v, g, s0_f, s0_b, tb, narrow=()):
    t, w = q.shape
    L = HGRN_CHUNK
    nblk = t // tb
    sb = min(tb, HGRN_SUB_BLOCK)
    consts = []
    for reverse in (False, True):
        mall_np, nlev = _hgrn_tables(L, reverse)
        consts.append((jnp.asarray(mall_np, dtype=BF16),
                       jnp.asarray(np.kron(np.eye(sb // L, dtype=np.float32), mall_np[0:L]), dtype=BF16)))
    (mall_f, trib_f), (mall_b, trib_b) = consts
    fwd = lambda i: (i, 0)
    bwd = lambda i: (nblk - 1 - i, 0)
    bwd_g = lambda i: (nblk - 1 - i, 1)
    blk = lambda m: pl.BlockSpec((tb, w), m)
    n_in, n_out, n_shape = _narrow_specs(list(narrow), nblk)
    return pl.pallas_call(
        functools.partial(_hgrn_kernel, nsub=tb // sb, nchunk=sb // L, L=L, nlev=nlev,
                          n_narrow=len(narrow)),
        grid=(nblk,),
        in_specs=[blk(fwd), blk(fwd), blk(fwd), blk(bwd), blk(bwd), blk(bwd_g),
                  _const_spec(mall_f.shape), _const_spec(mall_b.shape),
                  _const_spec(trib_f.shape), _const_spec(trib_b.shape),
                  _const_spec(s0_f.shape), _const_spec(s0_b.shape)] + n_in,
        out_specs=[blk(fwd), blk(bwd)] + [pl.BlockSpec(s0_f.shape, lambda i: (0, 0, 0))] * 2 + n_out,
        out_shape=[jax.ShapeDtypeStruct((t, w), F32)] * 2
        + [jax.ShapeDtypeStruct(s0_f.shape, F32)] * 2 + n_shape,
        scratch_shapes=[pltpu.VMEM(s0_f.shape, F32)] * 2,
        compiler_params=_cparams(),
        name="hgrn",
    )(q, v, g, q, v, g, mall_f, mall_b, trib_f, trib_b, s0_f, s0_b, *narrow)


def _merge_kernel(x_ref, er_ref, ec_ref, lng_ref, lnb_ref, mod_ref, wg_ref, bg_ref, of_ref, ob_ref,
                  ng_ref, four_ref, wfp_ref, whp_ref, wo_ref, bo_ref, w1_ref, w2_ref,
                  o_ref, w1b_ref, w2b_ref, *, tm):
    _narrow([w1_ref, w2_ref], [w1b_ref, w2b_ref])
    w4 = N_HEADS * HEAD_DIM
    d = x_ref.shape[1]
    sub = min(tm, SUB_ROWS)
    for r0 in range(0, tm, sub):
        rows = slice(r0, r0 + sub)
        xl, hl = _ln_in_modulated(x_ref, er_ref, ec_ref, lng_ref, lnb_ref, mod_ref, r0, sub)
        hb = hl.astype(BF16)
        og = _dot(hb, wg_ref[:, 0:w4]) + bg_ref[:, 0:w4]
        o = of_ref[rows, :] + ob_ref[rows, :]
        parts = []
        for hd in range(N_HEADS):
            oh = o[:, hd * HEAD_DIM:(hd + 1) * HEAD_DIM]
            ms = jnp.mean(oh * oh, axis=-1, keepdims=True)
            parts.append(oh * lax.rsqrt(ms + RMS_EPS))
        on = jnp.concatenate(parts, axis=1) * ng_ref[...]
        oh = (on * (og * _sigmoid(og))).astype(BF16)
        g_four = _sigmoid(_dot(hb, wg_ref[:, w4:w4 + d]) + bg_ref[:, w4:w4 + d])
        four = jnp.concatenate([four_ref[gi, rows, :] for gi in range(N_GROUPS)], axis=1)
        y = g_four * _dot(four.astype(BF16), wfp_ref[...])
        g_hgrn = _sigmoid(_dot(hb, wg_ref[:, w4 + d:w4 + 2 * d]) + bg_ref[:, w4 + d:w4 + 2 * d])
        y = y + g_hgrn * _dot(oh, whp_ref[...])
        mix = _dot(y.astype(BF16), wo_ref[...]) + bo_ref[...]
        o_ref[rows, :] = ALPHA * xl + mod_ref[2:3, :] * mix


def _merge(x, er, ec, lng, lnb, mod, wg, bg, o_f, o_b, ng, four, wfp, whp, wo, bo, w1, w2, tm):
    t, d = x.shape
    w4 = N_HEADS * HEAD_DIM
    nrow = tm // GRID_W
    row = lambda i: (i, 0)
    consts = [ec, lng, lnb, mod]
    consts2 = [wfp, whp, wo, bo]
    n_in, n_out, n_shape = _narrow_specs([w1, w2], t // tm)
    return pl.pallas_call(
        functools.partial(_merge_kernel, tm=tm),
        grid=(t // tm,),
        in_specs=[pl.BlockSpec((tm, d), row), pl.BlockSpec((nrow, d // 2), row)]
        + [_const_spec(a.shape) for a in consts]
        + [_const_cols(wg, w4 + 2 * d, 1), _const_cols(bg, w4 + 2 * d, 1)]
        + [pl.BlockSpec((tm, w4), row), pl.BlockSpec((tm, w4), row), _const_spec(ng.shape),
           pl.BlockSpec((N_GROUPS, tm, GROUP_DIM), lambda i: (0, i, 0))]
        + [_const_spec(a.shape) for a in consts2] + n_in,
        out_specs=[pl.BlockSpec((tm, d), row)] + n_out,
        out_shape=[jax.ShapeDtypeStruct((t, d), F32)] + n_shape,
        compiler_params=_cparams(),
        name="merge",
    )(x, er, *consts, wg, bg, o_f, o_b, ng, four, *consts2, w1, w2)


def _mlp_kernel(x_ref, mod_ref, ag_ref, ab_ref, w1_ref, b1_ref, w2_ref, b2_ref, pg_ref, pb_ref, o_ref,
                *, nsplit):
    dff = w1_ref.shape[1]
    cw = dff // nsplit
    tm = x_ref.shape[0]
    sub = min(tm, SUB_ROWS)
    for r0 in range(0, tm, sub):
        rows = slice(r0, r0 + sub)
        x1 = _layernorm(x_ref[rows, :], ag_ref[...], ab_ref[...])
        hb = (x1 * (1.0 + mod_ref[4:5, :]) + mod_ref[3:4, :]).astype(BF16)
        acc = jnp.zeros(x1.shape, F32)
        for c in range(nsplit):
            cs = slice(c * cw, (c + 1) * cw)
            a = jnp.maximum(_dot(hb, w1_ref[:, cs]) + b1_ref[:, cs], 0.0)
            acc = acc + _dot((a * a).astype(BF16), w2_ref[cs, :])
        m = acc + b2_ref[...]
        o_ref[rows, :] = _layernorm(ALPHA * x1 + mod_ref[5:6, :] * m, pg_ref[...], pb_ref[...])


def _mlp(z1, mod, ag, ab, w1, b1, w2, b2, pg, pb, tm):
    t, d = z1.shape
    row = lambda i: (i, 0)
    consts = [mod, ag, ab, w1, b1, w2, b2, pg, pb]
    return pl.pallas_call(
        functools.partial(_mlp_kernel, nsplit=4),
        grid=(t // tm,),
        in_specs=[pl.BlockSpec((tm, d), row)] + [_const_spec(a.shape) for a in consts],
        out_specs=pl.BlockSpec((tm, d), row),
        out_shape=jax.ShapeDtypeStruct((t, d), F32),
        compiler_params=_cparams(),
        name="mlp",
    )(z1, *consts)


def _pos_tables(rows, cols, dim):
    quarter = dim // 4
    omega = 1.0 / (POS_BASE ** (np.arange(quarter, dtype=np.float64) / quarter))
    r = np.arange(rows, dtype=np.float64)[:, None] * omega
    cc = np.arange(cols, dtype=np.float64)[:, None] * omega
    er = np.concatenate([np.sin(r), np.cos(r)], axis=-1)
    ec = np.concatenate([np.sin(cc), np.cos(cc)], axis=-1)
    return jnp.asarray(er.astype(np.float32)), jnp.asarray(ec.astype(np.float32))


def _dft_constants(t):
    n = RADIX
    kn = np.outer(np.arange(n), np.arange(n)).astype(np.float64)
    c = np.cos(2.0 * np.pi * kn / n)
    s = np.sin(2.0 * np.pi * kn / n)
    chan = np.concatenate([c, s], axis=0)
    stage1 = np.block([[c, -s], [-s, -c]])
    scale = 1.0 / np.sqrt(float(t) * GROUP_DIM)
    base2 = np.stack([c, s]) * scale
    beta = 2.0 * np.pi * kn / t
    twiddle = np.concatenate([np.cos(beta), np.sin(beta)], axis=1)
    as_f32 = lambda a: jnp.asarray(a.astype(np.float32))
    return as_f32(chan), as_f32(stage1), as_f32(base2), as_f32(twiddle)


def kernel(x, c, ctx, c_ctx, ln_in_g, ln_in_b, w_ada, b_ada, w_in, b_in, hgrn_lb_logits, hgrn_norm_g,
           w_four_proj, w_hgrn_proj, w_out, b_out, w_mlp1, b_mlp1, w_mlp2, b_mlp2, ln_post_g, ln_post_b):
    B, T, D = x.shape
    assert B == 1 and T == RADIX * RADIX and T % GRID_W == 0
    TC = ctx.shape[1]
    w4 = N_HEADS * HEAD_DIM
    row2 = lambda a: a.reshape(1, -1)

    mod_l, mod_c, w_in_b = _mod_vectors(c[0], c_ctx, w_ada[0], row2(b_ada[0]), w_in[0])
    mod_l = mod_l.reshape(6, D)
    mod_c = mod_c.reshape(6, D)

    er, ec = _pos_tables(T // GRID_W, GRID_W, D)
    lng, lnb = row2(ln_in_g), row2(ln_in_b)
    b_in2 = row2(b_in[0])
    assert w_in_b.shape[1] == 2 * 5 * w4
    w_a = w_g = w_in_b
    b_a = b_g = b_in2
    l0 = hgrn_lb_logits[:, 0, :].reshape(1, 2 * w4)
    l1 = hgrn_lb_logits[:, 1, :].reshape(1, 2 * w4)
    dft_chan, dft_s1, dft_base2, dft_tw = _dft_constants(T)

    zc = jnp.zeros((TC // GRID_W, D // 2), F32)
    _, qc, vc, gc = _inproj(ctx[0], zc, jnp.zeros_like(ec), lng, lnb, mod_c, w_a, b_a, l0, l1, tm=TC)
    s_zero = jnp.zeros((N_HEADS, HEAD_DIM, HEAD_DIM), F32)
    _, _, s_f, s_b = _hgrn_scan(qc, vc, gc, s_zero, s_zero, tb=TC)

    u, q, v, g = _inproj(x[0], er, ec, lng, lnb, mod_l, w_a, b_a, l0, l1, tm=ROW_TILE)
    four = _dft2(dft_base2, dft_tw, _dft1(dft_chan, dft_s1, u))
    o_f, o_b, _, _, w_fp_b, w_hp_b, w_out_b = _hgrn_scan(
        q, v, g, s_f, s_b, tb=HGRN_BLOCK, narrow=(w_four_proj[0], w_hgrn_proj[0], w_out[0]))

    z1, w1_b, w2_b = _merge(x[0], er, ec, lng, lnb, mod_l, w_g, b_g, o_f, o_b, row2(hgrn_norm_g[0]),
                            four, w_fp_b, w_hp_b, w_out_b, row2(b_out[0]), w_mlp1[0], w_mlp2[0],
                            tm=ROW_TILE)
    out = _mlp(z1, mod_l, row2(ln_post_g[0, 0]), row2(ln_post_b[0, 0]), w1_b, row2(b_mlp1[0]),
               w2_b, row2(b_mlp2[0]), row2(ln_post_g[0, 1]), row2(ln_post_b[0, 1]), tm=ROW_TILE)
    return out[None]
```

```python
import functools

import numpy as np
import jax
import jax.numpy as jnp
from jax import lax
from jax.experimental import pallas as pl
from jax.experimental.pallas import tpu as pltpu

F32 = jnp.float32
BF16 = jnp.bfloat16

GRID_W = 64
N_GROUPS = 4
GROUP_DIM = 128
N_HEADS = 4
HEAD_DIM = 128
POS_BASE = 10000.0
LN_EPS = 1e-5
RMS_EPS = 1e-6
DEPTH = 1
ALPHA = (2.0 * DEPTH) ** 0.25

RADIX = 128
HGRN_CHUNK = 64
HGRN_SUB_BLOCK = 256
HGRN_BLOCK = 512
ROW_TILE = 1024
SUB_ROWS = 256
MOD_COL_TILE = 1536
SAFE_LOG_DECAY = 80.0
VMEM_LIMIT_BYTES = 56 * 1024 * 1024


def _cparams(n_axes=1):
    return pltpu.CompilerParams(dimension_semantics=("arbitrary",) * n_axes,
                                vmem_limit_bytes=VMEM_LIMIT_BYTES)


def _const_spec(shape):
    nd = len(shape)
    return pl.BlockSpec(shape, lambda *_: (0,) * nd, pipeline_mode=pl.Buffered(1))


def _const_cols(arr, width, j):
    return pl.BlockSpec((arr.shape[0], width), lambda *_: (0, j), pipeline_mode=pl.Buffered(1))


def _sigmoid(x):
    return 1.0 / (1.0 + jnp.exp(-x))


def _layernorm(x, g, b):
    mu = jnp.mean(x, axis=-1, keepdims=True)
    xc = x - mu
    var = jnp.mean(xc * xc, axis=-1, keepdims=True)
    return xc * lax.rsqrt(var + LN_EPS) * g + b


def _dot(a, b):
    return jnp.dot(a, b, preferred_element_type=F32)


def _dot_nt(a, b):
    return lax.dot_general(a, b, (((1,), (1,)), ((), ())), preferred_element_type=F32)


def _dot_tn(a, b):
    return lax.dot_general(a, b, (((0,), (0,)), ((), ())), preferred_element_type=F32)


def _narrow_specs(arrays, nsteps, width=None, col=0):
    shape = lambda a: (a.shape[0] // nsteps, width or a.shape[1])
    return ([pl.BlockSpec(shape(a), lambda i, *_: (i, col)) for a in arrays],
            [pl.BlockSpec(shape(a), lambda i, *_: (i, 0)) for a in arrays],
            [jax.ShapeDtypeStruct((a.shape[0], width or a.shape[1]), BF16) for a in arrays])


def _narrow(src_refs, dst_refs):
    for s_ref, d_ref in zip(src_refs, dst_refs):
        d_ref[...] = s_ref[...].astype(BF16)


def _mod_kernel(cl_ref, cx_ref, w_ref, b_ref, win_ref, ol_ref, ox_ref, winb_ref):
    w = w_ref[...]
    for c_ref, o_ref in ((cl_ref, ol_ref), (cx_ref, ox_ref)):
        cs = c_ref[...]
        s = cs * _sigmoid(cs)
        o_ref[...] = jnp.sum(s * w, axis=0, keepdims=True) + b_ref[...]
    _narrow([win_ref], [winb_ref])


def _mod_vectors(c_lat, c_ctx, w_ada, b_ada, w_in):
    d, n = w_ada.shape
    tn = MOD_COL_TILE
    col = lambda j: (0, j)
    n_in, n_out, n_shape = _narrow_specs([w_in], n // tn, width=w_in.shape[1] // 2, col=0)
    return pl.pallas_call(
        _mod_kernel,
        grid=(n // tn,),
        in_specs=[_const_spec((d, 1)), _const_spec((d, 1)),
                  pl.BlockSpec((d, tn), col), pl.BlockSpec((1, tn), col)] + n_in,
        out_specs=[pl.BlockSpec((1, tn), col)] * 2 + n_out,
        out_shape=[jax.ShapeDtypeStruct((1, n), F32)] * 2 + n_shape,
        compiler_params=_cparams(),
        name="mod",
    )(c_lat.reshape(d, 1), c_ctx.reshape(d, 1), w_ada, b_ada, w_in)


def _ln_in_modulated(x_ref, er_ref, ec_ref, lng_ref, lnb_ref, mod_ref, r0, nr):
    x = x_ref[r0:r0 + nr, :]
    half = x.shape[1] // 2
    nrow = nr // GRID_W
    e0 = r0 // GRID_W
    left = jnp.concatenate(
        [jnp.broadcast_to(er_ref[e0 + r:e0 + r + 1, :], (GRID_W, half)) for r in range(nrow)], axis=0)
    right = jnp.concatenate([ec_ref[...]] * nrow, axis=0)
    xp = jnp.concatenate([x[:, :half] + left, x[:, half:] + right], axis=1)
    xl = _layernorm(xp, lng_ref[...], lnb_ref[...])
    hl = xl * (1.0 + mod_ref[1:2, :]) + mod_ref[0:1, :]
    return xl, hl


def _inproj_kernel(x_ref, er_ref, ec_ref, lng_ref, lnb_ref, mod_ref, w_ref, b_ref,
                   l0_ref, l1_ref, u_ref, q_ref, v_ref, g_ref, *, tm):
    w4 = N_GROUPS * GROUP_DIM
    l0 = l0_ref[...]
    l1 = l1_ref[...]
    m = jnp.maximum(l0, l1)
    e0 = jnp.exp(l0 - m)
    lb = e0 / (e0 + jnp.exp(l1 - m))
    sub = min(tm, SUB_ROWS)
    for r0 in range(0, tm, sub):
        rows = slice(r0, r0 + sub)
        _, hl = _ln_in_modulated(x_ref, er_ref, ec_ref, lng_ref, lnb_ref, mod_ref, r0, sub)
        hb = hl.astype(BF16)
        fp = _dot(hb, w_ref[:, 3 * w4:5 * w4]) + b_ref[:, 3 * w4:5 * w4]
        g_ref[rows, :] = jnp.log(lb + (1.0 - lb) * _sigmoid(fp))
        qp = _dot(hb, w_ref[:, w4:2 * w4]) + b_ref[:, w4:2 * w4]
        q_ref[rows, :] = (qp * _sigmoid(qp)).astype(BF16)
        u = (_dot(hb, w_ref[:, 0:w4]) + b_ref[:, 0:w4]).astype(BF16)
        for gi in range(N_GROUPS):
            u_ref[gi, rows, :] = u[:, gi * GROUP_DIM:(gi + 1) * GROUP_DIM]
        v_ref[rows, :] = (_dot(hb, w_ref[:, 2 * w4:3 * w4]) + b_ref[:, 2 * w4:3 * w4]).astype(BF16)


def _inproj(x, er, ec, lng, lnb, mod, w, b, l0, l1, tm):
    t, d = x.shape
    w4 = N_GROUPS * GROUP_DIM
    nrow = tm // GRID_W
    row = lambda i: (i, 0)
    return pl.pallas_call(
        functools.partial(_inproj_kernel, tm=tm),
        grid=(t // tm,),
        in_specs=[pl.BlockSpec((tm, d), row),
                  pl.BlockSpec((nrow, d // 2), row),
                  _const_spec(ec.shape), _const_spec(lng.shape), _const_spec(lnb.shape),
                  _const_spec(mod.shape), _const_cols(w, 5 * w4, 0), _const_cols(b, 5 * w4, 0),
                  _const_spec(l0.shape), _const_spec(l1.shape)],
        out_specs=[pl.BlockSpec((N_GROUPS, tm, GROUP_DIM), lambda i: (0, i, 0))]
        + [pl.BlockSpec((tm, w4), row)] * 2 + [pl.BlockSpec((tm, 2 * w4), row)],
        out_shape=[jax.ShapeDtypeStruct((N_GROUPS, t, GROUP_DIM), BF16)]
        + [jax.ShapeDtypeStruct((t, w4), BF16)] * 2 + [jax.ShapeDtypeStruct((t, 2 * w4), F32)],
        compiler_params=_cparams(),
        name="inproj",
    )(x, er, ec, lng, lnb, mod, w, b, l0, l1)


DFT_BATCH = 16
F32_SUBLANES = 8


def _dft1_kernel(cs_ref, m_ref, u_ref, a_ref, su_ref, sa_ref):
    r, nb, w = u_ref.shape
    hs = F32_SUBLANES
    u = u_ref[...].astype(F32)
    for h in range(nb // hs):
        su_ref[h] = u[:, h * hs:(h + 1) * hs, :].reshape(r * hs, w)
    cols = [su_ref[j // hs, pl.ds(j % hs, r, stride=hs), :].astype(BF16) for j in range(nb)]
    pq = _dot(cs_ref[...].astype(BF16), jnp.concatenate(cols, axis=1))
    for j in range(nb):
        sa_ref[j // hs, pl.ds(j % hs, 2 * r, stride=hs), :] = pq[:, j * w:(j + 1) * w]
    mat = m_ref[...].astype(BF16)
    parts = []
    for h in range(nb // hs):
        rows_pq = jnp.concatenate([sa_ref[h, 0:r * hs, :], sa_ref[h, r * hs:2 * r * hs, :]], axis=1)
        parts.append(_dot(rows_pq.astype(BF16), mat))
    a_ref[0] = jnp.concatenate([p[:, :w].reshape(r, hs, w) for p in parts], axis=1).astype(BF16)
    a_ref[1] = jnp.concatenate([p[:, w:].reshape(r, hs, w) for p in parts], axis=1).astype(BF16)


def _dft1(cs, mat, u):
    ng, t, w = u.shape
    r = RADIX
    nb = 2 * DFT_BATCH
    nh = nb // F32_SUBLANES
    return pl.pallas_call(
        _dft1_kernel,
        grid=(ng, r // nb),
        in_specs=[_const_spec(cs.shape), _const_spec(mat.shape),
                  pl.BlockSpec((None, r, nb, w), lambda g, o: (g, 0, o, 0))],
        out_specs=pl.BlockSpec((2, r, nb, w), lambda g, o: (0, 0, o, g)),
        out_shape=jax.ShapeDtypeStruct((2, r, r, ng * w), BF16),
        scratch_shapes=[pltpu.VMEM((nh, r * F32_SUBLANES, w), F32),
                        pltpu.VMEM((nh, 2 * r * F32_SUBLANES, w), F32)],
        compiler_params=_cparams(2),
        name="dft1",
    )(cs, mat, u.reshape(ng, r, r, w))


def _dft2_kernel(cs_ref, tw_ref, a_ref, o_ref, s_ref):
    nb, r, w = a_ref.shape[1], a_ref.shape[2], a_ref.shape[3]
    hs = F32_SUBLANES
    cos_a, sin_a = cs_ref[0], cs_ref[1]
    for j in range(nb):
        cos_b, sin_b = tw_ref[j:j + 1, 0:r], tw_ref[j:j + 1, r:2 * r]
        gmat = jnp.concatenate([cos_a * cos_b - sin_a * sin_b, sin_a * cos_b + cos_a * sin_b], axis=1)
        a = jnp.concatenate([a_ref[0, j], a_ref[1, j]], axis=0)
        y = _dot(gmat.astype(BF16), a)
        for gi in range(N_GROUPS):
            s_ref[gi, j // hs, pl.ds(j % hs, r, stride=hs), :] = y[:, gi * GROUP_DIM:(gi + 1) * GROUP_DIM]
    for gi in range(N_GROUPS):
        o_ref[gi] = jnp.concatenate([s_ref[gi, h].reshape(r, hs, GROUP_DIM) for h in range(nb // hs)],
                                    axis=1).astype(BF16)


def _dft2(base, twiddle, a4):
    _, r, _, w = a4.shape
    nb = DFT_BATCH
    out = pl.pallas_call(
        _dft2_kernel,
        grid=(r // nb,),
        in_specs=[_const_spec(base.shape),
                  pl.BlockSpec((nb, 2 * r), lambda k: (k, 0)),
                  pl.BlockSpec((2, nb, r, w), lambda k: (0, k, 0, 0))],
        out_specs=pl.BlockSpec((N_GROUPS, r, None, nb, GROUP_DIM), lambda k: (0, 0, k, 0, 0)),
        out_shape=jax.ShapeDtypeStruct((N_GROUPS, r, r // nb, nb, GROUP_DIM), BF16),
        scratch_shapes=[pltpu.VMEM((N_GROUPS, nb // F32_SUBLANES, r * F32_SUBLANES, GROUP_DIM), F32)],
        compiler_params=_cparams(),
        name="dft2",
    )(base, twiddle, a4)
    return out.reshape(N_GROUPS, r * r, GROUP_DIM)


def _hgrn_tables(L, reverse):
    nlev = int(np.log2(L))
    idx = np.arange(L)
    t = idx[:, None]
    i = idx[None, :]
    blocks = [(i >= t) if reverse else (i <= t)]
    for j in range(nlev):
        h = L >> (j + 1)
        mid = (t // (2 * h)) * (2 * h) + h
        upper = t >= mid
        if reverse:
            blk = np.where(upper, (i >= mid) & (i < t), (i >= t) & (i < mid))
        else:
            blk = np.where(upper, (i >= mid) & (i <= t), (i > t) & (i < mid))
        blocks.append(blk)
    blocks.append((i < t) if reverse else (i > t))
    return np.concatenate(blocks, axis=0).astype(np.float32), nlev


def _as_column(row):
    n = row.shape[1]
    return jnp.broadcast_to(row, (n, n)).T


def _split_hi_lo(g):
    hi = g.astype(BF16)
    return hi, (g - hi.astype(F32)).astype(BF16)


def _hgrn_chunk_exact(q_ref, v_ref, g_ref, o_ref, st_ref, rows, mall, pair_masks, query_rows,
                      L, nlev, reverse):
    last = 0 if reverse else L - 1
    g = g_ref[rows, :]
    g_hi, g_lo = _split_hi_lo(g)
    ex = jnp.exp(_dot(mall, g_hi) + _dot(mall, g_lo))
    q = q_ref[rows, :].astype(F32)
    v = v_ref[rows, :]
    k = 1.0 - jnp.exp(g)
    e_cum = ex[0:L]
    qe = (q * e_cum).astype(BF16)
    ke = (k * ex[(nlev + 1) * L:(nlev + 2) * L]).astype(BF16)
    zs = [(jnp.where(query_rows[j], q, k) * ex[(j + 1) * L:(j + 2) * L]).astype(BF16)
          for j in range(nlev)]
    qk = q * k
    e_last = e_cum[last:last + 1, :]
    for hd in range(N_HEADS):
        sl = slice(hd * HEAD_DIM, (hd + 1) * HEAD_DIM)
        sc = jnp.zeros((L, L), F32)
        for j in range(nlev):
            zj = zs[j][:, sl]
            sc = jnp.where(pair_masks[j], _dot_nt(zj, zj), sc)
        st = st_ref[hd]
        vh = v[:, sl]
        o = _dot(sc.astype(BF16), vh) + _dot(qe[:, sl], st.astype(BF16))
        o = o + jnp.sum(qk[:, sl], axis=-1, keepdims=True) * vh.astype(F32)
        o_ref[rows, sl] = o
        st_ref[hd] = st * _as_column(e_last[:, sl]) + _dot_tn(ke[:, sl], vh)


def _hgrn_block_fast(q_ref, v_ref, g_ref, trib_ref, o_ref, st_ref, r0, nchunk, L, reverse):
    half = L // 2
    tb = nchunk * L
    blk = slice(r0, r0 + tb)
    chunks = []
    for c in range(nchunk):
        base = c * L
        if reverse:
            chunks.append((slice(base, base + L), slice(base + half, base + L),
                           slice(base, base + half), base + half, base))
        else:
            chunks.append((slice(base, base + L), slice(base, base + half),
                           slice(base + half, base + L), base + half - 1, base + L - 1))
    g = g_ref[blk, :]
    b = _dot(trib_ref[...], g.astype(BF16))
    q = q_ref[blk, :].astype(F32)
    v = v_ref[blk, :]
    k = 1.0 - jnp.exp(g)
    e_b = jnp.exp(b)
    qe = (q * e_b).astype(BF16)
    c2 = [b[sec] - b[edge:edge + 1, :] for (_, _, sec, edge, _) in chunks]
    q2 = jnp.concatenate([q[ch[2]] * jnp.exp(c2[c]) for c, ch in enumerate(chunks)], axis=0)
    own = []
    for c, (_, fst, _, _, _) in enumerate(chunks):
        own += [c2[c], b[fst]] if reverse else [b[fst], c2[c]]
    kh = (k * jnp.exp(-jnp.concatenate(own, axis=0))).astype(BF16)
    tail = jnp.concatenate([b[last:last + 1, :] - b[rows] for (rows, _, _, _, last) in chunks], axis=0)
    ke = (k * jnp.exp(tail)).astype(BF16)
    lhs = jnp.concatenate([qe, q2.astype(BF16)], axis=0)

    row = lax.broadcasted_iota(jnp.int32, (tb, tb), 0)
    col = lax.broadcasted_iota(jnp.int32, (tb, tb), 1)
    valid = (row // L == col // L) & ((col >= row) if reverse else (col <= row))
    if reverse:
        use_near = (row % L < half) & (col % L < half)
    else:
        use_near = (row % L >= half) & (col % L >= half)

    for hd in range(N_HEADS):
        sl = slice(hd * HEAD_DIM, (hd + 1) * HEAD_DIM)
        s_all = _dot_nt(lhs[:, sl], kh[:, sl])
        far = s_all[0:tb]
        pieces = []
        for c, (_, fst, _, _, _) in enumerate(chunks):
            near_c = s_all[tb + c * half:tb + (c + 1) * half]
            pieces += [near_c, far[fst]] if reverse else [far[fst], near_c]
        near = jnp.concatenate(pieces, axis=0)
        sc = jnp.where(valid, jnp.where(use_near, near, far), 0.0).astype(BF16)
        vh = v[:, sl]
        o_intra = _dot(sc, vh)
        upd = [_dot_tn(ke[rows, sl], vh[rows]) for (rows, _, _, _, _) in chunks]
        st = st_ref[hd]
        o_inter = [None] * nchunk
        for c in (range(nchunk - 1, -1, -1) if reverse else range(nchunk)):
            rows, _, _, _, last = chunks[c]
            o_inter[c] = _dot(qe[rows, sl], st.astype(BF16))
            st = st * _as_column(e_b[last:last + 1, sl]) + upd[c]
        st_ref[hd] = st
        o_ref[blk, sl] = o_intra + jnp.concatenate(o_inter, axis=0)


def _hgrn_block_exact(q_ref, v_ref, g_ref, mall_ref, o_ref, st_ref, nchunk, L, nlev, reverse):
    row = lax.broadcasted_iota(jnp.int32, (L, L), 0)
    col = lax.broadcasted_iota(jnp.int32, (L, L), 1)
    rowc = lax.broadcasted_iota(jnp.int32, (L, N_HEADS * HEAD_DIM), 0)
    pair_masks, query_rows = [], []
    for j in range(nlev):
        h = L >> (j + 1)
        same = (row // (2 * h)) == (col // (2 * h))
        row_up = (row // h) % 2 == 1
        col_up = (col // h) % 2 == 1
        if reverse:
            pair_masks.append(same & jnp.logical_not(row_up) & col_up)
            query_rows.append((rowc // h) % 2 == 0)
        else:
            pair_masks.append(same & row_up & jnp.logical_not(col_up))
            query_rows.append((rowc // h) % 2 == 1)
    mall = mall_ref[...]
    for c in (range(nchunk - 1, -1, -1) if reverse else range(nchunk)):
        _hgrn_chunk_exact(q_ref, v_ref, g_ref, o_ref, st_ref, slice(c * L, (c + 1) * L),
                          mall, pair_masks, query_rows, L, nlev, reverse)


def _min_leaf_log_decay(g_ref, leaf):
    g = g_ref[...]
    return jnp.min(jnp.sum(g.reshape(g.shape[0] // leaf, leaf, g.shape[1]), axis=1))


def _hgrn_kernel(*refs, nsub, nchunk, L, nlev, n_narrow):
    (qf_ref, vf_ref, gf_ref, qb_ref, vb_ref, gb_ref, mallf_ref, mallb_ref,
     tribf_ref, tribb_ref, s0f_ref, s0b_ref) = refs[:12]
    narrow_in = refs[12:12 + n_narrow]
    of_ref, ob_ref, sff_ref, sfb_ref = refs[12 + n_narrow:16 + n_narrow]
    narrow_out = refs[16 + n_narrow:16 + 2 * n_narrow]
    stf_ref, stb_ref = refs[16 + 2 * n_narrow:]
    _narrow(narrow_in, narrow_out)
    i = pl.program_id(0)

    @pl.when(i == 0)
    def _():
        stf_ref[...] = s0f_ref[...]
        stb_ref[...] = s0b_ref[...]

    safe = jnp.minimum(_min_leaf_log_decay(gf_ref, L // 2),
                       _min_leaf_log_decay(gb_ref, L // 2)) >= -SAFE_LOG_DECAY

    @pl.when(safe)
    def _():
        for j in range(nsub):
            _hgrn_block_fast(qf_ref, vf_ref, gf_ref, tribf_ref, of_ref, stf_ref,
                             j * nchunk * L, nchunk, L, False)
            _hgrn_block_fast(qb_ref, vb_ref, gb_ref, tribb_ref, ob_ref, stb_ref,
                             (nsub - 1 - j) * nchunk * L, nchunk, L, True)

    @pl.when(jnp.logical_not(safe))
    def _():
        nc = nsub * nchunk
        _hgrn_block_exact(qf_ref, vf_ref, gf_ref, mallf_ref, of_ref, stf_ref, nc, L, nlev, False)
        _hgrn_block_exact(qb_ref, vb_ref, gb_ref, mallb_ref, ob_ref, stb_ref, nc, L, nlev, True)

    @pl.when(i == pl.num_programs(0) - 1)
    def _():
        sff_ref[...] = stf_ref[...]
        sfb_ref[...] = stb_ref[...]


def _hgrn_scan(q, v, g, s0_f, s0_b, tb, narrow=(), narrow_half1=()):
    t, w = q.shape
    L = HGRN_CHUNK
    nblk = t // tb
    sb = min(tb, HGRN_SUB_BLOCK)
    consts = []
    for reverse in (False, True):
        mall_np, nlev = _hgrn_tables(L, reverse)
        consts.append((jnp.asarray(mall_np, dtype=BF16),
                       jnp.asarray(np.kron(np.eye(sb // L, dtype=np.float32), mall_np[0:L]), dtype=BF16)))
    (mall_f, trib_f), (mall_b, trib_b) = consts
    fwd = lambda i: (i, 0)
    bwd = lambda i: (nblk - 1 - i, 0)
    bwd_g = lambda i: (nblk - 1 - i, 1)
    blk = lambda m: pl.BlockSpec((tb, w), m)
    n_in, n_out, n_shape = _narrow_specs(list(narrow), nblk)
    for a in narrow_half1:
        h_in, h_out, h_shape = _narrow_specs([a], nblk, width=a.shape[1] // 2, col=1)
        n_in, n_out, n_shape = n_in + h_in, n_out + h_out, n_shape + h_shape
    narrow = tuple(narrow) + tuple(narrow_half1)
    return pl.pallas_call(
        functools.partial(_hgrn_kernel, nsub=tb // sb, nchunk=sb // L, L=L, nlev=nlev,
                          n_narrow=len(narrow)),
        grid=(nblk,),
        in_specs=[blk(fwd), blk(fwd), blk(fwd), blk(bwd), blk(bwd), blk(bwd_g),
                  _const_spec(mall_f.shape), _const_spec(mall_b.shape),
                  _const_spec(trib_f.shape), _const_spec(trib_b.shape),
                  _const_spec(s0_f.shape), _const_spec(s0_b.shape)] + n_in,
        out_specs=[blk(fwd), blk(bwd)] + [pl.BlockSpec(s0_f.shape, lambda i: (0, 0, 0))] * 2 + n_out,
        out_shape=[jax.ShapeDtypeStruct((t, w), F32)] * 2
        + [jax.ShapeDtypeStruct(s0_f.shape, F32)] * 2 + n_shape,
        scratch_shapes=[pltpu.VMEM(s0_f.shape, F32)] * 2,
        compiler_params=_cparams(),
        name="hgrn",
    )(q, v, g, q, v, g, mall_f, mall_b, trib_f, trib_b, s0_f, s0_b, *narrow)


def _merge_kernel(x_ref, er_ref, ec_ref, lng_ref, lnb_ref, mod_ref, wg_ref, bg_ref, of_ref, ob_ref,
                  ng_ref, four_ref, wfp_ref, whp_ref, wo_ref, bo_ref, w1_ref, w2_ref,
                  o_ref, w1b_ref, w2b_ref, *, tm):
    _narrow([w1_ref, w2_ref], [w1b_ref, w2b_ref])
    w4 = N_HEADS * HEAD_DIM
    d = x_ref.shape[1]
    sub = min(tm, SUB_ROWS)
    for r0 in range(0, tm, sub):
        rows = slice(r0, r0 + sub)
        xl, hl = _ln_in_modulated(x_ref, er_ref, ec_ref, lng_ref, lnb_ref, mod_ref, r0, sub)
        hb = hl.astype(BF16)
        og = _dot(hb, wg_ref[:, 0:w4]) + bg_ref[:, 0:w4]
        o = of_ref[rows, :] + ob_ref[rows, :]
        parts = []
        for hd in range(N_HEADS):
            oh = o[:, hd * HEAD_DIM:(hd + 1) * HEAD_DIM]
            ms = jnp.mean(oh * oh, axis=-1, keepdims=True)
            parts.append(oh * lax.rsqrt(ms + RMS_EPS))
        on = jnp.concatenate(parts, axis=1) * ng_ref[...]
        oh = (on * (og * _sigmoid(og))).astype(BF16)
        g_four = _sigmoid(_dot(hb, wg_ref[:, w4:w4 + d]) + bg_ref[:, w4:w4 + d])
        four = jnp.concatenate([four_ref[gi, rows, :] for gi in range(N_GROUPS)], axis=1)
        y = g_four * _dot(four.astype(BF16), wfp_ref[...])
        g_hgrn = _sigmoid(_dot(hb, wg_ref[:, w4 + d:w4 + 2 * d]) + bg_ref[:, w4 + d:w4 + 2 * d])
        y = y + g_hgrn * _dot(oh, whp_ref[...])
        mix = _dot(y.astype(BF16), wo_ref[...]) + bo_ref[...]
        o_ref[rows, :] = ALPHA * xl + mod_ref[2:3, :] * mix


def _merge(x, er, ec, lng, lnb, mod, wg, bg, o_f, o_b, ng, four, wfp, whp, wo, bo, w1, w2, tm):
    t, d = x.shape
    w4 = N_HEADS * HEAD_DIM
    assert wg.shape == (d, w4 + 2 * d) and bg.shape == (1, 2 * wg.shape[1])
    nrow = tm // GRID_W
    row = lambda i: (i, 0)
    consts = [ec, lng, lnb, mod]
    consts2 = [wfp, whp, wo, bo]
    n_in, n_out, n_shape = _narrow_specs([w1, w2], t // tm)
    return pl.pallas_call(
        functools.partial(_merge_kernel, tm=tm),
        grid=(t // tm,),
        in_specs=[pl.BlockSpec((tm, d), row), pl.BlockSpec((nrow, d // 2), row)]
        + [_const_spec(a.shape) for a in consts]
        + [_const_spec(wg.shape), _const_cols(bg, w4 + 2 * d, 1)]
        + [pl.BlockSpec((tm, w4), row), pl.BlockSpec((tm, w4), row), _const_spec(ng.shape),
           pl.BlockSpec((N_GROUPS, tm, GROUP_DIM), lambda i: (0, i, 0))]
        + [_const_spec(a.shape) for a in consts2] + n_in,
        out_specs=[pl.BlockSpec((tm, d), row)] + n_out,
        out_shape=[jax.ShapeDtypeStruct((t, d), F32)] + n_shape,
        compiler_params=_cparams(),
        name="merge",
    )(x, er, *consts, wg, bg, o_f, o_b, ng, four, *consts2, w1, w2)


def _mlp_kernel(x_ref, mod_ref, ag_ref, ab_ref, w1_ref, b1_ref, w2_ref, b2_ref, pg_ref, pb_ref, o_ref,
                *, nsplit):
    dff = w1_ref.shape[1]
    cw = dff // nsplit
    tm = x_ref.shape[0]
    sub = min(tm, SUB_ROWS)
    for r0 in range(0, tm, sub):
        rows = slice(r0, r0 + sub)
        x1 = _layernorm(x_ref[rows, :], ag_ref[...], ab_ref[...])
        hb = (x1 * (1.0 + mod_ref[4:5, :]) + mod_ref[3:4, :]).astype(BF16)
        acc = jnp.zeros(x1.shape, F32)
        for c in range(nsplit):
            cs = slice(c * cw, (c + 1) * cw)
            a = jnp.maximum(_dot(hb, w1_ref[:, cs]) + b1_ref[:, cs], 0.0)
            acc = acc + _dot((a * a).astype(BF16), w2_ref[cs, :])
        m = acc + b2_ref[...]
        o_ref[rows, :] = _layernorm(ALPHA * x1 + mod_ref[5:6, :] * m, pg_ref[...], pb_ref[...])


def _mlp(z1, mod, ag, ab, w1, b1, w2, b2, pg, pb, tm):
    t, d = z1.shape
    row = lambda i: (i, 0)
    consts = [mod, ag, ab, w1, b1, w2, b2, pg, pb]
    return pl.pallas_call(
        functools.partial(_mlp_kernel, nsplit=4),
        grid=(t // tm,),
        in_specs=[pl.BlockSpec((tm, d), row)] + [_const_spec(a.shape) for a in consts],
        out_specs=pl.BlockSpec((tm, d), row),
        out_shape=jax.ShapeDtypeStruct((t, d), F32),
        compiler_params=_cparams(),
        name="mlp",
    )(z1, *consts)


def _pos_tables(rows, cols, dim):
    quarter = dim // 4
    omega = 1.0 / (POS_BASE ** (np.arange(quarter, dtype=np.float64) / quarter))
    r = np.arange(rows, dtype=np.float64)[:, None] * omega
    cc = np.arange(cols, dtype=np.float64)[:, None] * omega
    er = np.concatenate([np.sin(r), np.cos(r)], axis=-1)
    ec = np.concatenate([np.sin(cc), np.cos(cc)], axis=-1)
    return jnp.asarray(er.astype(np.float32)), jnp.asarray(ec.astype(np.float32))


def _dft_constants(t):
    n = RADIX
    kn = np.outer(np.arange(n), np.arange(n)).astype(np.float64)
    c = np.cos(2.0 * np.pi * kn / n)
    s = np.sin(2.0 * np.pi * kn / n)
    chan = np.concatenate([c, s], axis=0)
    stage1 = np.block([[c, -s], [-s, -c]])
    scale = 1.0 / np.sqrt(float(t) * GROUP_DIM)
    base2 = np.stack([c, s]) * scale
    beta = 2.0 * np.pi * kn / t
    twiddle = np.concatenate([np.cos(beta), np.sin(beta)], axis=1)
    as_f32 = lambda a: jnp.asarray(a.astype(np.float32))
    return as_f32(chan), as_f32(stage1), as_f32(base2), as_f32(twiddle)


def kernel(x, c, ctx, c_ctx, ln_in_g, ln_in_b, w_ada, b_ada, w_in, b_in, hgrn_lb_logits, hgrn_norm_g,
           w_four_proj, w_hgrn_proj, w_out, b_out, w_mlp1, b_mlp1, w_mlp2, b_mlp2, ln_post_g, ln_post_b):
    B, T, D = x.shape
    assert B == 1 and T == RADIX * RADIX and T % GRID_W == 0
    TC = ctx.shape[1]
    w4 = N_HEADS * HEAD_DIM
    row2 = lambda a: a.reshape(1, -1)

    assert w_in.shape[2] == 2 * 5 * w4
    mod_l, mod_c, w_a = _mod_vectors(c[0], c_ctx, w_ada[0], row2(b_ada[0]), w_in[0])
    mod_l = mod_l.reshape(6, D)
    mod_c = mod_c.reshape(6, D)

    er, ec = _pos_tables(T // GRID_W, GRID_W, D)
    lng, lnb = row2(ln_in_g), row2(ln_in_b)
    b_in2 = row2(b_in[0])
    b_a = b_g = b_in2
    l0 = hgrn_lb_logits[:, 0, :].reshape(1, 2 * w4)
    l1 = hgrn_lb_logits[:, 1, :].reshape(1, 2 * w4)
    dft_chan, dft_s1, dft_base2, dft_tw = _dft_constants(T)

    zc = jnp.zeros((TC // GRID_W, D // 2), F32)
    _, qc, vc, gc = _inproj(ctx[0], zc, jnp.zeros_like(ec), lng, lnb, mod_c, w_a, b_a, l0, l1, tm=TC)
    s_zero = jnp.zeros((N_HEADS, HEAD_DIM, HEAD_DIM), F32)
    _, _, s_f, s_b = _hgrn_scan(qc, vc, gc, s_zero, s_zero, tb=TC)

    u, q, v, g = _inproj(x[0], er, ec, lng, lnb, mod_l, w_a, b_a, l0, l1, tm=ROW_TILE)
    four = _dft2(dft_base2, dft_tw, _dft1(dft_chan, dft_s1, u))
    o_f, o_b, _, _, w_fp_b, w_hp_b, w_out_b, w_g = _hgrn_scan(
        q, v, g, s_f, s_b, tb=HGRN_BLOCK, narrow=(w_four_proj[0], w_hgrn_proj[0], w_out[0]),
        narrow_half1=(w_in[0],))

    z1, w1_b, w2_b = _merge(x[0], er, ec, lng, lnb, mod_l, w_g, b_g, o_f, o_b, row2(hgrn_norm_g[0]),
                            four, w_fp_b, w_hp_b, w_out_b, row2(b_out[0]), w_mlp1[0], w_mlp2[0],
                            tm=ROW_TILE)
    out = _mlp(z1, mod_l, row2(ln_post_g[0, 0]), row2(ln_post_b[0, 0]), w1_b, row2(b_mlp1[0]),
               w2_b, row2(b_mlp2[0]), row2(ln_post_g[0, 1]), row2(ln_post_b[0, 1]), tm=ROW_TILE)
    return out[None]
```

```python
import functools

import numpy as np
import jax
import jax.numpy as jnp
from jax import lax
from jax.experimental import pallas as pl
from jax.experimental.pallas import tpu as pltpu

F32 = jnp.float32
BF16 = jnp.bfloat16

GRID_W = 64
N_GROUPS = 4
GROUP_DIM = 128
N_HEADS = 4
HEAD_DIM = 128
POS_BASE = 10000.0
LN_EPS = 1e-5
RMS_EPS = 1e-6
DEPTH = 1
ALPHA = (2.0 * DEPTH) ** 0.25

RADIX = 128
HGRN_CHUNK = 64
HGRN_SUB_BLOCK = 256
HGRN_BLOCK = 512
ROW_TILE = 1024
SUB_ROWS = 256
MOD_COL_TILE = 1536
SAFE_LOG_DECAY = 80.0
VMEM_LIMIT_BYTES = 56 * 1024 * 1024


def _cparams(n_axes=1):
    return pltpu.CompilerParams(dimension_semantics=("arbitrary",) * n_axes,
                                vmem_limit_bytes=VMEM_LIMIT_BYTES)


def _const_spec(shape):
    nd = len(shape)
    return pl.BlockSpec(shape, lambda *_: (0,) * nd, pipeline_mode=pl.Buffered(1))


def _const_cols(arr, width, j):
    return pl.BlockSpec((arr.shape[0], width), lambda *_: (0, j), pipeline_mode=pl.Buffered(1))


def _sigmoid(x):
    return 1.0 / (1.0 + jnp.exp(-x))


def _layernorm(x, g, b):
    mu = jnp.mean(x, axis=-1, keepdims=True)
    xc = x - mu
    var = jnp.mean(xc * xc, axis=-1, keepdims=True)
    return xc * lax.rsqrt(var + LN_EPS) * g + b


def _dot(a, b):
    return jnp.dot(a, b, preferred_element_type=F32)


def _dot_nt(a, b):
    return lax.dot_general(a, b, (((1,), (1,)), ((), ())), preferred_element_type=F32)


def _dot_tn(a, b):
    return lax.dot_general(a, b, (((0,), (0,)), ((), ())), preferred_element_type=F32)


def _narrow_specs(arrays, nsteps, width=None, col=0):
    shape = lambda a: (a.shape[0] // nsteps, width or a.shape[1])
    return ([pl.BlockSpec(shape(a), lambda i, *_: (i, col)) for a in arrays],
            [pl.BlockSpec(shape(a), lambda i, *_: (i, 0)) for a in arrays],
            [jax.ShapeDtypeStruct((a.shape[0], width or a.shape[1]), BF16) for a in arrays])


def _narrow(src_refs, dst_refs):
    for s_ref, d_ref in zip(src_refs, dst_refs):
        d_ref[...] = s_ref[...].astype(BF16)


def _mod_kernel(cl_ref, cx_ref, w_ref, b_ref, win_ref, ol_ref, ox_ref, winb_ref):
    w = w_ref[...]
    for c_ref, o_ref in ((cl_ref, ol_ref), (cx_ref, ox_ref)):
        cs = c_ref[...]
        s = cs * _sigmoid(cs)
        o_ref[...] = jnp.sum(s * w, axis=0, keepdims=True) + b_ref[...]
    _narrow([win_ref], [winb_ref])


def _mod_vectors(c_lat, c_ctx, w_ada, b_ada, w_in):
    d, n = w_ada.shape
    tn = MOD_COL_TILE
    col = lambda j: (0, j)
    n_in, n_out, n_shape = _narrow_specs([w_in], n // tn, width=w_in.shape[1] // 2, col=0)
    return pl.pallas_call(
        _mod_kernel,
        grid=(n // tn,),
        in_specs=[_const_spec((d, 1)), _const_spec((d, 1)),
                  pl.BlockSpec((d, tn), col), pl.BlockSpec((1, tn), col)] + n_in,
        out_specs=[pl.BlockSpec((1, tn), col)] * 2 + n_out,
        out_shape=[jax.ShapeDtypeStruct((1, n), F32)] * 2 + n_shape,
        compiler_params=_cparams(),
        name="mod",
    )(c_lat.reshape(d, 1), c_ctx.reshape(d, 1), w_ada, b_ada, w_in)


def _ln_in_modulated(x_ref, er_ref, ec_ref, lng_ref, lnb_ref, mod_ref, r0, nr):
    x = x_ref[r0:r0 + nr, :]
    half = x.shape[1] // 2
    nrow = nr // GRID_W
    e0 = r0 // GRID_W
    left = jnp.concatenate(
        [jnp.broadcast_to(er_ref[e0 + r:e0 + r + 1, :], (GRID_W, half)) for r in range(nrow)], axis=0)
    right = jnp.concatenate([ec_ref[...]] * nrow, axis=0)
    xp = jnp.concatenate([x[:, :half] + left, x[:, half:] + right], axis=1)
    xl = _layernorm(xp, lng_ref[...], lnb_ref[...])
    hl = xl * (1.0 + mod_ref[1:2, :]) + mod_ref[0:1, :]
    return xl, hl


def _inproj_kernel(x_ref, er_ref, ec_ref, lng_ref, lnb_ref, mod_ref, w_ref, b_ref,
                   l0_ref, l1_ref, u_ref, q_ref, v_ref, g_ref, *, tm):
    w4 = N_GROUPS * GROUP_DIM
    l0 = l0_ref[...]
    l1 = l1_ref[...]
    m = jnp.maximum(l0, l1)
    e0 = jnp.exp(l0 - m)
    lb = e0 / (e0 + jnp.exp(l1 - m))
    sub = min(tm, SUB_ROWS)
    for r0 in range(0, tm, sub):
        rows = slice(r0, r0 + sub)
        _, hl = _ln_in_modulated(x_ref, er_ref, ec_ref, lng_ref, lnb_ref, mod_ref, r0, sub)
        hb = hl.astype(BF16)
        fp = _dot(hb, w_ref[:, 3 * w4:5 * w4]) + b_ref[:, 3 * w4:5 * w4]
        g_ref[rows, :] = jnp.log(lb + (1.0 - lb) * _sigmoid(fp))
        qp = _dot(hb, w_ref[:, w4:2 * w4]) + b_ref[:, w4:2 * w4]
        q_ref[rows, :] = (qp * _sigmoid(qp)).astype(BF16)
        u = (_dot(hb, w_ref[:, 0:w4]) + b_ref[:, 0:w4]).astype(BF16)
        for gi in range(N_GROUPS):
            u_ref[gi, rows, :] = u[:, gi * GROUP_DIM:(gi + 1) * GROUP_DIM]
        v_ref[rows, :] = (_dot(hb, w_ref[:, 2 * w4:3 * w4]) + b_ref[:, 2 * w4:3 * w4]).astype(BF16)


def _inproj(x, er, ec, lng, lnb, mod, w, b, l0, l1, tm):
    t, d = x.shape
    w4 = N_GROUPS * GROUP_DIM
    nrow = tm // GRID_W
    row = lambda i: (i, 0)
    return pl.pallas_call(
        functools.partial(_inproj_kernel, tm=tm),
        grid=(t // tm,),
        in_specs=[pl.BlockSpec((tm, d), row),
                  pl.BlockSpec((nrow, d // 2), row),
                  _const_spec(ec.shape), _const_spec(lng.shape), _const_spec(lnb.shape),
                  _const_spec(mod.shape), _const_cols(w, 5 * w4, 0), _const_cols(b, 5 * w4, 0),
                  _const_spec(l0.shape), _const_spec(l1.shape)],
        out_specs=[pl.BlockSpec((N_GROUPS, tm, GROUP_DIM), lambda i: (0, i, 0))]
        + [pl.BlockSpec((tm, w4), row)] * 2 + [pl.BlockSpec((tm, 2 * w4), row)],
        out_shape=[jax.ShapeDtypeStruct((N_GROUPS, t, GROUP_DIM), BF16)]
        + [jax.ShapeDtypeStruct((t, w4), BF16)] * 2 + [jax.ShapeDtypeStruct((t, 2 * w4), F32)],
        compiler_params=_cparams(),
        name="inproj",
    )(x, er, ec, lng, lnb, mod, w, b, l0, l1)


DFT_BATCH = 16
F32_SUBLANES = 8


def _dft1_kernel(cs_ref, m_ref, u_ref, a_ref, su_ref, sa_ref):
    r, nb, w = u_ref.shape
    hs = F32_SUBLANES
    u = u_ref[...].astype(F32)
    for h in range(nb // hs):
        su_ref[h] = u[:, h * hs:(h + 1) * hs, :].reshape(r * hs, w)
    cols = [su_ref[j // hs, pl.ds(j % hs, r, stride=hs), :].astype(BF16) for j in range(nb)]
    pq = _dot(cs_ref[...].astype(BF16), jnp.concatenate(cols, axis=1))
    for j in range(nb):
        sa_ref[j // hs, pl.ds(j % hs, 2 * r, stride=hs), :] = pq[:, j * w:(j + 1) * w]
    mat = m_ref[...].astype(BF16)
    parts = []
    for h in range(nb // hs):
        rows_pq = jnp.concatenate([sa_ref[h, 0:r * hs, :], sa_ref[h, r * hs:2 * r * hs, :]], axis=1)
        parts.append(_dot(rows_pq.astype(BF16), mat))
    a_ref[0] = jnp.concatenate([p[:, :w].reshape(r, hs, w) for p in parts], axis=1).astype(BF16)
    a_ref[1] = jnp.concatenate([p[:, w:].reshape(r, hs, w) for p in parts], axis=1).astype(BF16)


def _dft1(cs, mat, u):
    ng, t, w = u.shape
    r = RADIX
    nb = 2 * DFT_BATCH
    nh = nb // F32_SUBLANES
    return pl.pallas_call(
        _dft1_kernel,
        grid=(ng, r // nb),
        in_specs=[_const_spec(cs.shape), _const_spec(mat.shape),
                  pl.BlockSpec((None, r, nb, w), lambda g, o: (g, 0, o, 0))],
        out_specs=pl.BlockSpec((2, r, nb, w), lambda g, o: (0, 0, o, g)),
        out_shape=jax.ShapeDtypeStruct((2, r, r, ng * w), BF16),
        scratch_shapes=[pltpu.VMEM((nh, r * F32_SUBLANES, w), F32),
                        pltpu.VMEM((nh, 2 * r * F32_SUBLANES, w), F32)],
        compiler_params=_cparams(2),
        name="dft1",
    )(cs, mat, u.reshape(ng, r, r, w))


def _dft2_kernel(cs_ref, tw_ref, a_ref, o_ref, s_ref):
    nb, r, w = a_ref.shape[1], a_ref.shape[2], a_ref.shape[3]
    hs = F32_SUBLANES
    cos_a, sin_a = cs_ref[0], cs_ref[1]
    for j in range(nb):
        cos_b, sin_b = tw_ref[j:j + 1, 0:r], tw_ref[j:j + 1, r:2 * r]
        gmat = jnp.concatenate([cos_a * cos_b - sin_a * sin_b, sin_a * cos_b + cos_a * sin_b], axis=1)
        a = jnp.concatenate([a_ref[0, j], a_ref[1, j]], axis=0)
        y = _dot(gmat.astype(BF16), a)
        for gi in range(N_GROUPS):
            s_ref[gi, j // hs, pl.ds(j % hs, r, stride=hs), :] = y[:, gi * GROUP_DIM:(gi + 1) * GROUP_DIM]
    for gi in range(N_GROUPS):
        o_ref[gi] = jnp.concatenate([s_ref[gi, h].reshape(r, hs, GROUP_DIM) for h in range(nb // hs)],
                                    axis=1).astype(BF16)


def _dft2(base, twiddle, a4):
    _, r, _, w = a4.shape
    nb = DFT_BATCH
    out = pl.pallas_call(
        _dft2_kernel,
        grid=(r // nb,),
        in_specs=[_const_spec(base.shape),
                  pl.BlockSpec((nb, 2 * r), lambda k: (k, 0)),
                  pl.BlockSpec((2, nb, r, w), lambda k: (0, k, 0, 0))],
        out_specs=pl.BlockSpec((N_GROUPS, r, None, nb, GROUP_DIM), lambda k: (0, 0, k, 0, 0)),
        out_shape=jax.ShapeDtypeStruct((N_GROUPS, r, r // nb, nb, GROUP_DIM), BF16),
        scratch_shapes=[pltpu.VMEM((N_GROUPS, nb // F32_SUBLANES, r * F32_SUBLANES, GROUP_DIM), F32)],
        compiler_params=_cparams(),
        name="dft2",
    )(base, twiddle, a4)
    return out.reshape(N_GROUPS, r * r, GROUP_DIM)


def _hgrn_tables(L, reverse):
    nlev = int(np.log2(L))
    idx = np.arange(L)
    t = idx[:, None]
    i = idx[None, :]
    blocks = [(i >= t) if reverse else (i <= t)]
    for j in range(nlev):
        h = L >> (j + 1)
        mid = (t // (2 * h)) * (2 * h) + h
        upper = t >= mid
        if reverse:
            blk = np.where(upper, (i >= mid) & (i < t), (i >= t) & (i < mid))
        else:
            blk = np.where(upper, (i >= mid) & (i <= t), (i > t) & (i < mid))
        blocks.append(blk)
    blocks.append((i < t) if reverse else (i > t))
    return np.concatenate(blocks, axis=0).astype(np.float32), nlev


def _as_column(row):
    n = row.shape[1]
    return jnp.broadcast_to(row, (n, n)).T


def _split_hi_lo(g):
    hi = g.astype(BF16)
    return hi, (g - hi.astype(F32)).astype(BF16)


def _hgrn_chunk_exact(q_ref, v_ref, g_ref, o_ref, st_ref, rows, mall, pair_masks, query_rows,
                      L, nlev, reverse):
    last = 0 if reverse else L - 1
    g = g_ref[rows, :]
    g_hi, g_lo = _split_hi_lo(g)
    ex = jnp.exp(_dot(mall, g_hi) + _dot(mall, g_lo))
    q = q_ref[rows, :].astype(F32)
    v = v_ref[rows, :]
    k = 1.0 - jnp.exp(g)
    e_cum = ex[0:L]
    qe = (q * e_cum).astype(BF16)
    ke = (k * ex[(nlev + 1) * L:(nlev + 2) * L]).astype(BF16)
    zs = [(jnp.where(query_rows[j], q, k) * ex[(j + 1) * L:(j + 2) * L]).astype(BF16)
          for j in range(nlev)]
    qk = q * k
    e_last = e_cum[last:last + 1, :]
    for hd in range(N_HEADS):
        sl = slice(hd * HEAD_DIM, (hd + 1) * HEAD_DIM)
        sc = jnp.zeros((L, L), F32)
        for j in range(nlev):
            zj = zs[j][:, sl]
            sc = jnp.where(pair_masks[j], _dot_nt(zj, zj), sc)
        st = st_ref[hd]
        vh = v[:, sl]
        o = _dot(sc.astype(BF16), vh) + _dot(qe[:, sl], st.astype(BF16))
        o = o + jnp.sum(qk[:, sl], axis=-1, keepdims=True) * vh.astype(F32)
        o_ref[rows, sl] = o
        st_ref[hd] = st * _as_column(e_last[:, sl]) + _dot_tn(ke[:, sl], vh)


def _hgrn_block_fast(q_ref, v_ref, g_ref, trib_ref, o_ref, st_ref, r0, nchunk, L, reverse):
    half = L // 2
    tb = nchunk * L
    blk = slice(r0, r0 + tb)
    chunks = []
    for c in range(nchunk):
        base = c * L
        if reverse:
            chunks.append((slice(base, base + L), slice(base + half, base + L),
                           slice(base, base + half), base + half, base))
        else:
            chunks.append((slice(base, base + L), slice(base, base + half),
                           slice(base + half, base + L), base + half - 1, base + L - 1))
    g = g_ref[blk, :]
    b = _dot(trib_ref[...], g.astype(BF16))
    q = q_ref[blk, :].astype(F32)
    v = v_ref[blk, :]
    k = 1.0 - jnp.exp(g)
    e_b = jnp.exp(b)
    qe = (q * e_b).astype(BF16)
    c2 = [b[sec] - b[edge:edge + 1, :] for (_, _, sec, edge, _) in chunks]
    q2 = jnp.concatenate([q[ch[2]] * jnp.exp(c2[c]) for c, ch in enumerate(chunks)], axis=0)
    own = []
    for c, (_, fst, _, _, _) in enumerate(chunks):
        own += [c2[c], b[fst]] if reverse else [b[fst], c2[c]]
    kh = (k * jnp.exp(-jnp.concatenate(own, axis=0))).astype(BF16)
    tail = jnp.concatenate([b[last:last + 1, :] - b[rows] for (rows, _, _, _, last) in chunks], axis=0)
    ke = (k * jnp.exp(tail)).astype(BF16)
    lhs = jnp.concatenate([qe, q2.astype(BF16)], axis=0)

    row = lax.broadcasted_iota(jnp.int32, (tb, tb), 0)
    col = lax.broadcasted_iota(jnp.int32, (tb, tb), 1)
    valid = (row // L == col // L) & ((col >= row) if reverse else (col <= row))
    if reverse:
        use_near = (row % L < half) & (col % L < half)
    else:
        use_near = (row % L >= half) & (col % L >= half)

    for hd in range(N_HEADS):
        sl = slice(hd * HEAD_DIM, (hd + 1) * HEAD_DIM)
        s_all = _dot_nt(lhs[:, sl], kh[:, sl])
        far = s_all[0:tb]
        pieces = []
        for c, (_, fst, _, _, _) in enumerate(chunks):
            near_c = s_all[tb + c * half:tb + (c + 1) * half]
            pieces += [near_c, far[fst]] if reverse else [far[fst], near_c]
        near = jnp.concatenate(pieces, axis=0)
        sc = jnp.where(valid, jnp.where(use_near, near, far), 0.0).astype(BF16)
        vh = v[:, sl]
        o_intra = _dot(sc, vh)
        upd = [_dot_tn(ke[rows, sl], vh[rows]) for (rows, _, _, _, _) in chunks]
        st = st_ref[hd]
        o_inter = [None] * nchunk
        for c in (range(nchunk - 1, -1, -1) if reverse else range(nchunk)):
            rows, _, _, _, last = chunks[c]
            o_inter[c] = _dot(qe[rows, sl], st.astype(BF16))
            st = st * _as_column(e_b[last:last + 1, sl]) + upd[c]
        st_ref[hd] = st
        o_ref[blk, sl] = o_intra + jnp.concatenate(o_inter, axis=0)

    leaf_decay = []
    for (_, _, _, edge, last) in chunks:
        leaf_decay += [b[edge:edge + 1, :], b[last:last + 1, :] - b[edge:edge + 1, :]]
    return functools.reduce(jnp.minimum, leaf_decay)


def _hgrn_block_exact(q_ref, v_ref, g_ref, mall_ref, o_ref, st_ref, nchunk, L, nlev, reverse):
    row = lax.broadcasted_iota(jnp.int32, (L, L), 0)
    col = lax.broadcasted_iota(jnp.int32, (L, L), 1)
    rowc = lax.broadcasted_iota(jnp.int32, (L, N_HEADS * HEAD_DIM), 0)
    pair_masks, query_rows = [], []
    for j in range(nlev):
        h = L >> (j + 1)
        same = (row // (2 * h)) == (col // (2 * h))
        row_up = (row // h) % 2 == 1
        col_up = (col // h) % 2 == 1
        if reverse:
            pair_masks.append(same & jnp.logical_not(row_up) & col_up)
            query_rows.append((rowc // h) % 2 == 0)
        else:
            pair_masks.append(same & row_up & jnp.logical_not(col_up))
            query_rows.append((rowc // h) % 2 == 1)
    mall = mall_ref[...]
    for c in (range(nchunk - 1, -1, -1) if reverse else range(nchunk)):
        _hgrn_chunk_exact(q_ref, v_ref, g_ref, o_ref, st_ref, slice(c * L, (c + 1) * L),
                          mall, pair_masks, query_rows, L, nlev, reverse)


def _hgrn_kernel(*refs, nsub, nchunk, L, nlev, n_narrow):
    (qf_ref, vf_ref, gf_ref, qb_ref, vb_ref, gb_ref, mallf_ref, mallb_ref,
     tribf_ref, tribb_ref, s0f_ref, s0b_ref) = refs[:12]
    narrow_in = refs[12:12 + n_narrow]
    of_ref, ob_ref, sff_ref, sfb_ref = refs[12 + n_narrow:16 + n_narrow]
    narrow_out = refs[16 + n_narrow:16 + 2 * n_narrow]
    stf_ref, stb_ref, keepf_ref, keepb_ref = refs[16 + 2 * n_narrow:]
    _narrow(narrow_in, narrow_out)
    i = pl.program_id(0)

    @pl.when(i == 0)
    def _():
        stf_ref[...] = s0f_ref[...]
        stb_ref[...] = s0b_ref[...]

    keepf_ref[...] = stf_ref[...]
    keepb_ref[...] = stb_ref[...]
    leaf_decay = []
    for j in range(nsub):
        leaf_decay.append(_hgrn_block_fast(qf_ref, vf_ref, gf_ref, tribf_ref, of_ref, stf_ref,
                                           j * nchunk * L, nchunk, L, False))
        leaf_decay.append(_hgrn_block_fast(qb_ref, vb_ref, gb_ref, tribb_ref, ob_ref, stb_ref,
                                           (nsub - 1 - j) * nchunk * L, nchunk, L, True))
    safe = jnp.min(functools.reduce(jnp.minimum, leaf_decay)) >= -SAFE_LOG_DECAY

    @pl.when(jnp.logical_not(safe))
    def _():
        stf_ref[...] = keepf_ref[...]
        stb_ref[...] = keepb_ref[...]
        nc = nsub * nchunk
        _hgrn_block_exact(qf_ref, vf_ref, gf_ref, mallf_ref, of_ref, stf_ref, nc, L, nlev, False)
        _hgrn_block_exact(qb_ref, vb_ref, gb_ref, mallb_ref, ob_ref, stb_ref, nc, L, nlev, True)

    @pl.when(i == pl.num_programs(0) - 1)
    def _():
        sff_ref[...] = stf_ref[...]
        sfb_ref[...] = stb_ref[...]


def _hgrn_scan(q, v, g, s0_f, s0_b, tb, narrow=(), narrow_half1=()):
    t, w = q.shape
    L = HGRN_CHUNK
    nblk = t // tb
    sb = min(tb, HGRN_SUB_BLOCK)
    consts = []
    for reverse in (False, True):
        mall_np, nlev = _hgrn_tables(L, reverse)
        consts.append((jnp.asarray(mall_np, dtype=BF16),
                       jnp.asarray(np.kron(np.eye(sb // L, dtype=np.float32), mall_np[0:L]), dtype=BF16)))
    (mall_f, trib_f), (mall_b, trib_b) = consts
    fwd = lambda i: (i, 0)
    bwd = lambda i: (nblk - 1 - i, 0)
    bwd_g = lambda i: (nblk - 1 - i, 1)
    blk = lambda m: pl.BlockSpec((tb, w), m)
    n_in, n_out, n_shape = _narrow_specs(list(narrow), nblk)
    for a in narrow_half1:
        h_in, h_out, h_shape = _narrow_specs([a], nblk, width=a.shape[1] // 2, col=1)
        n_in, n_out, n_shape = n_in + h_in, n_out + h_out, n_shape + h_shape
    narrow = tuple(narrow) + tuple(narrow_half1)
    return pl.pallas_call(
        functools.partial(_hgrn_kernel, nsub=tb // sb, nchunk=sb // L, L=L, nlev=nlev,
                          n_narrow=len(narrow)),
        grid=(nblk,),
        in_specs=[blk(fwd), blk(fwd), blk(fwd), blk(bwd), blk(bwd), blk(bwd_g),
                  _const_spec(mall_f.shape), _const_spec(mall_b.shape),
                  _const_spec(trib_f.shape), _const_spec(trib_b.shape),
                  _const_spec(s0_f.shape), _const_spec(s0_b.shape)] + n_in,
        out_specs=[blk(fwd), blk(bwd)] + [pl.BlockSpec(s0_f.shape, lambda i: (0, 0, 0))] * 2 + n_out,
        out_shape=[jax.ShapeDtypeStruct((t, w), F32)] * 2
        + [jax.ShapeDtypeStruct(s0_f.shape, F32)] * 2 + n_shape,
        scratch_shapes=[pltpu.VMEM(s0_f.shape, F32)] * 4,
        compiler_params=_cparams(),
        name="hgrn",
    )(q, v, g, q, v, g, mall_f, mall_b, trib_f, trib_b, s0_f, s0_b, *narrow)


def _merge_kernel(x_ref, er_ref, ec_ref, lng_ref, lnb_ref, mod_ref, wg_ref, bg_ref, of_ref, ob_ref,
                  ng_ref, four_ref, wfp_ref, whp_ref, wo_ref, bo_ref, w1_ref, w2_ref,
                  o_ref, w1b_ref, w2b_ref, *, tm):
    _narrow([w1_ref, w2_ref], [w1b_ref, w2b_ref])
    w4 = N_HEADS * HEAD_DIM
    d = x_ref.shape[1]
    sub = min(tm, SUB_ROWS)
    for r0 in range(0, tm, sub):
        rows = slice(r0, r0 + sub)
        xl, hl = _ln_in_modulated(x_ref, er_ref, ec_ref, lng_ref, lnb_ref, mod_ref, r0, sub)
        hb = hl.astype(BF16)
        og = _dot(hb, wg_ref[:, 0:w4]) + bg_ref[:, 0:w4]
        o = of_ref[rows, :] + ob_ref[rows, :]
        parts = []
        for hd in range(N_HEADS):
            oh = o[:, hd * HEAD_DIM:(hd + 1) * HEAD_DIM]
            ms = jnp.mean(oh * oh, axis=-1, keepdims=True)
            parts.append(oh * lax.rsqrt(ms + RMS_EPS))
        on = jnp.concatenate(parts, axis=1) * ng_ref[...]
        oh = (on * (og * _sigmoid(og))).astype(BF16)
        g_four = _sigmoid(_dot(hb, wg_ref[:, w4:w4 + d]) + bg_ref[:, w4:w4 + d])
        four = jnp.concatenate([four_ref[gi, rows, :] for gi in range(N_GROUPS)], axis=1)
        y = g_four * _dot(four.astype(BF16), wfp_ref[...])
        g_hgrn = _sigmoid(_dot(hb, wg_ref[:, w4 + d:w4 + 2 * d]) + bg_ref[:, w4 + d:w4 + 2 * d])
        y = y + g_hgrn * _dot(oh, whp_ref[...])
        mix = _dot(y.astype(BF16), wo_ref[...]) + bo_ref[...]
        o_ref[rows, :] = ALPHA * xl + mod_ref[2:3, :] * mix


def _merge(x, er, ec, lng, lnb, mod, wg, bg, o_f, o_b, ng, four, wfp, whp, wo, bo, w1, w2, tm):
    t, d = x.shape
    w4 = N_HEADS * HEAD_DIM
    assert wg.shape == (d, w4 + 2 * d) and bg.shape == (1, 2 * wg.shape[1])
    nrow = tm // GRID_W
    row = lambda i: (i, 0)
    consts = [ec, lng, lnb, mod]
    consts2 = [wfp, whp, wo, bo]
    n_in, n_out, n_shape = _narrow_specs([w1, w2], t // tm)
    return pl.pallas_call(
        functools.partial(_merge_kernel, tm=tm),
        grid=(t // tm,),
        in_specs=[pl.BlockSpec((tm, d), row), pl.BlockSpec((nrow, d // 2), row)]
        + [_const_spec(a.shape) for a in consts]
        + [_const_spec(wg.shape), _const_cols(bg, w4 + 2 * d, 1)]
        + [pl.BlockSpec((tm, w4), row), pl.BlockSpec((tm, w4), row), _const_spec(ng.shape),
           pl.BlockSpec((N_GROUPS, tm, GROUP_DIM), lambda i: (0, i, 0))]
        + [_const_spec(a.shape) for a in consts2] + n_in,
        out_specs=[pl.BlockSpec((tm, d), row)] + n_out,
        out_shape=[jax.ShapeDtypeStruct((t, d), F32)] + n_shape,
        compiler_params=_cparams(),
        name="merge",
    )(x, er, *consts, wg, bg, o_f, o_b, ng, four, *consts2, w1, w2)


def _mlp_kernel(x_ref, mod_ref, ag_ref, ab_ref, w1_ref, b1_ref, w2_ref, b2_ref, pg_ref, pb_ref, o_ref,
                *, nsplit):
    dff = w1_ref.shape[1]
    cw = dff // nsplit
    tm = x_ref.shape[0]
    sub = min(tm, SUB_ROWS)
    for r0 in range(0, tm, sub):
        rows = slice(r0, r0 + sub)
        x1 = _layernorm(x_ref[rows, :], ag_ref[...], ab_ref[...])
        hb = (x1 * (1.0 + mod_ref[4:5, :]) + mod_ref[3:4, :]).astype(BF16)
        acc = jnp.zeros(x1.shape, F32)
        for c in range(nsplit):
            cs = slice(c * cw, (c + 1) * cw)
            a = jnp.maximum(_dot(hb, w1_ref[:, cs]) + b1_ref[:, cs], 0.0)
            acc = acc + _dot((a * a).astype(BF16), w2_ref[cs, :])
        m = acc + b2_ref[...]
        o_ref[rows, :] = _layernorm(ALPHA * x1 + mod_ref[5:6, :] * m, pg_ref[...], pb_ref[...])


def _mlp(z1, mod, ag, ab, w1, b1, w2, b2, pg, pb, tm):
    t, d = z1.shape
    row = lambda i: (i, 0)
    consts = [mod, ag, ab, w1, b1, w2, b2, pg, pb]
    return pl.pallas_call(
        functools.partial(_mlp_kernel, nsplit=4),
        grid=(t // tm,),
        in_specs=[pl.BlockSpec((tm, d), row)] + [_const_spec(a.shape) for a in consts],
        out_specs=pl.BlockSpec((tm, d), row),
        out_shape=jax.ShapeDtypeStruct((t, d), F32),
        compiler_params=_cparams(),
        name="mlp",
    )(z1, *consts)


def _pos_tables(rows, cols, dim):
    quarter = dim // 4
    omega = 1.0 / (POS_BASE ** (np.arange(quarter, dtype=np.float64) / quarter))
    r = np.arange(rows, dtype=np.float64)[:, None] * omega
    cc = np.arange(cols, dtype=np.float64)[:, None] * omega
    er = np.concatenate([np.sin(r), np.cos(r)], axis=-1)
    ec = np.concatenate([np.sin(cc), np.cos(cc)], axis=-1)
    return jnp.asarray(er.astype(np.float32)), jnp.asarray(ec.astype(np.float32))


def _dft_constants(t):
    n = RADIX
    kn = np.outer(np.arange(n), np.arange(n)).astype(np.float64)
    c = np.cos(2.0 * np.pi * kn / n)
    s = np.sin(2.0 * np.pi * kn / n)
    chan = np.concatenate([c, s], axis=0)
    stage1 = np.block([[c, -s], [-s, -c]])
    scale = 1.0 / np.sqrt(float(t) * GROUP_DIM)
    base2 = np.stack([c, s]) * scale
    beta = 2.0 * np.pi * kn / t
    twiddle = np.concatenate([np.cos(beta), np.sin(beta)], axis=1)
    as_f32 = lambda a: jnp.asarray(a.astype(np.float32))
    return as_f32(chan), as_f32(stage1), as_f32(base2), as_f32(twiddle)


def kernel(x, c, ctx, c_ctx, ln_in_g, ln_in_b, w_ada, b_ada, w_in, b_in, hgrn_lb_logits, hgrn_norm_g,
           w_four_proj, w_hgrn_proj, w_out, b_out, w_mlp1, b_mlp1, w_mlp2, b_mlp2, ln_post_g, ln_post_b):
    B, T, D = x.shape
    assert B == 1 and T == RADIX * RADIX and T % GRID_W == 0
    TC = ctx.shape[1]
    w4 = N_HEADS * HEAD_DIM
    row2 = lambda a: a.reshape(1, -1)

    assert w_in.shape[2] == 2 * 5 * w4
    mod_l, mod_c, w_a = _mod_vectors(c[0], c_ctx, w_ada[0], row2(b_ada[0]), w_in[0])
    mod_l = mod_l.reshape(6, D)
    mod_c = mod_c.reshape(6, D)

    er, ec = _pos_tables(T // GRID_W, GRID_W, D)
    lng, lnb = row2(ln_in_g), row2(ln_in_b)
    b_in2 = row2(b_in[0])
    b_a = b_g = b_in2
    l0 = hgrn_lb_logits[:, 0, :].reshape(1, 2 * w4)
    l1 = hgrn_lb_logits[:, 1, :].reshape(1, 2 * w4)
    dft_chan, dft_s1, dft_base2, dft_tw = _dft_constants(T)

    zc = jnp.zeros((TC // GRID_W, D // 2), F32)
    _, qc, vc, gc = _inproj(ctx[0], zc, jnp.zeros_like(ec), lng, lnb, mod_c, w_a, b_a, l0, l1, tm=TC)
    s_zero = jnp.zeros((N_HEADS, HEAD_DIM, HEAD_DIM), F32)
    _, _, s_f, s_b = _hgrn_scan(qc, vc, gc, s_zero, s_zero, tb=TC)

    u, q, v, g = _inproj(x[0], er, ec, lng, lnb, mod_l, w_a, b_a, l0, l1, tm=ROW_TILE)
    four = _dft2(dft_base2, dft_tw, _dft1(dft_chan, dft_s1, u))
    o_f, o_b, _, _, w_fp_b, w_hp_b, w_out_b, w_g = _hgrn_scan(
        q, v, g, s_f, s_b, tb=HGRN_BLOCK, narrow=(w_four_proj[0], w_hgrn_proj[0], w_out[0]),
        narrow_half1=(w_in[0],))

    z1, w1_b, w2_b = _merge(x[0], er, ec, lng, lnb, mod_l, w_g, b_g, o_f, o_b, row2(hgrn_norm_g[0]),
                            four, w_fp_b, w_hp_b, w_out_b, row2(b_out[0]), w_mlp1[0], w_mlp2[0],
                            tm=ROW_TILE)
    out = _mlp(z1, mod_l, row2(ln_post_g[0, 0]), row2(ln_post_b[0, 0]), w1_b, row2(b_mlp1[0]),
               w2_b, row2(b_mlp2[0]), row2(ln_post_g[0, 1]), row2(ln_post_b[0, 1]), tm=ROW_TILE)
    return out[None]
```

```python
import functools

import numpy as np
import jax
import jax.numpy as jnp
from jax import lax
from jax.experimental import pallas as pl
from jax.experimental.pallas import tpu as pltpu

F32 = jnp.float32
BF16 = jnp.bfloat16

GRID_W = 64
N_GROUPS = 4
GROUP_DIM = 128
N_HEADS = 4
HEAD_DIM = 128
POS_BASE = 10000.0
LN_EPS = 1e-5
RMS_EPS = 1e-6
DEPTH = 1
ALPHA = (2.0 * DEPTH) ** 0.25

RADIX = 128
HGRN_CHUNK = 64
HGRN_SUB_BLOCK = 256
HGRN_BLOCK = 512
ROW_TILE = 1024
SUB_ROWS = 256
MOD_COL_TILE = 1536
SAFE_LOG_DECAY = 80.0
VMEM_LIMIT_BYTES = 56 * 1024 * 1024


def _cparams(n_axes=1):
    return pltpu.CompilerParams(dimension_semantics=("arbitrary",) * n_axes,
                                vmem_limit_bytes=VMEM_LIMIT_BYTES)


def _const_spec(shape):
    nd = len(shape)
    return pl.BlockSpec(shape, lambda *_: (0,) * nd, pipeline_mode=pl.Buffered(1))


def _const_cols(arr, width, j):
    return pl.BlockSpec((arr.shape[0], width), lambda *_: (0, j), pipeline_mode=pl.Buffered(1))


def _sigmoid(x):
    return 1.0 / (1.0 + jnp.exp(-x))


def _layernorm(x, g, b):
    mu = jnp.mean(x, axis=-1, keepdims=True)
    xc = x - mu
    var = jnp.mean(xc * xc, axis=-1, keepdims=True)
    return xc * lax.rsqrt(var + LN_EPS) * g + b


def _dot(a, b):
    return jnp.dot(a, b, preferred_element_type=F32)


def _dot_nt(a, b):
    return lax.dot_general(a, b, (((1,), (1,)), ((), ())), preferred_element_type=F32)


def _dot_tn(a, b):
    return lax.dot_general(a, b, (((0,), (0,)), ((), ())), preferred_element_type=F32)


def _narrow_specs(arrays, nsteps, width=None, col=0):
    shape = lambda a: (a.shape[0] // nsteps, width or a.shape[1])
    return ([pl.BlockSpec(shape(a), lambda i, *_: (i, col)) for a in arrays],
            [pl.BlockSpec(shape(a), lambda i, *_: (i, 0)) for a in arrays],
            [jax.ShapeDtypeStruct((a.shape[0], width or a.shape[1]), BF16) for a in arrays])


def _narrow(src_refs, dst_refs):
    for s_ref, d_ref in zip(src_refs, dst_refs):
        d_ref[...] = s_ref[...].astype(BF16)


def _mod_kernel(cl_ref, cx_ref, w_ref, b_ref, win_ref, ol_ref, ox_ref, winb_ref):
    w = w_ref[...]
    for c_ref, o_ref in ((cl_ref, ol_ref), (cx_ref, ox_ref)):
        cs = c_ref[...]
        s = cs * _sigmoid(cs)
        o_ref[...] = jnp.sum(s * w, axis=0, keepdims=True) + b_ref[...]
    _narrow([win_ref], [winb_ref])


def _mod_vectors(c_lat, c_ctx, w_ada, b_ada, w_in):
    d, n = w_ada.shape
    tn = MOD_COL_TILE
    col = lambda j: (0, j)
    n_in, n_out, n_shape = _narrow_specs([w_in], n // tn, width=w_in.shape[1] // 2, col=0)
    return pl.pallas_call(
        _mod_kernel,
        grid=(n // tn,),
        in_specs=[_const_spec((d, 1)), _const_spec((d, 1)),
                  pl.BlockSpec((d, tn), col), pl.BlockSpec((1, tn), col)] + n_in,
        out_specs=[pl.BlockSpec((1, tn), col)] * 2 + n_out,
        out_shape=[jax.ShapeDtypeStruct((1, n), F32)] * 2 + n_shape,
        compiler_params=_cparams(),
        name="mod",
    )(c_lat.reshape(d, 1), c_ctx.reshape(d, 1), w_ada, b_ada, w_in)


def _ln_in_modulated(x_ref, er_ref, ec_ref, lng_ref, lnb_ref, mod_ref, r0, nr):
    x = x_ref[r0:r0 + nr, :]
    half = x.shape[1] // 2
    nrow = nr // GRID_W
    e0 = r0 // GRID_W
    left = jnp.concatenate(
        [jnp.broadcast_to(er_ref[e0 + r:e0 + r + 1, :], (GRID_W, half)) for r in range(nrow)], axis=0)
    right = jnp.concatenate([ec_ref[...]] * nrow, axis=0)
    xp = jnp.concatenate([x[:, :half] + left, x[:, half:] + right], axis=1)
    xl = _layernorm(xp, lng_ref[...], lnb_ref[...])
    hl = xl * (1.0 + mod_ref[1:2, :]) + mod_ref[0:1, :]
    return xl, hl


def _inproj_kernel(x_ref, er_ref, ec_ref, lng_ref, lnb_ref, mod_ref, w_ref, b_ref,
                   l0_ref, l1_ref, u_ref, q_ref, v_ref, g_ref, *, tm):
    w4 = N_GROUPS * GROUP_DIM
    l0 = l0_ref[...]
    l1 = l1_ref[...]
    m = jnp.maximum(l0, l1)
    e0 = jnp.exp(l0 - m)
    lb = e0 / (e0 + jnp.exp(l1 - m))
    sub = min(tm, SUB_ROWS)
    for r0 in range(0, tm, sub):
        rows = slice(r0, r0 + sub)
        _, hl = _ln_in_modulated(x_ref, er_ref, ec_ref, lng_ref, lnb_ref, mod_ref, r0, sub)
        hb = hl.astype(BF16)
        fp = _dot(hb, w_ref[:, 3 * w4:5 * w4]) + b_ref[:, 3 * w4:5 * w4]
        g_ref[rows, :] = jnp.log(lb + (1.0 - lb) * _sigmoid(fp))
        qp = _dot(hb, w_ref[:, w4:2 * w4]) + b_ref[:, w4:2 * w4]
        q_ref[rows, :] = (qp * _sigmoid(qp)).astype(BF16)
        u = (_dot(hb, w_ref[:, 0:w4]) + b_ref[:, 0:w4]).astype(BF16)
        for gi in range(N_GROUPS):
            u_ref[gi, rows, :] = u[:, gi * GROUP_DIM:(gi + 1) * GROUP_DIM]
        v_ref[rows, :] = (_dot(hb, w_ref[:, 2 * w4:3 * w4]) + b_ref[:, 2 * w4:3 * w4]).astype(BF16)


def _inproj(x, er, ec, lng, lnb, mod, w, b, l0, l1, tm):
    t, d = x.shape
    w4 = N_GROUPS * GROUP_DIM
    nrow = tm // GRID_W
    row = lambda i: (i, 0)
    return pl.pallas_call(
        functools.partial(_inproj_kernel, tm=tm),
        grid=(t // tm,),
        in_specs=[pl.BlockSpec((tm, d), row),
                  pl.BlockSpec((nrow, d // 2), row),
                  _const_spec(ec.shape), _const_spec(lng.shape), _const_spec(lnb.shape),
                  _const_spec(mod.shape), _const_cols(w, 5 * w4, 0), _const_cols(b, 5 * w4, 0),
                  _const_spec(l0.shape), _const_spec(l1.shape)],
        out_specs=[pl.BlockSpec((N_GROUPS, tm, GROUP_DIM), lambda i: (0, i, 0))]
        + [pl.BlockSpec((tm, w4), row)] * 2 + [pl.BlockSpec((tm, 2 * w4), row)],
        out_shape=[jax.ShapeDtypeStruct((N_GROUPS, t, GROUP_DIM), BF16)]
        + [jax.ShapeDtypeStruct((t, w4), BF16)] * 2 + [jax.ShapeDtypeStruct((t, 2 * w4), F32)],
        compiler_params=_cparams(),
        name="inproj",
    )(x, er, ec, lng, lnb, mod, w, b, l0, l1)


DFT_BATCH = 16
F32_SUBLANES = 8


def _dft1_kernel(cs_ref, m_ref, u_ref, a_ref, su_ref, sa_ref):
    r, nb, w = u_ref.shape
    hs = F32_SUBLANES
    u = u_ref[...].astype(F32)
    for h in range(nb // hs):
        su_ref[h] = u[:, h * hs:(h + 1) * hs, :].reshape(r * hs, w)
    cols = [su_ref[j // hs, pl.ds(j % hs, r, stride=hs), :].astype(BF16) for j in range(nb)]
    pq = _dot(cs_ref[...].astype(BF16), jnp.concatenate(cols, axis=1))
    for j in range(nb):
        sa_ref[j // hs, pl.ds(j % hs, 2 * r, stride=hs), :] = pq[:, j * w:(j + 1) * w]
    mat = m_ref[...].astype(BF16)
    parts = []
    for h in range(nb // hs):
        rows_pq = jnp.concatenate([sa_ref[h, 0:r * hs, :], sa_ref[h, r * hs:2 * r * hs, :]], axis=1)
        parts.append(_dot(rows_pq.astype(BF16), mat))
    a_ref[0] = jnp.concatenate([p[:, :w].reshape(r, hs, w) for p in parts], axis=1).astype(BF16)
    a_ref[1] = jnp.concatenate([p[:, w:].reshape(r, hs, w) for p in parts], axis=1).astype(BF16)


def _dft1(cs, mat, u):
    ng, t, w = u.shape
    r = RADIX
    nb = 2 * DFT_BATCH
    nh = nb // F32_SUBLANES
    return pl.pallas_call(
        _dft1_kernel,
        grid=(ng, r // nb),
        in_specs=[_const_spec(cs.shape), _const_spec(mat.shape),
                  pl.BlockSpec((None, r, nb, w), lambda g, o: (g, 0, o, 0))],
        out_specs=pl.BlockSpec((2, r, nb, w), lambda g, o: (0, 0, o, g)),
        out_shape=jax.ShapeDtypeStruct((2, r, r, ng * w), BF16),
        scratch_shapes=[pltpu.VMEM((nh, r * F32_SUBLANES, w), F32),
                        pltpu.VMEM((nh, 2 * r * F32_SUBLANES, w), F32)],
        compiler_params=_cparams(2),
        name="dft1",
    )(cs, mat, u.reshape(ng, r, r, w))


def _dft2_kernel(cs_ref, tw_ref, a_ref, o_ref, s_ref):
    nb, r, w = a_ref.shape[1], a_ref.shape[2], a_ref.shape[3]
    hs = F32_SUBLANES
    cos_a, sin_a = cs_ref[0], cs_ref[1]
    for j in range(nb):
        cos_b, sin_b = tw_ref[j:j + 1, 0:r], tw_ref[j:j + 1, r:2 * r]
        gmat = jnp.concatenate([cos_a * cos_b - sin_a * sin_b, sin_a * cos_b + cos_a * sin_b], axis=1)
        a = jnp.concatenate([a_ref[0, j], a_ref[1, j]], axis=0)
        y = _dot(gmat.astype(BF16), a)
        for gi in range(N_GROUPS):
            s_ref[gi, j // hs, pl.ds(j % hs, r, stride=hs), :] = y[:, gi * GROUP_DIM:(gi + 1) * GROUP_DIM]
    for gi in range(N_GROUPS):
        o_ref[gi] = jnp.concatenate([s_ref[gi, h].reshape(r, hs, GROUP_DIM) for h in range(nb // hs)],
                                    axis=1).astype(BF16)


def _dft2(base, twiddle, a4):
    _, r, _, w = a4.shape
    nb = DFT_BATCH
    out = pl.pallas_call(
        _dft2_kernel,
        grid=(r // nb,),
        in_specs=[_const_spec(base.shape),
                  pl.BlockSpec((nb, 2 * r), lambda k: (k, 0)),
                  pl.BlockSpec((2, nb, r, w), lambda k: (0, k, 0, 0))],
        out_specs=pl.BlockSpec((N_GROUPS, r, None, nb, GROUP_DIM), lambda k: (0, 0, k, 0, 0)),
        out_shape=jax.ShapeDtypeStruct((N_GROUPS, r, r // nb, nb, GROUP_DIM), BF16),
        scratch_shapes=[pltpu.VMEM((N_GROUPS, nb // F32_SUBLANES, r * F32_SUBLANES, GROUP_DIM), F32)],
        compiler_params=_cparams(),
        name="dft2",
    )(base, twiddle, a4)
    return out.reshape(N_GROUPS, r * r, GROUP_DIM)


def _hgrn_tables(L, reverse):
    nlev = int(np.log2(L))
    idx = np.arange(L)
    t = idx[:, None]
    i = idx[None, :]
    blocks = [(i >= t) if reverse else (i <= t)]
    for j in range(nlev):
        h = L >> (j + 1)
        mid = (t // (2 * h)) * (2 * h) + h
        upper = t >= mid
        if reverse:
            blk = np.where(upper, (i >= mid) & (i < t), (i >= t) & (i < mid))
        else:
            blk = np.where(upper, (i >= mid) & (i <= t), (i > t) & (i < mid))
        blocks.append(blk)
    blocks.append((i < t) if reverse else (i > t))
    return np.concatenate(blocks, axis=0).astype(np.float32), nlev


def _as_column(row):
    n = row.shape[1]
    return jnp.broadcast_to(row, (n, n)).T


def _split_hi_lo(g):
    hi = g.astype(BF16)
    return hi, (g - hi.astype(F32)).astype(BF16)


def _hgrn_chunk_exact(q_ref, v_ref, g_ref, o_ref, st_ref, rows, mall, pair_masks, query_rows,
                      L, nlev, reverse):
    last = 0 if reverse else L - 1
    g = g_ref[rows, :]
    g_hi, g_lo = _split_hi_lo(g)
    ex = jnp.exp(_dot(mall, g_hi) + _dot(mall, g_lo))
    q = q_ref[rows, :].astype(F32)
    v = v_ref[rows, :]
    k = 1.0 - jnp.exp(g)
    e_cum = ex[0:L]
    qe = (q * e_cum).astype(BF16)
    ke = (k * ex[(nlev + 1) * L:(nlev + 2) * L]).astype(BF16)
    zs = [(jnp.where(query_rows[j], q, k) * ex[(j + 1) * L:(j + 2) * L]).astype(BF16)
          for j in range(nlev)]
    qk = q * k
    e_last = e_cum[last:last + 1, :]
    for hd in range(N_HEADS):
        sl = slice(hd * HEAD_DIM, (hd + 1) * HEAD_DIM)
        sc = jnp.zeros((L, L), F32)
        for j in range(nlev):
            zj = zs[j][:, sl]
            sc = jnp.where(pair_masks[j], _dot_nt(zj, zj), sc)
        st = st_ref[hd]
        vh = v[:, sl]
        o = _dot(sc.astype(BF16), vh) + _dot(qe[:, sl], st.astype(BF16))
        o = o + jnp.sum(qk[:, sl], axis=-1, keepdims=True) * vh.astype(F32)
        o_ref[rows, sl] = o
        st_ref[hd] = st * _as_column(e_last[:, sl]) + _dot_tn(ke[:, sl], vh)


def _hgrn_block_fast(q_ref, v_ref, g_ref, trib_ref, o_ref, st_ref, r0, nchunk, L, reverse):
    half = L // 2
    tb = nchunk * L
    blk = slice(r0, r0 + tb)
    chunks = []
    for c in range(nchunk):
        base = c * L
        if reverse:
            chunks.append((slice(base, base + L), slice(base + half, base + L),
                           slice(base, base + half), base + half, base))
        else:
            chunks.append((slice(base, base + L), slice(base, base + half),
                           slice(base + half, base + L), base + half - 1, base + L - 1))
    g = g_ref[blk, :]
    b = _dot(trib_ref[...], g.astype(BF16))
    q = q_ref[blk, :].astype(F32)
    v = v_ref[blk, :]
    k = 1.0 - jnp.exp(g)
    e_b = jnp.exp(b)
    qe = (q * e_b).astype(BF16)
    c2 = [b[sec] - b[edge:edge + 1, :] for (_, _, sec, edge, _) in chunks]
    q2 = jnp.concatenate([q[ch[2]] * jnp.exp(c2[c]) for c, ch in enumerate(chunks)], axis=0)
    own = []
    for c, (_, fst, _, _, _) in enumerate(chunks):
        own += [c2[c], b[fst]] if reverse else [b[fst], c2[c]]
    kh = (k * jnp.exp(-jnp.concatenate(own, axis=0))).astype(BF16)
    tail = jnp.concatenate([b[last:last + 1, :] - b[rows] for (rows, _, _, _, last) in chunks], axis=0)
    ke = (k * jnp.exp(tail)).astype(BF16)
    lhs = jnp.concatenate([qe, q2.astype(BF16)], axis=0)

    row = lax.broadcasted_iota(jnp.int32, (tb, tb), 0)
    col = lax.broadcasted_iota(jnp.int32, (tb, tb), 1)
    valid = (row // L == col // L) & ((col >= row) if reverse else (col <= row))
    if reverse:
        use_near = (row % L < half) & (col % L < half)
    else:
        use_near = (row % L >= half) & (col % L >= half)

    for hd in range(N_HEADS):
        sl = slice(hd * HEAD_DIM, (hd + 1) * HEAD_DIM)
        s_all = _dot_nt(lhs[:, sl], kh[:, sl])
        far = s_all[0:tb]
        pieces = []
        for c, (_, fst, _, _, _) in enumerate(chunks):
            near_c = s_all[tb + c * half:tb + (c + 1) * half]
            pieces += [near_c, far[fst]] if reverse else [far[fst], near_c]
        near = jnp.concatenate(pieces, axis=0)
        sc = jnp.where(valid, jnp.where(use_near, near, far), 0.0).astype(BF16)
        vh = v[:, sl]
        o_intra = _dot(sc, vh)
        upd = [_dot_tn(ke[rows, sl], vh[rows]) for (rows, _, _, _, _) in chunks]
        st = st_ref[hd]
        o_inter = [None] * nchunk
        for c in (range(nchunk - 1, -1, -1) if reverse else range(nchunk)):
            rows, _, _, _, last = chunks[c]
            o_inter[c] = _dot(qe[rows, sl], st.astype(BF16))
            st = st * _as_column(e_b[last:last + 1, sl]) + upd[c]
        st_ref[hd] = st
        o_ref[blk, sl] = o_intra + jnp.concatenate(o_inter, axis=0)

    leaf_decay = []
    for (_, _, _, edge, last) in chunks:
        leaf_decay += [b[edge:edge + 1, :], b[last:last + 1, :] - b[edge:edge + 1, :]]
    return functools.reduce(jnp.minimum, leaf_decay)


def _hgrn_block_exact(q_ref, v_ref, g_ref, mall_ref, o_ref, st_ref, nchunk, L, nlev, reverse):
    row = lax.broadcasted_iota(jnp.int32, (L, L), 0)
    col = lax.broadcasted_iota(jnp.int32, (L, L), 1)
    rowc = lax.broadcasted_iota(jnp.int32, (L, N_HEADS * HEAD_DIM), 0)
    pair_masks, query_rows = [], []
    for j in range(nlev):
        h = L >> (j + 1)
        same = (row // (2 * h)) == (col // (2 * h))
        row_up = (row // h) % 2 == 1
        col_up = (col // h) % 2 == 1
        if reverse:
            pair_masks.append(same & jnp.logical_not(row_up) & col_up)
            query_rows.append((rowc // h) % 2 == 0)
        else:
            pair_masks.append(same & row_up & jnp.logical_not(col_up))
            query_rows.append((rowc // h) % 2 == 1)
    mall = mall_ref[...]

    def chunk(t, carry):
        c = nchunk - 1 - t if reverse else t
        _hgrn_chunk_exact(q_ref, v_ref, g_ref, o_ref, st_ref, pl.ds(pl.multiple_of(c * L, L), L),
                          mall, pair_masks, query_rows, L, nlev, reverse)
        return carry

    lax.fori_loop(0, nchunk, chunk, 0)


def _hgrn_kernel(*refs, nsub, nchunk, L, nlev, n_narrow):
    (qf_ref, vf_ref, gf_ref, qb_ref, vb_ref, gb_ref, mallf_ref, mallb_ref,
     tribf_ref, tribb_ref, s0f_ref, s0b_ref) = refs[:12]
    narrow_in = refs[12:12 + n_narrow]
    of_ref, ob_ref, sff_ref, sfb_ref = refs[12 + n_narrow:16 + n_narrow]
    narrow_out = refs[16 + n_narrow:16 + 2 * n_narrow]
    stf_ref, stb_ref, keepf_ref, keepb_ref = refs[16 + 2 * n_narrow:]
    _narrow(narrow_in, narrow_out)
    i = pl.program_id(0)

    @pl.when(i == 0)
    def _():
        stf_ref[...] = s0f_ref[...]
        stb_ref[...] = s0b_ref[...]

    keepf_ref[...] = stf_ref[...]
    keepb_ref[...] = stb_ref[...]
    leaf_decay = []
    for j in range(nsub):
        leaf_decay.append(_hgrn_block_fast(qf_ref, vf_ref, gf_ref, tribf_ref, of_ref, stf_ref,
                                           j * nchunk * L, nchunk, L, False))
        leaf_decay.append(_hgrn_block_fast(qb_ref, vb_ref, gb_ref, tribb_ref, ob_ref, stb_ref,
                                           (nsub - 1 - j) * nchunk * L, nchunk, L, True))
    safe = jnp.min(functools.reduce(jnp.minimum, leaf_decay)) >= -SAFE_LOG_DECAY

    @pl.when(jnp.logical_not(safe))
    def _():
        stf_ref[...] = keepf_ref[...]
        stb_ref[...] = keepb_ref[...]
        nc = nsub * nchunk
        _hgrn_block_exact(qf_ref, vf_ref, gf_ref, mallf_ref, of_ref, stf_ref, nc, L, nlev, False)
        _hgrn_block_exact(qb_ref, vb_ref, gb_ref, mallb_ref, ob_ref, stb_ref, nc, L, nlev, True)

    @pl.when(i == pl.num_programs(0) - 1)
    def _():
        sff_ref[...] = stf_ref[...]
        sfb_ref[...] = stb_ref[...]


def _hgrn_scan(q, v, g, s0_f, s0_b, tb, narrow=(), narrow_half1=()):
    t, w = q.shape
    L = HGRN_CHUNK
    nblk = t // tb
    sb = min(tb, HGRN_SUB_BLOCK)
    consts = []
    for reverse in (False, True):
        mall_np, nlev = _hgrn_tables(L, reverse)
        consts.append((jnp.asarray(mall_np, dtype=BF16),
                       jnp.asarray(np.kron(np.eye(sb // L, dtype=np.float32), mall_np[0:L]), dtype=BF16)))
    (mall_f, trib_f), (mall_b, trib_b) = consts
    fwd = lambda i: (i, 0)
    bwd = lambda i: (nblk - 1 - i, 0)
    bwd_g = lambda i: (nblk - 1 - i, 1)
    blk = lambda m: pl.BlockSpec((tb, w), m)
    n_in, n_out, n_shape = _narrow_specs(list(narrow), nblk)
    for a in narrow_half1:
        h_in, h_out, h_shape = _narrow_specs([a], nblk, width=a.shape[1] // 2, col=1)
        n_in, n_out, n_shape = n_in + h_in, n_out + h_out, n_shape + h_shape
    narrow = tuple(narrow) + tuple(narrow_half1)
    return pl.pallas_call(
        functools.partial(_hgrn_kernel, nsub=tb // sb, nchunk=sb // L, L=L, nlev=nlev,
                          n_narrow=len(narrow)),
        grid=(nblk,),
        in_specs=[blk(fwd), blk(fwd), blk(fwd), blk(bwd), blk(bwd), blk(bwd_g),
                  _const_spec(mall_f.shape), _const_spec(mall_b.shape),
                  _const_spec(trib_f.shape), _const_spec(trib_b.shape),
                  _const_spec(s0_f.shape), _const_spec(s0_b.shape)] + n_in,
        out_specs=[blk(fwd), blk(bwd)] + [pl.BlockSpec(s0_f.shape, lambda i: (0, 0, 0))] * 2 + n_out,
        out_shape=[jax.ShapeDtypeStruct((t, w), F32)] * 2
        + [jax.ShapeDtypeStruct(s0_f.shape, F32)] * 2 + n_shape,
        scratch_shapes=[pltpu.VMEM(s0_f.shape, F32)] * 4,
        compiler_params=_cparams(),
        name="hgrn",
    )(q, v, g, q, v, g, mall_f, mall_b, trib_f, trib_b, s0_f, s0_b, *narrow)


def _merge_kernel(x_ref, er_ref, ec_ref, lng_ref, lnb_ref, mod_ref, wg_ref, bg_ref, of_ref, ob_ref,
                  ng_ref, four_ref, wfp_ref, whp_ref, wo_ref, bo_ref, w1_ref, w2_ref,
                  o_ref, w1b_ref, w2b_ref, *, tm):
    _narrow([w1_ref, w2_ref], [w1b_ref, w2b_ref])
    w4 = N_HEADS * HEAD_DIM
    d = x_ref.shape[1]
    sub = min(tm, SUB_ROWS)
    for r0 in range(0, tm, sub):
        rows = slice(r0, r0 + sub)
        xl, hl = _ln_in_modulated(x_ref, er_ref, ec_ref, lng_ref, lnb_ref, mod_ref, r0, sub)
        hb = hl.astype(BF16)
        og = _dot(hb, wg_ref[:, 0:w4]) + bg_ref[:, 0:w4]
        o = of_ref[rows, :] + ob_ref[rows, :]
        parts = []
        for hd in range(N_HEADS):
            oh = o[:, hd * HEAD_DIM:(hd + 1) * HEAD_DIM]
            ms = jnp.mean(oh * oh, axis=-1, keepdims=True)
            parts.append(oh * lax.rsqrt(ms + RMS_EPS))
        on = jnp.concatenate(parts, axis=1) * ng_ref[...]
        oh = (on * (og * _sigmoid(og))).astype(BF16)
        g_four = _sigmoid(_dot(hb, wg_ref[:, w4:w4 + d]) + bg_ref[:, w4:w4 + d])
        four = jnp.concatenate([four_ref[gi, rows, :] for gi in range(N_GROUPS)], axis=1)
        y = g_four * _dot(four.astype(BF16), wfp_ref[...])
        g_hgrn = _sigmoid(_dot(hb, wg_ref[:, w4 + d:w4 + 2 * d]) + bg_ref[:, w4 + d:w4 + 2 * d])
        y = y + g_hgrn * _dot(oh, whp_ref[...])
        mix = _dot(y.astype(BF16), wo_ref[...]) + bo_ref[...]
        o_ref[rows, :] = ALPHA * xl + mod_ref[2:3, :] * mix


def _merge(x, er, ec, lng, lnb, mod, wg, bg, o_f, o_b, ng, four, wfp, whp, wo, bo, w1, w2, tm):
    t, d = x.shape
    w4 = N_HEADS * HEAD_DIM
    assert wg.shape == (d, w4 + 2 * d) and bg.shape == (1, 2 * wg.shape[1])
    nrow = tm // GRID_W
    row = lambda i: (i, 0)
    consts = [ec, lng, lnb, mod]
    consts2 = [wfp, whp, wo, bo]
    n_in, n_out, n_shape = _narrow_specs([w1, w2], t // tm)
    return pl.pallas_call(
        functools.partial(_merge_kernel, tm=tm),
        grid=(t // tm,),
        in_specs=[pl.BlockSpec((tm, d), row), pl.BlockSpec((nrow, d // 2), row)]
        + [_const_spec(a.shape) for a in consts]
        + [_const_spec(wg.shape), _const_cols(bg, w4 + 2 * d, 1)]
        + [pl.BlockSpec((tm, w4), row), pl.BlockSpec((tm, w4), row), _const_spec(ng.shape),
           pl.BlockSpec((N_GROUPS, tm, GROUP_DIM), lambda i: (0, i, 0))]
        + [_const_spec(a.shape) for a in consts2] + n_in,
        out_specs=[pl.BlockSpec((tm, d), row)] + n_out,
        out_shape=[jax.ShapeDtypeStruct((t, d), F32)] + n_shape,
        compiler_params=_cparams(),
        name="merge",
    )(x, er, *consts, wg, bg, o_f, o_b, ng, four, *consts2, w1, w2)


def _mlp_kernel(x_ref, mod_ref, ag_ref, ab_ref, w1_ref, b1_ref, w2_ref, b2_ref, pg_ref, pb_ref, o_ref,
                *, nsplit):
    dff = w1_ref.shape[1]
    cw = dff // nsplit
    tm = x_ref.shape[0]
    sub = min(tm, SUB_ROWS)
    for r0 in range(0, tm, sub):
        rows = slice(r0, r0 + sub)
        x1 = _layernorm(x_ref[rows, :], ag_ref[...], ab_ref[...])
        hb = (x1 * (1.0 + mod_ref[4:5, :]) + mod_ref[3:4, :]).astype(BF16)
        acc = jnp.zeros(x1.shape, F32)
        for c in range(nsplit):
            cs = slice(c * cw, (c + 1) * cw)
            a = jnp.maximum(_dot(hb, w1_ref[:, cs]) + b1_ref[:, cs], 0.0)
            acc = acc + _dot((a * a).astype(BF16), w2_ref[cs, :])
        m = acc + b2_ref[...]
        o_ref[rows, :] = _layernorm(ALPHA * x1 + mod_ref[5:6, :] * m, pg_ref[...], pb_ref[...])


def _mlp(z1, mod, ag, ab, w1, b1, w2, b2, pg, pb, tm):
    t, d = z1.shape
    row = lambda i: (i, 0)
    consts = [mod, ag, ab, w1, b1, w2, b2, pg, pb]
    return pl.pallas_call(
        functools.partial(_mlp_kernel, nsplit=4),
        grid=(t // tm,),
        in_specs=[pl.BlockSpec((tm, d), row)] + [_const_spec(a.shape) for a in consts],
        out_specs=pl.BlockSpec((tm, d), row),
        out_shape=jax.ShapeDtypeStruct((t, d), F32),
        compiler_params=_cparams(),
        name="mlp",
    )(z1, *consts)


def _pos_tables(rows, cols, dim):
    quarter = dim // 4
    omega = 1.0 / (POS_BASE ** (np.arange(quarter, dtype=np.float64) / quarter))
    r = np.arange(rows, dtype=np.float64)[:, None] * omega
    cc = np.arange(cols, dtype=np.float64)[:, None] * omega
    er = np.concatenate([np.sin(r), np.cos(r)], axis=-1)
    ec = np.concatenate([np.sin(cc), np.cos(cc)], axis=-1)
    return jnp.asarray(er.astype(np.float32)), jnp.asarray(ec.astype(np.float32))


def _dft_constants(t):
    n = RADIX
    kn = np.outer(np.arange(n), np.arange(n)).astype(np.float64)
    c = np.cos(2.0 * np.pi * kn / n)
    s = np.sin(2.0 * np.pi * kn / n)
    chan = np.concatenate([c, s], axis=0)
    stage1 = np.block([[c, -s], [-s, -c]])
    scale = 1.0 / np.sqrt(float(t) * GROUP_DIM)
    base2 = np.stack([c, s]) * scale
    beta = 2.0 * np.pi * kn / t
    twiddle = np.concatenate([np.cos(beta), np.sin(beta)], axis=1)
    as_f32 = lambda a: jnp.asarray(a.astype(np.float32))
    return as_f32(chan), as_f32(stage1), as_f32(base2), as_f32(twiddle)


def kernel(x, c, ctx, c_ctx, ln_in_g, ln_in_b, w_ada, b_ada, w_in, b_in, hgrn_lb_logits, hgrn_norm_g,
           w_four_proj, w_hgrn_proj, w_out, b_out, w_mlp1, b_mlp1, w_mlp2, b_mlp2, ln_post_g, ln_post_b):
    B, T, D = x.shape
    assert B == 1 and T == RADIX * RADIX and T % GRID_W == 0
    TC = ctx.shape[1]
    w4 = N_HEADS * HEAD_DIM
    row2 = lambda a: a.reshape(1, -1)

    assert w_in.shape[2] == 2 * 5 * w4
    mod_l, mod_c, w_a = _mod_vectors(c[0], c_ctx, w_ada[0], row2(b_ada[0]), w_in[0])
    mod_l = mod_l.reshape(6, D)
    mod_c = mod_c.reshape(6, D)

    er, ec = _pos_tables(T // GRID_W, GRID_W, D)
    lng, lnb = row2(ln_in_g), row2(ln_in_b)
    b_in2 = row2(b_in[0])
    b_a = b_g = b_in2
    l0 = hgrn_lb_logits[:, 0, :].reshape(1, 2 * w4)
    l1 = hgrn_lb_logits[:, 1, :].reshape(1, 2 * w4)
    dft_chan, dft_s1, dft_base2, dft_tw = _dft_constants(T)

    zc = jnp.zeros((TC // GRID_W, D // 2), F32)
    _, qc, vc, gc = _inproj(ctx[0], zc, jnp.zeros_like(ec), lng, lnb, mod_c, w_a, b_a, l0, l1, tm=TC)
    s_zero = jnp.zeros((N_HEADS, HEAD_DIM, HEAD_DIM), F32)
    _, _, s_f, s_b = _hgrn_scan(qc, vc, gc, s_zero, s_zero, tb=TC)

    u, q, v, g = _inproj(x[0], er, ec, lng, lnb, mod_l, w_a, b_a, l0, l1, tm=ROW_TILE)
    four = _dft2(dft_base2, dft_tw, _dft1(dft_chan, dft_s1, u))
    o_f, o_b, _, _, w_fp_b, w_hp_b, w_out_b, w_g = _hgrn_scan(
        q, v, g, s_f, s_b, tb=HGRN_BLOCK, narrow=(w_four_proj[0], w_hgrn_proj[0], w_out[0]),
        narrow_half1=(w_in[0],))

    z1, w1_b, w2_b = _merge(x[0], er, ec, lng, lnb, mod_l, w_g, b_g, o_f, o_b, row2(hgrn_norm_g[0]),
                            four, w_fp_b, w_hp_b, w_out_b, row2(b_out[0]), w_mlp1[0], w_mlp2[0],
                            tm=ROW_TILE)
    out = _mlp(z1, mod_l, row2(ln_post_g[0, 0]), row2(ln_post_b[0, 0]), w1_b, row2(b_mlp1[0]),
               w2_b, row2(b_mlp2[0]), row2(ln_post_g[0, 1]), row2(ln_post_b[0, 1]), tm=ROW_TILE)
    return out[None]
```

```python
import functools

import numpy as np
import jax
import jax.numpy as jnp
from jax import lax
from jax.experimental import pallas as pl
from jax.experimental.pallas import tpu as pltpu

F32 = jnp.float32
BF16 = jnp.bfloat16

GRID_W = 64
N_GROUPS = 4
GROUP_DIM = 128
N_HEADS = 4
HEAD_DIM = 128
POS_BASE = 10000.0
LN_EPS = 1e-5
RMS_EPS = 1e-6
DEPTH = 1
ALPHA = (2.0 * DEPTH) ** 0.25

RADIX = 128
HGRN_CHUNK = 64
HGRN_SUB_BLOCK = 256
HGRN_BLOCK = 512
ROW_TILE = 1024
SUB_ROWS = 256
MOD_COL_TILE = 1536
SAFE_LOG_DECAY = 80.0
VMEM_LIMIT_BYTES = 56 * 1024 * 1024


def _cparams(n_axes=1):
    return pltpu.CompilerParams(dimension_semantics=("arbitrary",) * n_axes,
                                vmem_limit_bytes=VMEM_LIMIT_BYTES)


def _const_spec(shape):
    nd = len(shape)
    return pl.BlockSpec(shape, lambda *_: (0,) * nd, pipeline_mode=pl.Buffered(1))


def _const_cols(arr, width, j):
    return pl.BlockSpec((arr.shape[0], width), lambda *_: (0, j), pipeline_mode=pl.Buffered(1))


def _sigmoid(x):
    return 1.0 / (1.0 + jnp.exp(-x))


def _layernorm(x, g, b):
    mu = jnp.mean(x, axis=-1, keepdims=True)
    xc = x - mu
    var = jnp.mean(xc * xc, axis=-1, keepdims=True)
    return xc * lax.rsqrt(var + LN_EPS) * g + b


def _dot(a, b):
    return jnp.dot(a, b, preferred_element_type=F32)


def _dot_nt(a, b):
    return lax.dot_general(a, b, (((1,), (1,)), ((), ())), preferred_element_type=F32)


def _dot_tn(a, b):
    return lax.dot_general(a, b, (((0,), (0,)), ((), ())), preferred_element_type=F32)


def _narrow_specs(arrays, nsteps, width=None, col=0):
    shape = lambda a: (a.shape[0] // nsteps, width or a.shape[1])
    return ([pl.BlockSpec(shape(a), lambda i, *_: (i, col)) for a in arrays],
            [pl.BlockSpec(shape(a), lambda i, *_: (i, 0)) for a in arrays],
            [jax.ShapeDtypeStruct((a.shape[0], width or a.shape[1]), BF16) for a in arrays])


def _narrow(src_refs, dst_refs):
    for s_ref, d_ref in zip(src_refs, dst_refs):
        d_ref[...] = s_ref[...].astype(BF16)


def _mod_kernel(cl_ref, cx_ref, w_ref, b_ref, win_ref, ol_ref, ox_ref, winb_ref):
    w = w_ref[...]
    for c_ref, o_ref in ((cl_ref, ol_ref), (cx_ref, ox_ref)):
        cs = c_ref[...]
        s = cs * _sigmoid(cs)
        o_ref[...] = jnp.sum(s * w, axis=0, keepdims=True) + b_ref[...]
    _narrow([win_ref], [winb_ref])


def _mod_vectors(c_lat, c_ctx, w_ada, b_ada, w_in):
    d, n = w_ada.shape
    tn = MOD_COL_TILE
    col = lambda j: (0, j)
    n_in, n_out, n_shape = _narrow_specs([w_in], n // tn, width=w_in.shape[1] // 2, col=0)
    return pl.pallas_call(
        _mod_kernel,
        grid=(n // tn,),
        in_specs=[_const_spec((d, 1)), _const_spec((d, 1)),
                  pl.BlockSpec((d, tn), col), pl.BlockSpec((1, tn), col)] + n_in,
        out_specs=[pl.BlockSpec((1, tn), col)] * 2 + n_out,
        out_shape=[jax.ShapeDtypeStruct((1, n), F32)] * 2 + n_shape,
        compiler_params=_cparams(),
        name="mod",
    )(c_lat.reshape(d, 1), c_ctx.reshape(d, 1), w_ada, b_ada, w_in)


def _ln_in_modulated(x_ref, er_ref, ec_ref, lng_ref, lnb_ref, mod_ref, r0, nr):
    x = x_ref[r0:r0 + nr, :]
    half = x.shape[1] // 2
    nrow = nr // GRID_W
    e0 = r0 // GRID_W
    left = jnp.concatenate(
        [jnp.broadcast_to(er_ref[e0 + r:e0 + r + 1, :], (GRID_W, half)) for r in range(nrow)], axis=0)
    right = jnp.concatenate([ec_ref[...]] * nrow, axis=0)
    xp = jnp.concatenate([x[:, :half] + left, x[:, half:] + right], axis=1)
    xl = _layernorm(xp, lng_ref[...], lnb_ref[...])
    hl = xl * (1.0 + mod_ref[1:2, :]) + mod_ref[0:1, :]
    return xl, hl


X_SLOTS = 3


def _inproj_kernel(x_hbm, er_ref, ec_ref, lng_ref, lnb_ref, mod_ref, w_ref, b_ref,
                   l0_ref, l1_ref, u_ref, q_ref, v_ref, g_ref, xbuf_ref, sem_ref, *, tm):
    s = pl.program_id(0)
    nsteps = pl.num_programs(0)

    def x_copy(step):
        slot = step % X_SLOTS
        return pltpu.make_async_copy(x_hbm.at[pl.ds(pl.multiple_of(step * tm, tm), tm), :],
                                     xbuf_ref.at[slot], sem_ref.at[slot])

    @pl.when(s == 0)
    def _():
        x_copy(0).start()

        @pl.when(nsteps > 1)
        def _():
            x_copy(1).start()

    @pl.when(s + 2 < nsteps)
    def _():
        x_copy(s + 2).start()

    x_copy(s).wait()
    x_ref = xbuf_ref.at[s % X_SLOTS]
    w4 = N_GROUPS * GROUP_DIM
    l0 = l0_ref[...]
    l1 = l1_ref[...]
    m = jnp.maximum(l0, l1)
    e0 = jnp.exp(l0 - m)
    lb = e0 / (e0 + jnp.exp(l1 - m))
    sub = min(tm, SUB_ROWS)
    for r0 in range(0, tm, sub):
        rows = slice(r0, r0 + sub)
        _, hl = _ln_in_modulated(x_ref, er_ref, ec_ref, lng_ref, lnb_ref, mod_ref, r0, sub)
        hb = hl.astype(BF16)
        fp = _dot(hb, w_ref[:, 3 * w4:5 * w4]) + b_ref[:, 3 * w4:5 * w4]
        g_ref[rows, :] = jnp.log(lb + (1.0 - lb) * _sigmoid(fp))
        qp = _dot(hb, w_ref[:, w4:2 * w4]) + b_ref[:, w4:2 * w4]
        q_ref[rows, :] = (qp * _sigmoid(qp)).astype(BF16)
        u = (_dot(hb, w_ref[:, 0:w4]) + b_ref[:, 0:w4]).astype(BF16)
        for gi in range(N_GROUPS):
            u_ref[gi, rows, :] = u[:, gi * GROUP_DIM:(gi + 1) * GROUP_DIM]
        v_ref[rows, :] = (_dot(hb, w_ref[:, 2 * w4:3 * w4]) + b_ref[:, 2 * w4:3 * w4]).astype(BF16)


def _inproj(x, er, ec, lng, lnb, mod, w, b, l0, l1, tm):
    t, d = x.shape
    w4 = N_GROUPS * GROUP_DIM
    nrow = tm // GRID_W
    row = lambda i: (i, 0)
    return pl.pallas_call(
        functools.partial(_inproj_kernel, tm=tm),
        grid=(t // tm,),
        in_specs=[pl.BlockSpec(memory_space=pl.ANY),
                  pl.BlockSpec((nrow, d // 2), row),
                  _const_spec(ec.shape), _const_spec(lng.shape), _const_spec(lnb.shape),
                  _const_spec(mod.shape), _const_cols(w, 5 * w4, 0), _const_cols(b, 5 * w4, 0),
                  _const_spec(l0.shape), _const_spec(l1.shape)],
        out_specs=[pl.BlockSpec((N_GROUPS, tm, GROUP_DIM), lambda i: (0, i, 0))]
        + [pl.BlockSpec((tm, w4), row)] * 2 + [pl.BlockSpec((tm, 2 * w4), row)],
        out_shape=[jax.ShapeDtypeStruct((N_GROUPS, t, GROUP_DIM), BF16)]
        + [jax.ShapeDtypeStruct((t, w4), BF16)] * 2 + [jax.ShapeDtypeStruct((t, 2 * w4), F32)],
        scratch_shapes=[pltpu.VMEM((X_SLOTS, tm, d), F32), pltpu.SemaphoreType.DMA((X_SLOTS,))],
        compiler_params=_cparams(),
        name="inproj",
    )(x, er, ec, lng, lnb, mod, w, b, l0, l1)


DFT_BATCH = 16
F32_SUBLANES = 8


def _dft1_kernel(cs_ref, m_ref, u_ref, a_ref, su_ref, sa_ref):
    r, nb, w = u_ref.shape
    hs = F32_SUBLANES
    u = u_ref[...].astype(F32)
    for h in range(nb // hs):
        su_ref[h] = u[:, h * hs:(h + 1) * hs, :].reshape(r * hs, w)
    cols = [su_ref[j // hs, pl.ds(j % hs, r, stride=hs), :].astype(BF16) for j in range(nb)]
    pq = _dot(cs_ref[...].astype(BF16), jnp.concatenate(cols, axis=1))
    for j in range(nb):
        sa_ref[j // hs, pl.ds(j % hs, 2 * r, stride=hs), :] = pq[:, j * w:(j + 1) * w]
    mat = m_ref[...].astype(BF16)
    parts = []
    for h in range(nb // hs):
        rows_pq = jnp.concatenate([sa_ref[h, 0:r * hs, :], sa_ref[h, r * hs:2 * r * hs, :]], axis=1)
        parts.append(_dot(rows_pq.astype(BF16), mat))
    a_ref[0] = jnp.concatenate([p[:, :w].reshape(r, hs, w) for p in parts], axis=1).astype(BF16)
    a_ref[1] = jnp.concatenate([p[:, w:].reshape(r, hs, w) for p in parts], axis=1).astype(BF16)


def _dft1(cs, mat, u):
    ng, t, w = u.shape
    r = RADIX
    nb = 2 * DFT_BATCH
    nh = nb // F32_SUBLANES
    return pl.pallas_call(
        _dft1_kernel,
        grid=(ng, r // nb),
        in_specs=[_const_spec(cs.shape), _const_spec(mat.shape),
                  pl.BlockSpec((None, r, nb, w), lambda g, o: (g, 0, o, 0))],
        out_specs=pl.BlockSpec((2, r, nb, w), lambda g, o: (0, 0, o, g)),
        out_shape=jax.ShapeDtypeStruct((2, r, r, ng * w), BF16),
        scratch_shapes=[pltpu.VMEM((nh, r * F32_SUBLANES, w), F32),
                        pltpu.VMEM((nh, 2 * r * F32_SUBLANES, w), F32)],
        compiler_params=_cparams(2),
        name="dft1",
    )(cs, mat, u.reshape(ng, r, r, w))


def _dft2_kernel(cs_ref, tw_ref, a_ref, o_ref, s_ref):
    nb, r, w = a_ref.shape[1], a_ref.shape[2], a_ref.shape[3]
    hs = F32_SUBLANES
    cos_a, sin_a = cs_ref[0], cs_ref[1]
    for j in range(nb):
        cos_b, sin_b = tw_ref[j:j + 1, 0:r], tw_ref[j:j + 1, r:2 * r]
        gmat = jnp.concatenate([cos_a * cos_b - sin_a * sin_b, sin_a * cos_b + cos_a * sin_b], axis=1)
        a = jnp.concatenate([a_ref[0, j], a_ref[1, j]], axis=0)
        y = _dot(gmat.astype(BF16), a)
        for gi in range(N_GROUPS):
            s_ref[gi, j // hs, pl.ds(j % hs, r, stride=hs), :] = y[:, gi * GROUP_DIM:(gi + 1) * GROUP_DIM]
    for gi in range(N_GROUPS):
        o_ref[gi] = jnp.concatenate([s_ref[gi, h].reshape(r, hs, GROUP_DIM) for h in range(nb // hs)],
                                    axis=1).astype(BF16)


def _dft2(base, twiddle, a4):
    _, r, _, w = a4.shape
    nb = DFT_BATCH
    out = pl.pallas_call(
        _dft2_kernel,
        grid=(r // nb,),
        in_specs=[_const_spec(base.shape),
                  pl.BlockSpec((nb, 2 * r), lambda k: (k, 0)),
                  pl.BlockSpec((2, nb, r, w), lambda k: (0, k, 0, 0))],
        out_specs=pl.BlockSpec((N_GROUPS, r, None, nb, GROUP_DIM), lambda k: (0, 0, k, 0, 0)),
        out_shape=jax.ShapeDtypeStruct((N_GROUPS, r, r // nb, nb, GROUP_DIM), BF16),
        scratch_shapes=[pltpu.VMEM((N_GROUPS, nb // F32_SUBLANES, r * F32_SUBLANES, GROUP_DIM), F32)],
        compiler_params=_cparams(),
        name="dft2",
    )(base, twiddle, a4)
    return out.reshape(N_GROUPS, r * r, GROUP_DIM)


def _hgrn_tables(L, reverse):
    nlev = int(np.log2(L))
    idx = np.arange(L)
    t = idx[:, None]
    i = idx[None, :]
    blocks = [(i >= t) if reverse else (i <= t)]
    for j in range(nlev):
        h = L >> (j + 1)
        mid = (t // (2 * h)) * (2 * h) + h
        upper = t >= mid
        if reverse:
            blk = np.where(upper, (i >= mid) & (i < t), (i >= t) & (i < mid))
        else:
            blk = np.where(upper, (i >= mid) & (i <= t), (i > t) & (i < mid))
        blocks.append(blk)
    blocks.append((i < t) if reverse else (i > t))
    return np.concatenate(blocks, axis=0).astype(np.float32), nlev


def _as_column(row):
    n = row.shape[1]
    return jnp.broadcast_to(row, (n, n)).T


def _split_hi_lo(g):
    hi = g.astype(BF16)
    return hi, (g - hi.astype(F32)).astype(BF16)


def _hgrn_chunk_exact(q_ref, v_ref, g_ref, o_ref, st_ref, rows, mall, pair_masks, query_rows,
                      L, nlev, reverse):
    last = 0 if reverse else L - 1
    g = g_ref[rows, :]
    g_hi, g_lo = _split_hi_lo(g)
    ex = jnp.exp(_dot(mall, g_hi) + _dot(mall, g_lo))
    q = q_ref[rows, :].astype(F32)
    v = v_ref[rows, :]
    k = 1.0 - jnp.exp(g)
    e_cum = ex[0:L]
    qe = (q * e_cum).astype(BF16)
    ke = (k * ex[(nlev + 1) * L:(nlev + 2) * L]).astype(BF16)
    zs = [(jnp.where(query_rows[j], q, k) * ex[(j + 1) * L:(j + 2) * L]).astype(BF16)
          for j in range(nlev)]
    qk = q * k
    e_last = e_cum[last:last + 1, :]
    for hd in range(N_HEADS):
        sl = slice(hd * HEAD_DIM, (hd + 1) * HEAD_DIM)
        sc = jnp.zeros((L, L), F32)
        for j in range(nlev):
            zj = zs[j][:, sl]
            sc = jnp.where(pair_masks[j], _dot_nt(zj, zj), sc)
        st = st_ref[hd]
        vh = v[:, sl]
        o = _dot(sc.astype(BF16), vh) + _dot(qe[:, sl], st.astype(BF16))
        o = o + jnp.sum(qk[:, sl], axis=-1, keepdims=True) * vh.astype(F32)
        o_ref[rows, sl] = o
        st_ref[hd] = st * _as_column(e_last[:, sl]) + _dot_tn(ke[:, sl], vh)


def _hgrn_block_fast(q_ref, v_ref, g_ref, trib_ref, o_ref, st_ref, r0, nchunk, L, reverse):
    half = L // 2
    tb = nchunk * L
    blk = slice(r0, r0 + tb)
    chunks = []
    for c in range(nchunk):
        base = c * L
        if reverse:
            chunks.append((slice(base, base + L), slice(base + half, base + L),
                           slice(base, base + half), base + half, base))
        else:
            chunks.append((slice(base, base + L), slice(base, base + half),
                           slice(base + half, base + L), base + half - 1, base + L - 1))
    g = g_ref[blk, :]
    b = _dot(trib_ref[...], g.astype(BF16))
    q = q_ref[blk, :].astype(F32)
    v = v_ref[blk, :]
    k = 1.0 - jnp.exp(g)
    e_b = jnp.exp(b)
    qe = (q * e_b).astype(BF16)
    c2 = [b[sec] - b[edge:edge + 1, :] for (_, _, sec, edge, _) in chunks]
    q2 = jnp.concatenate([q[ch[2]] * jnp.exp(c2[c]) for c, ch in enumerate(chunks)], axis=0)
    own = []
    for c, (_, fst, _, _, _) in enumerate(chunks):
        own += [c2[c], b[fst]] if reverse else [b[fst], c2[c]]
    kh = (k * jnp.exp(-jnp.concatenate(own, axis=0))).astype(BF16)
    tail = jnp.concatenate([b[last:last + 1, :] - b[rows] for (rows, _, _, _, last) in chunks], axis=0)
    ke = (k * jnp.exp(tail)).astype(BF16)
    lhs = jnp.concatenate([qe, q2.astype(BF16)], axis=0)

    row = lax.broadcasted_iota(jnp.int32, (tb, tb), 0)
    col = lax.broadcasted_iota(jnp.int32, (tb, tb), 1)
    valid = (row // L == col // L) & ((col >= row) if reverse else (col <= row))
    if reverse:
        use_near = (row % L < half) & (col % L < half)
    else:
        use_near = (row % L >= half) & (col % L >= half)

    for hd in range(N_HEADS):
        sl = slice(hd * HEAD_DIM, (hd + 1) * HEAD_DIM)
        s_all = _dot_nt(lhs[:, sl], kh[:, sl])
        far = s_all[0:tb]
        pieces = []
        for c, (_, fst, _, _, _) in enumerate(chunks):
            near_c = s_all[tb + c * half:tb + (c + 1) * half]
            pieces += [near_c, far[fst]] if reverse else [far[fst], near_c]
        near = jnp.concatenate(pieces, axis=0)
        sc = jnp.where(valid, jnp.where(use_near, near, far), 0.0).astype(BF16)
        vh = v[:, sl]
        o_intra = _dot(sc, vh)
        upd = [_dot_tn(ke[rows, sl], vh[rows]) for (rows, _, _, _, _) in chunks]
        st = st_ref[hd]
        o_inter = [None] * nchunk
        for c in (range(nchunk - 1, -1, -1) if reverse else range(nchunk)):
            rows, _, _, _, last = chunks[c]
            o_inter[c] = _dot(qe[rows, sl], st.astype(BF16))
            st = st * _as_column(e_b[last:last + 1, sl]) + upd[c]
        st_ref[hd] = st
        o_ref[blk, sl] = o_intra + jnp.concatenate(o_inter, axis=0)

    leaf_decay = []
    for (_, _, _, edge, last) in chunks:
        leaf_decay += [b[edge:edge + 1, :], b[last:last + 1, :] - b[edge:edge + 1, :]]
    return functools.reduce(jnp.minimum, leaf_decay)


def _hgrn_block_exact(q_ref, v_ref, g_ref, mall_ref, o_ref, st_ref, nchunk, L, nlev, reverse):
    row = lax.broadcasted_iota(jnp.int32, (L, L), 0)
    col = lax.broadcasted_iota(jnp.int32, (L, L), 1)
    rowc = lax.broadcasted_iota(jnp.int32, (L, N_HEADS * HEAD_DIM), 0)
    pair_masks, query_rows = [], []
    for j in range(nlev):
        h = L >> (j + 1)
        same = (row // (2 * h)) == (col // (2 * h))
        row_up = (row // h) % 2 == 1
        col_up = (col // h) % 2 == 1
        if reverse:
            pair_masks.append(same & jnp.logical_not(row_up) & col_up)
            query_rows.append((rowc // h) % 2 == 0)
        else:
            pair_masks.append(same & row_up & jnp.logical_not(col_up))
            query_rows.append((rowc // h) % 2 == 1)
    mall = mall_ref[...]
    for c in (range(nchunk - 1, -1, -1) if reverse else range(nchunk)):
        _hgrn_chunk_exact(q_ref, v_ref, g_ref, o_ref, st_ref, slice(c * L, (c + 1) * L),
                          mall, pair_masks, query_rows, L, nlev, reverse)


def _hgrn_kernel(*refs, nsub, nchunk, L, nlev, n_narrow):
    (qf_ref, vf_ref, gf_ref, qb_ref, vb_ref, gb_ref, mallf_ref, mallb_ref,
     tribf_ref, tribb_ref, s0f_ref, s0b_ref) = refs[:12]
    narrow_in = refs[12:12 + n_narrow]
    of_ref, ob_ref, sff_ref, sfb_ref = refs[12 + n_narrow:16 + n_narrow]
    narrow_out = refs[16 + n_narrow:16 + 2 * n_narrow]
    stf_ref, stb_ref, keepf_ref, keepb_ref = refs[16 + 2 * n_narrow:]
    _narrow(narrow_in, narrow_out)
    i = pl.program_id(0)

    @pl.when(i == 0)
    def _():
        stf_ref[...] = s0f_ref[...]
        stb_ref[...] = s0b_ref[...]

    keepf_ref[...] = stf_ref[...]
    keepb_ref[...] = stb_ref[...]
    leaf_decay = []
    for j in range(nsub):
        leaf_decay.append(_hgrn_block_fast(qf_ref, vf_ref, gf_ref, tribf_ref, of_ref, stf_ref,
                                           j * nchunk * L, nchunk, L, False))
        leaf_decay.append(_hgrn_block_fast(qb_ref, vb_ref, gb_ref, tribb_ref, ob_ref, stb_ref,
                                           (nsub - 1 - j) * nchunk * L, nchunk, L, True))
    safe = jnp.min(functools.reduce(jnp.minimum, leaf_decay)) >= -SAFE_LOG_DECAY

    @pl.when(jnp.logical_not(safe))
    def _():
        stf_ref[...] = keepf_ref[...]
        stb_ref[...] = keepb_ref[...]
        nc = nsub * nchunk
        _hgrn_block_exact(qf_ref, vf_ref, gf_ref, mallf_ref, of_ref, stf_ref, nc, L, nlev, False)
        _hgrn_block_exact(qb_ref, vb_ref, gb_ref, mallb_ref, ob_ref, stb_ref, nc, L, nlev, True)

    @pl.when(i == pl.num_programs(0) - 1)
    def _():
        sff_ref[...] = stf_ref[...]
        sfb_ref[...] = stb_ref[...]


def _hgrn_scan(q, v, g, s0_f, s0_b, tb, narrow=(), narrow_half1=()):
    t, w = q.shape
    L = HGRN_CHUNK
    nblk = t // tb
    sb = min(tb, HGRN_SUB_BLOCK)
    consts = []
    for reverse in (False, True):
        mall_np, nlev = _hgrn_tables(L, reverse)
        consts.append((jnp.asarray(mall_np, dtype=BF16),
                       jnp.asarray(np.kron(np.eye(sb // L, dtype=np.float32), mall_np[0:L]), dtype=BF16)))
    (mall_f, trib_f), (mall_b, trib_b) = consts
    fwd = lambda i: (i, 0)
    bwd = lambda i: (nblk - 1 - i, 0)
    bwd_g = lambda i: (nblk - 1 - i, 1)
    blk = lambda m: pl.BlockSpec((tb, w), m)
    n_in, n_out, n_shape = _narrow_specs(list(narrow), nblk)
    for a in narrow_half1:
        h_in, h_out, h_shape = _narrow_specs([a], nblk, width=a.shape[1] // 2, col=1)
        n_in, n_out, n_shape = n_in + h_in, n_out + h_out, n_shape + h_shape
    narrow = tuple(narrow) + tuple(narrow_half1)
    return pl.pallas_call(
        functools.partial(_hgrn_kernel, nsub=tb // sb, nchunk=sb // L, L=L, nlev=nlev,
                          n_narrow=len(narrow)),
        grid=(nblk,),
        in_specs=[blk(fwd), blk(fwd), blk(fwd), blk(bwd), blk(bwd), blk(bwd_g),
                  _const_spec(mall_f.shape), _const_spec(mall_b.shape),
                  _const_spec(trib_f.shape), _const_spec(trib_b.shape),
                  _const_spec(s0_f.shape), _const_spec(s0_b.shape)] + n_in,
        out_specs=[blk(fwd), blk(bwd)] + [pl.BlockSpec(s0_f.shape, lambda i: (0, 0, 0))] * 2 + n_out,
        out_shape=[jax.ShapeDtypeStruct((t, w), F32)] * 2
        + [jax.ShapeDtypeStruct(s0_f.shape, F32)] * 2 + n_shape,
        scratch_shapes=[pltpu.VMEM(s0_f.shape, F32)] * 4,
        compiler_params=_cparams(),
        name="hgrn",
    )(q, v, g, q, v, g, mall_f, mall_b, trib_f, trib_b, s0_f, s0_b, *narrow)


def _merge_kernel(x_ref, er_ref, ec_ref, lng_ref, lnb_ref, mod_ref, wg_ref, bg_ref, of_ref, ob_ref,
                  ng_ref, four_ref, wfp_ref, whp_ref, wo_ref, bo_ref, w1_ref, w2_ref,
                  o_ref, w1b_ref, w2b_ref, *, tm):
    _narrow([w1_ref, w2_ref], [w1b_ref, w2b_ref])
    w4 = N_HEADS * HEAD_DIM
    d = x_ref.shape[1]
    sub = min(tm, SUB_ROWS)
    for r0 in range(0, tm, sub):
        rows = slice(r0, r0 + sub)
        xl, hl = _ln_in_modulated(x_ref, er_ref, ec_ref, lng_ref, lnb_ref, mod_ref, r0, sub)
        hb = hl.astype(BF16)
        og = _dot(hb, wg_ref[:, 0:w4]) + bg_ref[:, 0:w4]
        o = of_ref[rows, :] + ob_ref[rows, :]
        parts = []
        for hd in range(N_HEADS):
            oh = o[:, hd * HEAD_DIM:(hd + 1) * HEAD_DIM]
            ms = jnp.mean(oh * oh, axis=-1, keepdims=True)
            parts.append(oh * lax.rsqrt(ms + RMS_EPS))
        on = jnp.concatenate(parts, axis=1) * ng_ref[...]
        oh = (on * (og * _sigmoid(og))).astype(BF16)
        g_four = _sigmoid(_dot(hb, wg_ref[:, w4:w4 + d]) + bg_ref[:, w4:w4 + d])
        four = jnp.concatenate([four_ref[gi, rows, :] for gi in range(N_GROUPS)], axis=1)
        y = g_four * _dot(four.astype(BF16), wfp_ref[...])
        g_hgrn = _sigmoid(_dot(hb, wg_ref[:, w4 + d:w4 + 2 * d]) + bg_ref[:, w4 + d:w4 + 2 * d])
        y = y + g_hgrn * _dot(oh, whp_ref[...])
        mix = _dot(y.astype(BF16), wo_ref[...]) + bo_ref[...]
        o_ref[rows, :] = ALPHA * xl + mod_ref[2:3, :] * mix


def _merge(x, er, ec, lng, lnb, mod, wg, bg, o_f, o_b, ng, four, wfp, whp, wo, bo, w1, w2, tm):
    t, d = x.shape
    w4 = N_HEADS * HEAD_DIM
    assert wg.shape == (d, w4 + 2 * d) and bg.shape == (1, 2 * wg.shape[1])
    nrow = tm // GRID_W
    row = lambda i: (i, 0)
    consts = [ec, lng, lnb, mod]
    consts2 = [wfp, whp, wo, bo]
    n_in, n_out, n_shape = _narrow_specs([w1, w2], t // tm)
    return pl.pallas_call(
        functools.partial(_merge_kernel, tm=tm),
        grid=(t // tm,),
        in_specs=[pl.BlockSpec((tm, d), row), pl.BlockSpec((nrow, d // 2), row)]
        + [_const_spec(a.shape) for a in consts]
        + [_const_spec(wg.shape), _const_cols(bg, w4 + 2 * d, 1)]
        + [pl.BlockSpec((tm, w4), row), pl.BlockSpec((tm, w4), row), _const_spec(ng.shape),
           pl.BlockSpec((N_GROUPS, tm, GROUP_DIM), lambda i: (0, i, 0))]
        + [_const_spec(a.shape) for a in consts2] + n_in,
        out_specs=[pl.BlockSpec((tm, d), row)] + n_out,
        out_shape=[jax.ShapeDtypeStruct((t, d), F32)] + n_shape,
        compiler_params=_cparams(),
        name="merge",
    )(x, er, *consts, wg, bg, o_f, o_b, ng, four, *consts2, w1, w2)


def _mlp_kernel(x_ref, mod_ref, ag_ref, ab_ref, w1_ref, b1_ref, w2_ref, b2_ref, pg_ref, pb_ref, o_ref,
                *, nsplit):
    dff = w1_ref.shape[1]
    cw = dff // nsplit
    tm = x_ref.shape[0]
    sub = min(tm, SUB_ROWS)
    for r0 in range(0, tm, sub):
        rows = slice(r0, r0 + sub)
        x1 = _layernorm(x_ref[rows, :], ag_ref[...], ab_ref[...])
        hb = (x1 * (1.0 + mod_ref[4:5, :]) + mod_ref[3:4, :]).astype(BF16)
        acc = jnp.zeros(x1.shape, F32)
        for c in range(nsplit):
            cs = slice(c * cw, (c + 1) * cw)
            a = jnp.maximum(_dot(hb, w1_ref[:, cs]) + b1_ref[:, cs], 0.0)
            acc = acc + _dot((a * a).astype(BF16), w2_ref[cs, :])
        m = acc + b2_ref[...]
        o_ref[rows, :] = _layernorm(ALPHA * x1 + mod_ref[5:6, :] * m, pg_ref[...], pb_ref[...])


def _mlp(z1, mod, ag, ab, w1, b1, w2, b2, pg, pb, tm):
    t, d = z1.shape
    row = lambda i: (i, 0)
    consts = [mod, ag, ab, w1, b1, w2, b2, pg, pb]
    return pl.pallas_call(
        functools.partial(_mlp_kernel, nsplit=4),
        grid=(t // tm,),
        in_specs=[pl.BlockSpec((tm, d), row)] + [_const_spec(a.shape) for a in consts],
        out_specs=pl.BlockSpec((tm, d), row),
        out_shape=jax.ShapeDtypeStruct((t, d), F32),
        compiler_params=_cparams(),
        name="mlp",
    )(z1, *consts)


def _pos_tables(rows, cols, dim):
    quarter = dim // 4
    omega = 1.0 / (POS_BASE ** (np.arange(quarter, dtype=np.float64) / quarter))
    r = np.arange(rows, dtype=np.float64)[:, None] * omega
    cc = np.arange(cols, dtype=np.float64)[:, None] * omega
    er = np.concatenate([np.sin(r), np.cos(r)], axis=-1)
    ec = np.concatenate([np.sin(cc), np.cos(cc)], axis=-1)
    return jnp.asarray(er.astype(np.float32)), jnp.asarray(ec.astype(np.float32))


def _dft_constants(t):
    n = RADIX
    kn = np.outer(np.arange(n), np.arange(n)).astype(np.float64)
    c = np.cos(2.0 * np.pi * kn / n)
    s = np.sin(2.0 * np.pi * kn / n)
    chan = np.concatenate([c, s], axis=0)
    stage1 = np.block([[c, -s], [-s, -c]])
    scale = 1.0 / np.sqrt(float(t) * GROUP_DIM)
    base2 = np.stack([c, s]) * scale
    beta = 2.0 * np.pi * kn / t
    twiddle = np.concatenate([np.cos(beta), np.sin(beta)], axis=1)
    as_f32 = lambda a: jnp.asarray(a.astype(np.float32))
    return as_f32(chan), as_f32(stage1), as_f32(base2), as_f32(twiddle)


def kernel(x, c, ctx, c_ctx, ln_in_g, ln_in_b, w_ada, b_ada, w_in, b_in, hgrn_lb_logits, hgrn_norm_g,
           w_four_proj, w_hgrn_proj, w_out, b_out, w_mlp1, b_mlp1, w_mlp2, b_mlp2, ln_post_g, ln_post_b):
    B, T, D = x.shape
    assert B == 1 and T == RADIX * RADIX and T % GRID_W == 0
    TC = ctx.shape[1]
    w4 = N_HEADS * HEAD_DIM
    row2 = lambda a: a.reshape(1, -1)

    assert w_in.shape[2] == 2 * 5 * w4
    mod_l, mod_c, w_a = _mod_vectors(c[0], c_ctx, w_ada[0], row2(b_ada[0]), w_in[0])
    mod_l = mod_l.reshape(6, D)
    mod_c = mod_c.reshape(6, D)

    er, ec = _pos_tables(T // GRID_W, GRID_W, D)
    lng, lnb = row2(ln_in_g), row2(ln_in_b)
    b_in2 = row2(b_in[0])
    b_a = b_g = b_in2
    l0 = hgrn_lb_logits[:, 0, :].reshape(1, 2 * w4)
    l1 = hgrn_lb_logits[:, 1, :].reshape(1, 2 * w4)
    dft_chan, dft_s1, dft_base2, dft_tw = _dft_constants(T)

    zc = jnp.zeros((TC // GRID_W, D // 2), F32)
    _, qc, vc, gc = _inproj(ctx[0], zc, jnp.zeros_like(ec), lng, lnb, mod_c, w_a, b_a, l0, l1, tm=TC)
    s_zero = jnp.zeros((N_HEADS, HEAD_DIM, HEAD_DIM), F32)
    _, _, s_f, s_b = _hgrn_scan(qc, vc, gc, s_zero, s_zero, tb=TC)

    u, q, v, g = _inproj(x[0], er, ec, lng, lnb, mod_l, w_a, b_a, l0, l1, tm=ROW_TILE)
    four = _dft2(dft_base2, dft_tw, _dft1(dft_chan, dft_s1, u))
    o_f, o_b, _, _, w_fp_b, w_hp_b, w_out_b, w_g = _hgrn_scan(
        q, v, g, s_f, s_b, tb=HGRN_BLOCK, narrow=(w_four_proj[0], w_hgrn_proj[0], w_out[0]),
        narrow_half1=(w_in[0],))

    z1, w1_b, w2_b = _merge(x[0], er, ec, lng, lnb, mod_l, w_g, b_g, o_f, o_b, row2(hgrn_norm_g[0]),
                            four, w_fp_b, w_hp_b, w_out_b, row2(b_out[0]), w_mlp1[0], w_mlp2[0],
                            tm=ROW_TILE)
    out = _mlp(z1, mod_l, row2(ln_post_g[0, 0]), row2(ln_post_b[0, 0]), w1_b, row2(b_mlp1[0]),
               w2_b, row2(b_mlp2[0]), row2(ln_post_g[0, 1]), row2(ln_post_b[0, 1]), tm=ROW_TILE)
    return out[None]
```
